```python
import math
import jax, jax.numpy as jnp
from jax import lax
import numpy as np

D_MODEL = 2048
BATCH = 2
SEQ = 4096
DEPTH = 2
DEC_BATCH = 8
DEC_SEQ = 8
PAST_LEN = 16384
PAGE_SIZE = 128

A_WIDTH = D_MODEL // 4
A_WINDOWS = (2, 4, 8, 16)
A_GROUPS = len(A_WINDOWS)
A_GROUP_DIM = A_WIDTH // A_GROUPS
A_BUF = max(A_WINDOWS) - 1
B_WIDTH = 3 * D_MODEL // 8
B_HEADS = 4
B_DV = B_WIDTH // B_HEADS
B_DK = B_DV // 2
B_KWIDTH = B_HEADS * B_DK
B_GATE_RANK = 16
B_GATE_TEMP = 16.0
B_CHUNK = 32
C_WIDTH = 3 * D_MODEL // 8
C_HEAD_DIM = 128
C_HEADS = C_WIDTH // C_HEAD_DIM
Q_BLOCK = 128
FORGET_BIAS_INIT = 3.0
N_BRANCH = 3
EPS = 1e-6

_SPLIT_SIZES = (A_WIDTH, A_WIDTH, B_KWIDTH, B_KWIDTH, B_WIDTH, B_WIDTH, B_GATE_RANK,
                C_WIDTH, C_WIDTH, C_WIDTH, C_WIDTH, C_HEADS, N_BRANCH * D_MODEL)
SPLIT_POINTS = tuple(int(p) for p in np.cumsum(_SPLIT_SIZES)[:-1])
D_IN = int(sum(_SPLIT_SIZES))

kernel_name = 'hybrid_pool_gla_fox_step'


def rms_norm(x, g):
    xf = x.astype(jnp.float32)
    r = lax.rsqrt(jnp.mean(xf * xf, axis=-1, keepdims=True) + EPS)
    return (xf * r).astype(x.dtype) * g.astype(x.dtype)


def pool_mixer(u, buf, start_pos, pool_w, pool_scale):
    B, T, _ = u.shape
    full = jnp.concatenate([buf.astype(u.dtype), u], axis=1)
    cs = jnp.cumsum(full.astype(jnp.float32), axis=1)
    cs = jnp.concatenate([jnp.zeros((B, 1, A_WIDTH), jnp.float32), cs], axis=1)
    pos = (start_pos + jnp.arange(T)).astype(jnp.float32)
    means = []
    for g, w in enumerate(A_WINDOWS):
        lo, hi = g * A_GROUP_DIM, (g + 1) * A_GROUP_DIM
        win_sum = cs[:, A_BUF + 1:A_BUF + 1 + T, lo:hi] - cs[:, A_BUF + 1 - w:A_BUF + 1 - w + T, lo:hi]
        count = jnp.minimum(pos + 1.0, float(w))[None, :, None]
        means.append(win_sum / count)
    pooled = jnp.concatenate(means, axis=-1).astype(u.dtype) - u
    mixed = jnp.einsum('btgc,gcd->btgd', pooled.reshape(B, T, A_GROUPS, A_GROUP_DIM), pool_w)
    return mixed.reshape(B, T, A_WIDTH) * pool_scale, full[:, -A_BUF:]


def gla_scan(q, k, v, log_a, s0):
    B, T, H, DK = q.shape
    DV = v.shape[-1]
    C = math.gcd(T, B_CHUNK)
    N = T // C

    def chunks(a):
        return a.reshape(B, N, C, H, a.shape[-1]).astype(jnp.float32)

    qc, kc, vc, la = chunks(q), chunks(k), chunks(v), chunks(log_a)
    b = jnp.cumsum(la, axis=2)
    b_last = b[:, :, -1:]
    q_t = qc * jnp.exp(b)
    k_t = kc * jnp.exp(-b)
    k_end = kc * jnp.exp(b_last - b)
    mask = jnp.tril(jnp.ones((C, C), dtype=bool))
    att = jnp.where(mask, jnp.einsum('bnihk,bnjhk->bnhij', q_t, k_t), 0.0)
    o_intra = jnp.einsum('bnhij,bnjhv->bnihv', att, vc)
    kv = jnp.einsum('bnjhk,bnjhv->bnhkv', k_end, vc)
    decay = jnp.exp(b_last[:, :, 0])

    def step(S, inp):
        q_n, kv_n, d_n = inp
        o = jnp.einsum('bihk,bhkv->bihv', q_n, S)
        return d_n[..., None] * S + kv_n, o

    S, o_inter = lax.scan(step, s0.astype(jnp.float32),
                          (jnp.moveaxis(q_t, 1, 0), jnp.moveaxis(kv, 1, 0), jnp.moveaxis(decay, 1, 0)))
    o = o_intra + jnp.moveaxis(o_inter, 0, 1)
    return o.reshape(B, T, H, DV).astype(q.dtype), S.astype(s0.dtype)


def fox_attention(q, k, v, cq, ck, q_pos, k_pos):
    B, Tq, H, D = q.shape
    blk = Q_BLOCK if Tq % Q_BLOCK == 0 else Tq
    nb = Tq // blk
    scale = D ** -0.5
    ckh = jnp.moveaxis(ck, 1, 2)

    def one_block(args):
        qb, cqb, pb = args
        s = jnp.einsum('bqhd,bkhd->bhqk', qb, k).astype(jnp.float32) * scale
        s = s + (jnp.moveaxis(cqb, 1, 2)[..., :, None] - ckh[..., None, :])
        s = jnp.where(pb[:, None] >= k_pos[None, :], s, -jnp.inf)
        p = jax.nn.softmax(s, axis=-1)
        return jnp.einsum('bhqk,bkhd->bqhd', p.astype(v.dtype), v)

    qs = q.reshape(B, nb, blk, H, D).swapaxes(0, 1)
    cqs = cq.reshape(B, nb, blk, H).swapaxes(0, 1)
    ps = q_pos.reshape(nb, blk)
    out = lax.map(one_block, (qs, cqs, ps))
    return out.swapaxes(0, 1).reshape(B, Tq, H, D)


def mixer_layer(x, pool_buf, gla_state, past, start_pos, norm_g, w_in, pool_w, pool_scale,
                gla_w_a2, gla_b_a, gla_norm, fox_b_f, fox_q_norm, fox_k_norm,
                w_branch_a, w_branch_b, w_branch_c, b_merge, w_out):
    B, T, _ = x.shape
    h = rms_norm(x, norm_g)
    proj = jnp.einsum('btd,de->bte', h, w_in)
    (a_u, a_g, b_q, b_k, b_v, b_g, b_r, c_q, c_k, c_v, c_g, c_f, m_g) = jnp.split(proj, SPLIT_POINTS, axis=-1)

    a_o, pool_new = pool_mixer(a_u, pool_buf, start_pos, pool_w, pool_scale)
    a_o = a_o * jax.nn.silu(a_g)

    q = b_q.reshape(B, T, B_HEADS, B_DK) * (B_DK ** -0.5)
    k = b_k.reshape(B, T, B_HEADS, B_DK)
    v = b_v.reshape(B, T, B_HEADS, B_DV)
    gate_logit = (jnp.einsum('btr,rk->btk', b_r, gla_w_a2) + gla_b_a).astype(jnp.float32)
    log_a = (jax.nn.log_sigmoid(gate_logit) / B_GATE_TEMP).reshape(B, T, B_HEADS, B_DK)
    b_o, gla_new = gla_scan(q, k, v, log_a, gla_state)
    b_o = rms_norm(b_o, gla_norm.reshape(B_HEADS, B_DV)).reshape(B, T, B_WIDTH) * jax.nn.silu(b_g)

    qc = rms_norm(c_q.reshape(B, T, C_HEADS, C_HEAD_DIM), fox_q_norm)
    kc = rms_norm(c_k.reshape(B, T, C_HEADS, C_HEAD_DIM), fox_k_norm)
    vc = c_v.reshape(B, T, C_HEADS, C_HEAD_DIM)
    logf = jax.nn.log_sigmoid((c_f + fox_b_f).astype(jnp.float32))
    q_pos = start_pos + jnp.arange(T)
    if past is None:
        c_new = jnp.cumsum(logf, axis=1)
        k_all, v_all, c_all, k_pos = kc, vc, c_new, q_pos
    else:
        k_past, v_past, logf_past = past
        c_past = jnp.cumsum(logf_past.astype(jnp.float32), axis=1)
        c_new = c_past[:, -1:] + jnp.cumsum(logf, axis=1)
        k_all = jnp.concatenate([k_past.astype(kc.dtype), kc], axis=1)
        v_all = jnp.concatenate([v_past.astype(vc.dtype), vc], axis=1)
        c_all = jnp.concatenate([c_past, c_new], axis=1)
        k_pos = jnp.arange(k_all.shape[1])
    c_o = fox_attention(qc, k_all, v_all, c_new, c_all, q_pos, k_pos)
    c_o = c_o.reshape(B, T, C_WIDTH) * jax.nn.silu(c_g)

    gates = jax.nn.sigmoid(m_g + b_merge).reshape(B, T, N_BRANCH, D_MODEL)
    merged = (gates[:, :, 0] * jnp.einsum('btc,cd->btd', a_o, w_branch_a)
              + gates[:, :, 1] * jnp.einsum('btc,cd->btd', b_o, w_branch_b)
              + gates[:, :, 2] * jnp.einsum('btc,cd->btd', c_o, w_branch_c))
    y = x + jnp.einsum('btd,de->bte', merged, w_out)
    return y, (kc, vc, logf.astype(x.dtype), gla_new, pool_new)


def setup_inputs(seed: int = 0) -> dict:
    key = jax.random.key(seed)
    ks = jax.random.split(key, 24)
    n_pages = PAST_LEN // PAGE_SIZE
    n_used = DEC_BATCH * n_pages
    n_pool = n_used + n_used // 4
    f32 = jnp.float32

    def nrm(k, shape, s):
        return jax.random.normal(k, shape, f32) * s

    x_prompt = nrm(ks[0], (BATCH, SEQ, D_MODEL), 1.0)
    x_sample = nrm(ks[1], (DEC_BATCH, DEC_SEQ, D_MODEL), 1.0)
    cache_k = nrm(ks[2], (DEPTH, n_pool, PAGE_SIZE, C_HEADS, C_HEAD_DIM), 1.0)
    cache_v = nrm(ks[3], (DEPTH, n_pool, PAGE_SIZE, C_HEADS, C_HEAD_DIM), 1.0)
    cache_logf = jax.nn.log_sigmoid(FORGET_BIAS_INIT + nrm(ks[4], (DEPTH, n_pool, PAGE_SIZE, C_HEADS), 1.0))
    state_gla = nrm(ks[5], (DEPTH, DEC_BATCH, B_HEADS, B_DK, B_DV), 1.0)
    state_pool = nrm(ks[6], (DEPTH, DEC_BATCH, A_BUF, A_WIDTH), 1.0)
    page_table = jax.random.permutation(ks[7], n_pool)[:n_used].reshape(DEC_BATCH, n_pages).astype(jnp.int32)
    norm_g = 1.0 + nrm(ks[8], (DEPTH, D_MODEL), 0.05)
    w_in = nrm(ks[9], (DEPTH, D_MODEL, D_IN), D_MODEL ** -0.5)
    pool_w = nrm(ks[10], (DEPTH, A_GROUPS, A_GROUP_DIM, A_GROUP_DIM), A_GROUP_DIM ** -0.5)
    pool_scale = 1.0 + nrm(ks[11], (DEPTH, A_WIDTH), 0.05)
    gla_w_a2 = nrm(ks[12], (DEPTH, B_GATE_RANK, B_KWIDTH), B_GATE_RANK ** -0.5)
    gla_b_a = nrm(ks[13], (DEPTH, B_KWIDTH), 0.1)
    gla_norm = 1.0 + nrm(ks[14], (DEPTH, B_WIDTH), 0.05)
    fox_b_f = FORGET_BIAS_INIT + nrm(ks[15], (DEPTH, C_HEADS), 0.1)
    fox_q_norm = 1.0 + nrm(ks[16], (DEPTH, C_HEAD_DIM), 0.05)
    fox_k_norm = 1.0 + nrm(ks[17], (DEPTH, C_HEAD_DIM), 0.05)
    w_branch_a = nrm(ks[18], (DEPTH, A_WIDTH, D_MODEL), A_WIDTH ** -0.5)
    w_branch_b = nrm(ks[19], (DEPTH, B_WIDTH, D_MODEL), B_WIDTH ** -0.5)
    w_branch_c = nrm(ks[20], (DEPTH, C_WIDTH, D_MODEL), C_WIDTH ** -0.5)
    b_merge = nrm(ks[21], (DEPTH, N_BRANCH * D_MODEL), 0.1)
    w_out = nrm(ks[22], (DEPTH, D_MODEL, D_MODEL), D_MODEL ** -0.5)
    return {'x_prompt': x_prompt, 'x_sample': x_sample, 'cache_k': cache_k, 'cache_v': cache_v,
            'cache_logf': cache_logf, 'state_gla': state_gla, 'state_pool': state_pool,
            'page_table': page_table, 'norm_g': norm_g, 'w_in': w_in, 'pool_w': pool_w,
            'pool_scale': pool_scale, 'gla_w_a2': gla_w_a2, 'gla_b_a': gla_b_a, 'gla_norm': gla_norm,
            'fox_b_f': fox_b_f, 'fox_q_norm': fox_q_norm, 'fox_k_norm': fox_k_norm,
            'w_branch_a': w_branch_a, 'w_branch_b': w_branch_b, 'w_branch_c': w_branch_c,
            'b_merge': b_merge, 'w_out': w_out}


def reference(x_prompt, x_sample, cache_k, cache_v, cache_logf, state_gla, state_pool, page_table,
              norm_g, w_in, pool_w, pool_scale, gla_w_a2, gla_b_a, gla_norm, fox_b_f, fox_q_norm,
              fox_k_norm, w_branch_a, w_branch_b, w_branch_c, b_merge, w_out):
    n_seq = x_sample.shape[0]
    past_len = page_table.shape[1] * PAGE_SIZE
    bp = x_prompt.shape[0]
    yp, ys = x_prompt, x_sample
    kp, vp, lp, gp, pp = [], [], [], [], []
    ksm, vsm, lsm, gsm, psm = [], [], [], [], []
    for l in range(DEPTH):
        params = (norm_g[l], w_in[l], pool_w[l], pool_scale[l], gla_w_a2[l], gla_b_a[l], gla_norm[l],
                  fox_b_f[l], fox_q_norm[l], fox_k_norm[l], w_branch_a[l], w_branch_b[l], w_branch_c[l],
                  b_merge[l], w_out[l])
        pool0 = jnp.zeros((bp, A_BUF, A_WIDTH), x_prompt.dtype)
        gla0 = jnp.zeros((bp, B_HEADS, B_DK, B_DV), x_prompt.dtype)
        yp, st = mixer_layer(yp, pool0, gla0, None, 0, *params)
        kp.append(st[0]); vp.append(st[1]); lp.append(st[2]); gp.append(st[3]); pp.append(st[4])
        k_past = cache_k[l][page_table].reshape(n_seq, past_len, C_HEADS, C_HEAD_DIM)
        v_past = cache_v[l][page_table].reshape(n_seq, past_len, C_HEADS, C_HEAD_DIM)
        lf_past = cache_logf[l][page_table].reshape(n_seq, past_len, C_HEADS)
        ys, st = mixer_layer(ys, state_pool[l], state_gla[l], (k_past, v_past, lf_past), past_len, *params)
        ksm.append(st[0]); vsm.append(st[1]); lsm.append(st[2]); gsm.append(st[3]); psm.append(st[4])
    k_prompt, v_prompt, logf_prompt = jnp.stack(kp), jnp.stack(vp), jnp.stack(lp)
    gla_prompt, pool_prompt = jnp.stack(gp), jnp.stack(pp)
    k_sample, v_sample, logf_sample = jnp.stack(ksm), jnp.stack(vsm), jnp.stack(lsm)
    gla_sample, pool_sample = jnp.stack(gsm), jnp.stack(psm)
    return (yp, ys, k_prompt, v_prompt, logf_prompt, gla_prompt, pool_prompt,
            k_sample, v_sample, logf_sample, gla_sample, pool_sample)
```

```python
import functools
import math

import jax
import jax.numpy as jnp
from jax import lax
from jax.experimental import pallas as pl
from jax.experimental.pallas import tpu as pltpu

F32 = jnp.float32
BF16 = jnp.bfloat16
HIGHEST = lax.Precision.HIGHEST

D_MODEL = 2048
A_WIDTH = 512
A_WINDOWS = (2, 4, 8, 16)
A_GROUPS = 4
A_GROUP_DIM = 128
A_BUF = 15
B_HEADS = 4
B_DK = 96
B_DV = 192
B_GATE_RANK = 16
B_GATE_TEMP = 16.0
B_CHUNK = 32
C_HEADS = 6
C_HEAD_DIM = 128
C_WIDTH = 768
N_BRANCH = 3
EPS = 1e-6
PAGE_SIZE = 128
SPLIT_SIZES = (512, 512, 384, 384, 768, 768, 16, 768, 768, 768, 768, 6, 6144)

LANES = 128
SUBLANES = 8
VMEM_LIMIT_BYTES = 56 * 1024 * 1024

B_DKP = 128
B_DVP = 256

OFF_MG = 0
OFF_CG = 6144
OFF_CQ = OFF_CG + C_WIDTH
OFF_CK = OFF_CQ + C_WIDTH
OFF_CV = OFF_CK + C_WIDTH
OFF_AU = OFF_CV + C_WIDTH
OFF_AG = OFF_AU + A_WIDTH
OFF_BQ = OFF_AG + A_WIDTH
OFF_BK = OFF_BQ + B_HEADS * B_DKP
OFF_BV = OFF_BK + B_HEADS * B_DKP
OFF_BG = OFF_BV + B_HEADS * B_DVP
OFF_SM = OFF_BG + B_HEADS * B_DVP
SM_WIDTH = 2 * LANES
PROJ_TN = 1536
NP = 13824
assert OFF_SM + SM_WIDTH <= NP and NP % PROJ_TN == 0
SM_TILE = OFF_SM // PROJ_TN
SM_LOCAL = OFF_SM - SM_TILE * PROJ_TN
assert SM_LOCAL + SM_WIDTH <= PROJ_TN


def _pick_tile(n, pref, align=SUBLANES):
    if n <= pref:
        return n
    t = (pref // align) * align
    while t > align and n % t:
        t -= align
    assert n % t == 0, (n, pref)
    return t


def _cparams(*sem):
    return pltpu.CompilerParams(dimension_semantics=sem, vmem_limit_bytes=VMEM_LIMIT_BYTES)


def _log_sigmoid(x):
    return jnp.minimum(x, 0.0) - jnp.log1p(jnp.exp(-jnp.abs(x)))


def _silu(x):
    return x * jax.nn.sigmoid(x)


def _iota(shape, dim):
    return lax.broadcasted_iota(jnp.int32, shape, dim)


def _inproj_kernel(x_ref, g_ref, w_ref, p_ref, f_ref, h_scr, *, nj, rchunk):
    j = pl.program_id(1)

    @pl.when(j == 0)
    def _():
        def body(c, carry):
            r0 = pl.multiple_of(c * rchunk, rchunk)
            x = x_ref[pl.ds(r0, rchunk), :]
            r = lax.rsqrt(jnp.mean(x * x, axis=-1, keepdims=True) + EPS)
            h_scr[pl.ds(r0, rchunk), :] = ((x * r) * g_ref[...]).astype(BF16)
            return carry

        lax.fori_loop(0, x_ref.shape[0] // rchunk, body, 0)

    acc = jnp.dot(h_scr[...], w_ref[...], preferred_element_type=F32)
    p_ref[...] = acc.astype(p_ref.dtype)

    @pl.when(j == nj - 1)
    def _():
        f_ref[...] = acc[:, SM_LOCAL:SM_LOCAL + SM_WIDTH]


def _inproj(x2, norm_g, w_packed, act_dt):
    rows = x2.shape[0]
    tm = _pick_tile(rows, 1024)
    nj = NP // PROJ_TN
    assert SM_TILE == nj - 1
    rchunk = _pick_tile(tm, 128)
    return pl.pallas_call(
        functools.partial(_inproj_kernel, nj=nj, rchunk=rchunk),
        grid=(rows // tm, nj),
        in_specs=[
            pl.BlockSpec((tm, D_MODEL), lambda i, j: (i, 0)),
            pl.BlockSpec((1, D_MODEL), lambda i, j: (0, 0)),
            pl.BlockSpec((D_MODEL, PROJ_TN), lambda i, j: (0, j)),
        ],
        out_specs=[
            pl.BlockSpec((tm, PROJ_TN), lambda i, j: (i, j)),
            pl.BlockSpec((tm, SM_WIDTH), lambda i, j: (i, 0)),
        ],
        out_shape=[
            jax.ShapeDtypeStruct((rows, NP), act_dt),
            jax.ShapeDtypeStruct((rows, SM_WIDTH), F32),
        ],
        scratch_shapes=[pltpu.VMEM((tm, D_MODEL), BF16)],
        compiler_params=_cparams("parallel", "arbitrary"),
        name="inproj",
    )(x2, norm_g.reshape(1, D_MODEL), w_packed)


def _pool_kernel(u_ref, g_ref, halo_ref, pw_ref, sc_ref, o_ref, *, tiles_per_seq, decode):
    i = pl.program_id(0)
    tr = u_ref.shape[0]
    hr = halo_ref.shape[0]
    mm_dt = F32 if decode else BF16
    u = u_ref[...].astype(F32)
    halo = halo_ref[...].astype(F32)
    if not decode:
        halo = jnp.where(i % tiles_per_seq == 0, 0.0, halo)
    ext = jnp.concatenate([halo, u], axis=0).astype(mm_dt)
    r = _iota((tr, hr + tr), 0)
    c = _iota((tr, hr + tr), 1) - hr
    rowpos = _iota((tr, 1), 0)
    pos = ((i % tiles_per_seq) * tr + rowpos).astype(F32)
    outs = []
    for g, w in enumerate(A_WINDOWS):
        lo, hi = g * A_GROUP_DIM, (g + 1) * A_GROUP_DIM
        band = ((c >= r - (w - 1)) & (c <= r)).astype(mm_dt)
        if decode:
            win_sum = jnp.dot(band, ext[:, lo:hi], preferred_element_type=F32, precision=HIGHEST)
            count = float(w)
        else:
            win_sum = jnp.dot(band, ext[:, lo:hi], preferred_element_type=F32)
            count = jnp.minimum(pos + 1.0, float(w))
        pooled = win_sum / count - u[:, lo:hi]
        outs.append(jnp.dot(pooled.astype(mm_dt), pw_ref[g].astype(mm_dt), preferred_element_type=F32))
    mixed = jnp.concatenate(outs, axis=1) * sc_ref[...]
    o_ref[...] = (mixed * _silu(g_ref[...].astype(F32))).astype(o_ref.dtype)


def _pool_branch(p_all, pool_w, pool_scale, *, seq_len, state=None):
    rows = p_all.shape[0]
    decode = state is not None
    wdt = F32 if decode else BF16
    if decode:
        tr, tps = seq_len, 1
        halo_spec = pl.BlockSpec((None, 16, A_WIDTH), lambda i: (i, 0, 0))
        halo_arr = state
    else:
        tr = _pick_tile(seq_len, 256)
        tps = seq_len // tr
        hb = LANES
        assert tr % hb == 0
        halo_spec = pl.BlockSpec((hb, A_WIDTH), lambda i: (jnp.maximum(i * (tr // hb) - 1, 0), OFF_AU // A_WIDTH))
        halo_arr = p_all
    return pl.pallas_call(
        functools.partial(_pool_kernel, tiles_per_seq=tps, decode=decode),
        grid=(rows // tr,),
        in_specs=[
            pl.BlockSpec((tr, A_WIDTH), lambda i: (i, OFF_AU // A_WIDTH)),
            pl.BlockSpec((tr, A_WIDTH), lambda i: (i, OFF_AG // A_WIDTH)),
            halo_spec,
            pl.BlockSpec((A_GROUPS, A_GROUP_DIM, A_GROUP_DIM), lambda i: (0, 0, 0)),
            pl.BlockSpec((1, A_WIDTH), lambda i: (0, 0)),
        ],
        out_specs=pl.BlockSpec((tr, A_WIDTH), lambda i: (i, 0)),
        out_shape=jax.ShapeDtypeStruct((rows, A_WIDTH), p_all.dtype),
        compiler_params=_cparams("arbitrary"),
        name="pool_decode" if decode else "pool_prompt",
    )(p_all, p_all, halo_arr, pool_w.astype(wdt), pool_scale.reshape(1, A_WIDTH))


def _gla_kernel(q_ref, k_ref, v_ref, g_ref, r_ref, wa_ref, ba_ref, gn_ref, s0_ref,
                o_ref, so_ref, s_scr, *, chunk, decode):
    tb = pl.program_id(1)
    nb = pl.num_programs(1)
    tbk = q_ref.shape[0]
    mm_dt = F32 if decode else BF16

    @pl.when(tb == 0)
    def _():
        s_scr[...] = s0_ref[...]

    logit = jnp.dot(r_ref[...], wa_ref[...], preferred_element_type=F32, precision=HIGHEST) + ba_ref[...]
    log_a = _log_sigmoid(logit) / B_GATE_TEMP
    tri = (_iota((chunk, chunk), 0) >= _iota((chunk, chunk), 1))
    tri_f = tri.astype(F32)
    scale = B_DK ** -0.5
    for c in range(tbk // chunk):
        rs = slice(c * chunk, (c + 1) * chunk)
        b = jnp.dot(tri_f, log_a[rs], preferred_element_type=F32, precision=HIGHEST)
        b_last = b[chunk - 1:chunk, :]
        qc = q_ref[rs, :].astype(F32)
        kc = k_ref[rs, :].astype(F32)
        q_t = (qc * scale) * jnp.exp(b)
        k_t = kc * jnp.exp(-b)
        k_end = kc * jnp.exp(b_last - b)
        decay = jnp.exp(b_last)
        for h in range(B_HEADS):
            ks = slice(h * B_DKP, (h + 1) * B_DKP)
            vs = slice(h * B_DVP, (h + 1) * B_DVP)
            qh = q_t[:, ks].astype(mm_dt)
            att = lax.dot_general(qh, k_t[:, ks].astype(mm_dt), (((1,), (1,)), ((), ())),
                                  preferred_element_type=F32)
            att = jnp.where(tri, att, 0.0)
            vh = v_ref[rs, vs].astype(mm_dt)
            st = s_scr[h]
            o = jnp.dot(att.astype(mm_dt), vh, preferred_element_type=F32)
            o = o + lax.dot_general(qh, st.astype(mm_dt), (((1,), (1,)), ((), ())),
                                    preferred_element_type=F32)
            kv_t = lax.dot_general(vh, k_end[:, ks].astype(mm_dt), (((0,), (0,)), ((), ())),
                                   preferred_element_type=F32)
            s_scr[h] = st * decay[:, ks] + kv_t
            r = lax.rsqrt(jnp.sum(o * o, axis=-1, keepdims=True) * (1.0 / B_DV) + EPS)
            on = (o * r) * gn_ref[:, vs]
            o_ref[rs, vs] = (on * _silu(g_ref[rs, vs].astype(F32))).astype(o_ref.dtype)

    @pl.when(tb == nb - 1)
    def _():
        so_ref[...] = s_scr[...]


def _gla_branch(p_all, f_all, wa_p, ba_p, gn_p, s0_t, *, n_seq, seq_len, decode):
    rows = p_all.shape[0]
    chunk = math.gcd(seq_len, B_CHUNK)
    tbk = _pick_tile(seq_len, 256)
    nb = seq_len // tbk
    kw = B_HEADS * B_DKP
    vw = B_HEADS * B_DVP
    return pl.pallas_call(
        functools.partial(_gla_kernel, chunk=chunk, decode=decode),
        grid=(n_seq, nb),
        in_specs=[
            pl.BlockSpec((tbk, kw), lambda s, t: (s * nb + t, OFF_BQ // kw)),
            pl.BlockSpec((tbk, kw), lambda s, t: (s * nb + t, OFF_BK // kw)),
            pl.BlockSpec((tbk, vw), lambda s, t: (s * nb + t, OFF_BV // vw)),
            pl.BlockSpec((tbk, vw), lambda s, t: (s * nb + t, OFF_BG // vw)),
            pl.BlockSpec((tbk, LANES), lambda s, t: (s * nb + t, 0)),
            pl.BlockSpec((LANES, kw), lambda s, t: (0, 0)),
            pl.BlockSpec((1, kw), lambda s, t: (0, 0)),
            pl.BlockSpec((1, vw), lambda s, t: (0, 0)),
            pl.BlockSpec((None, B_HEADS, B_DVP, B_DKP), lambda s, t: (s, 0, 0, 0)),
        ],
        out_specs=[
            pl.BlockSpec((tbk, vw), lambda s, t: (s * nb + t, 0)),
            pl.BlockSpec((None, B_HEADS, B_DVP, B_DKP), lambda s, t: (s, 0, 0, 0)),
        ],
        out_shape=[
            jax.ShapeDtypeStruct((rows, vw), p_all.dtype),
            jax.ShapeDtypeStruct((n_seq, B_HEADS, B_DVP, B_DKP), F32),
        ],
        scratch_shapes=[pltpu.VMEM((B_HEADS, B_DVP, B_DKP), F32)],
        compiler_params=_cparams("arbitrary", "arbitrary"),
        name="gla_decode" if decode else "gla_prompt",
    )(p_all, p_all, p_all, p_all, f_all, wa_p, ba_p, gn_p, s0_t)


def _foxprep_kernel(cq_ref, ck_ref, cv_ref, cf_ref, qn_ref, kn_ref, bf_ref,
                    qa_ref, ka_ref, ko_ref, vo_ref, lf_ref, c_scr, carry_scr, *, tiles_per_seq):
    i = pl.program_id(0)
    h = pl.program_id(1)
    tr = cq_ref.shape[0]
    lane = _iota((tr, LANES), 1)

    @pl.when(h == 0)
    def _():
        lf = jnp.where(lane < C_HEADS, _log_sigmoid(cf_ref[...] + bf_ref[...]), 0.0)
        lf_ref[...] = lf

        @pl.when(i % tiles_per_seq == 0)
        def _():
            carry_scr[...] = jnp.zeros_like(carry_scr)

        tri = (_iota((tr, tr), 0) >= _iota((tr, tr), 1)).astype(F32)
        cum = jnp.dot(tri, lf, preferred_element_type=F32, precision=HIGHEST) + carry_scr[...]
        c_scr[...] = cum
        carry_scr[...] = cum[tr - 1:tr, :]

    ch = jnp.sum(jnp.where(lane == h, c_scr[...], 0.0), axis=-1, keepdims=True)
    hi = ch.astype(BF16).astype(F32)
    r1 = ch - hi
    mid = r1.astype(BF16).astype(F32)
    lo = r1 - mid

    def qk_norm(ref, g_ref):
        x = ref[...].astype(F32)
        r = lax.rsqrt(jnp.mean(x * x, axis=-1, keepdims=True) + EPS)
        return (x * r) * g_ref[...]

    qn = qk_norm(cq_ref, qn_ref)
    kn = qk_norm(ck_ref, kn_ref)
    one = jnp.ones((tr, LANES), F32)
    zero = jnp.zeros((tr, LANES), F32)
    aux_q = jnp.where(lane == 0, hi, jnp.where(lane == 1, mid, jnp.where(lane == 2, lo,
                      jnp.where(lane < 6, one, zero))))
    aux_k = jnp.where(lane < 3, one, jnp.where(lane == 3, -hi, jnp.where(lane == 4, -mid,
                      jnp.where(lane == 5, -lo, zero))))
    qa_ref[:, :C_HEAD_DIM] = (qn * (C_HEAD_DIM ** -0.5)).astype(qa_ref.dtype)
    qa_ref[:, C_HEAD_DIM:] = aux_q.astype(qa_ref.dtype)
    ka_ref[:, :C_HEAD_DIM] = kn.astype(ka_ref.dtype)
    ka_ref[:, C_HEAD_DIM:] = aux_k.astype(ka_ref.dtype)
    ko_ref[...] = kn
    vo_ref[...] = cv_ref[...].astype(F32)


def _fox_prep(p_all, f_all, fox_q_norm, fox_k_norm, bf_p, *, seq_len):
    rows = p_all.shape[0]
    tr = _pick_tile(seq_len, 256)
    tps = seq_len // tr
    hd = C_HEAD_DIM
    aw = 2 * hd
    return pl.pallas_call(
        functools.partial(_foxprep_kernel, tiles_per_seq=tps),
        grid=(rows // tr, C_HEADS),
        in_specs=[
            pl.BlockSpec((tr, hd), lambda i, h: (i, OFF_CQ // hd + h)),
            pl.BlockSpec((tr, hd), lambda i, h: (i, OFF_CK // hd + h)),
            pl.BlockSpec((tr, hd), lambda i, h: (i, OFF_CV // hd + h)),
            pl.BlockSpec((tr, LANES), lambda i, h: (i, 1)),
            pl.BlockSpec((1, hd), lambda i, h: (0, 0)),
            pl.BlockSpec((1, hd), lambda i, h: (0, 0)),
            pl.BlockSpec((1, LANES), lambda i, h: (0, 0)),
        ],
        out_specs=[
            pl.BlockSpec((tr, aw), lambda i, h: (i, h)),
            pl.BlockSpec((tr, aw), lambda i, h: (i, h)),
            pl.BlockSpec((tr, hd), lambda i, h: (i, h)),
            pl.BlockSpec((tr, hd), lambda i, h: (i, h)),
            pl.BlockSpec((tr, LANES), lambda i, h: (i, 0)),
        ],
        out_shape=[
            jax.ShapeDtypeStruct((rows, C_HEADS * aw), p_all.dtype),
            jax.ShapeDtypeStruct((rows, C_HEADS * aw), p_all.dtype),
            jax.ShapeDtypeStruct((rows, C_WIDTH), F32),
            jax.ShapeDtypeStruct((rows, C_WIDTH), F32),
            jax.ShapeDtypeStruct((rows, LANES), F32),
        ],
        scratch_shapes=[pltpu.VMEM((tr, LANES), F32), pltpu.VMEM((1, LANES), F32)],
        compiler_params=_cparams("arbitrary", "arbitrary"),
        name="fox_prep",
    )(p_all, p_all, p_all, f_all, fox_q_norm.reshape(1, hd), fox_k_norm.reshape(1, hd), bf_p)


def _flash_kernel(q_ref, k_ref, v_ref, g_ref, o_ref, m_scr, l_scr, acc_scr):
    i = pl.program_id(2)
    j = pl.program_id(3)
    tq, tk = q_ref.shape[0], k_ref.shape[0]

    @pl.when(j == 0)
    def _():
        m_scr[...] = jnp.full_like(m_scr, -jnp.inf)
        l_scr[...] = jnp.zeros_like(l_scr)
        acc_scr[...] = jnp.zeros_like(acc_scr)

    def step(diagonal):
        s = lax.dot_general(q_ref[...], k_ref[...], (((1,), (1,)), ((), ())), preferred_element_type=F32)
        if diagonal:
            s = jnp.where(_iota((tq, tk), 0) >= _iota((tq, tk), 1), s, -jnp.inf)
        m_prev = m_scr[...]
        m_new = jnp.maximum(m_prev, jnp.max(s, axis=-1, keepdims=True))
        alpha = jnp.exp(m_prev - m_new)
        p = jnp.exp(s - m_new)
        l_scr[...] = alpha * l_scr[...] + jnp.sum(p, axis=-1, keepdims=True)
        acc_scr[...] = alpha * acc_scr[...] + jnp.dot(p.astype(BF16), v_ref[...], preferred_element_type=F32)
        m_scr[...] = m_new

    @pl.when(j < i)
    def _():
        step(False)

    @pl.when(j == i)
    def _():
        step(True)
        out = acc_scr[...] / l_scr[...]
        o_ref[...] = (out * _silu(g_ref[...].astype(F32))).astype(BF16)


def _fox_flash(q_aug, k_aug, p_all, *, n_seq, seq_len):
    rows = q_aug.shape[0]
    t = _pick_tile(seq_len, 512)
    nt = seq_len // t
    hd = C_HEAD_DIM
    aw = 2 * hd
    return pl.pallas_call(
        _flash_kernel,
        grid=(n_seq, C_HEADS, nt, nt),
        in_specs=[
            pl.BlockSpec((t, aw), lambda b, h, i, j: (b * nt + i, h)),
            pl.BlockSpec((t, aw), lambda b, h, i, j: (b * nt + jnp.minimum(j, i), h)),
            pl.BlockSpec((t, hd), lambda b, h, i, j: (b * nt + jnp.minimum(j, i), OFF_CV // hd + h)),
            pl.BlockSpec((t, hd), lambda b, h, i, j: (b * nt + i, OFF_CG // hd + h)),
        ],
        out_specs=pl.BlockSpec((t, hd), lambda b, h, i, j: (b * nt + i, h)),
        out_shape=jax.ShapeDtypeStruct((rows, C_WIDTH), BF16),
        scratch_shapes=[pltpu.VMEM((t, 1), F32), pltpu.VMEM((t, 1), F32), pltpu.VMEM((t, hd), F32)],
        compiler_params=_cparams("parallel", "parallel", "parallel", "arbitrary"),
        name="fox_flash",
    )(q_aug, k_aug, p_all, p_all)


def _decbias_kernel(pt_ref, lf_ref, lfn_ref, ck_ref, ckn_ref, carry_scr):
    del pt_ref
    p = pl.program_id(1)
    npg = pl.num_programs(1)
    upper = (_iota((LANES, LANES), 0) <= _iota((LANES, LANES), 1)).astype(F32)

    @pl.when(p == 0)
    def _():
        carry_scr[...] = jnp.zeros_like(carry_scr)

    cum = jnp.dot(lf_ref[...], upper, preferred_element_type=F32, precision=HIGHEST) + carry_scr[...]
    ck_ref[...] = cum
    total = jnp.broadcast_to(cum[:, LANES - 1:LANES], cum.shape)
    carry_scr[...] = total

    @pl.when(p == npg - 1)
    def _():
        ckn_ref[...] = jnp.dot(lfn_ref[...], upper, preferred_element_type=F32, precision=HIGHEST) + total


def _decode_bias(page_table, logf_t, lfn_t, *, layer):
    n_seq, npg = page_table.shape
    return pl.pallas_call(
        _decbias_kernel,
        grid_spec=pltpu.PrefetchScalarGridSpec(
            num_scalar_prefetch=1,
            grid=(n_seq, npg),
            in_specs=[
                pl.BlockSpec((None, None, SUBLANES, PAGE_SIZE), lambda s, p, pt: (layer, pt[s, p], 0, 0)),
                pl.BlockSpec((None, SUBLANES, LANES), lambda s, p, pt: (s, 0, 0)),
            ],
            out_specs=[
                pl.BlockSpec((None, SUBLANES, PAGE_SIZE), lambda s, p, pt: (s, 0, p)),
                pl.BlockSpec((None, SUBLANES, LANES), lambda s, p, pt: (s, 0, 0)),
            ],
            scratch_shapes=[pltpu.VMEM((SUBLANES, LANES), F32)],
        ),
        out_shape=[
            jax.ShapeDtypeStruct((n_seq, SUBLANES, npg * PAGE_SIZE), F32),
            jax.ShapeDtypeStruct((n_seq, SUBLANES, LANES), F32),
        ],
        compiler_params=_cparams("arbitrary", "arbitrary"),
        name="decode_bias",
    )(page_table, logf_t, lfn_t)


def _decode_kernel(pt_ref, qa_ref, k_ref, v_ref, ck_ref, ckn_ref, kn_ref, vn_ref, g_ref,
                   o_ref, q_scr, m_scr, l_scr, acc_scr, *, dec_seq):
    del pt_ref
    p = pl.program_id(1)
    npg = pl.num_programs(1)
    nq = dec_seq
    nrow = C_HEADS * nq
    hd = C_HEAD_DIM
    row = _iota((nrow, LANES), 0)
    lane = _iota((nrow, LANES), 1)
    qi = jnp.bitwise_and(row, nq - 1)

    @pl.when(p == 0)
    def _():
        qa = qa_ref[...].astype(F32)
        zero = jnp.zeros((nq, hd), F32)
        blocks = []
        for h in range(C_HEADS):
            qh = qa[:, h * 2 * hd:h * 2 * hd + hd]
            blocks.append(jnp.concatenate([qh if hh == h else zero for hh in range(C_HEADS)], axis=1))
        q_scr[...] = jnp.concatenate(blocks, axis=0).astype(BF16)
        m_scr[...] = jnp.full_like(m_scr, -jnp.inf)
        l_scr[...] = jnp.zeros_like(l_scr)
        acc_scr[...] = jnp.zeros_like(acc_scr)

    def head_rows(ref):
        return jnp.concatenate([jnp.broadcast_to(ref[h:h + 1, :], (nq, LANES)) for h in range(C_HEADS)], axis=0)

    ckn_rows = head_rows(ckn_ref)
    c_q = jnp.sum(jnp.where(lane == qi, ckn_rows, 0.0), axis=-1, keepdims=True)

    def attend(kcat, vcat, ck_rows, mask):
        s = lax.dot_general(q_scr[...], kcat, (((1,), (1,)), ((), ())), preferred_element_type=F32)
        s = s + (c_q - ck_rows)
        if mask is not None:
            s = jnp.where(mask, s, -jnp.inf)
        m_prev = m_scr[...]
        m_new = jnp.maximum(m_prev, jnp.max(s, axis=-1, keepdims=True))
        alpha = jnp.exp(m_prev - m_new)
        pr = jnp.exp(s - m_new)
        l_scr[...] = alpha * l_scr[...] + jnp.sum(pr, axis=-1, keepdims=True)
        o_all = jnp.dot(pr.astype(BF16), vcat, preferred_element_type=F32)
        o_diag = jnp.concatenate([o_all[h * nq:(h + 1) * nq, h * hd:(h + 1) * hd] for h in range(C_HEADS)], axis=0)
        acc_scr[...] = alpha * acc_scr[...] + o_diag
        m_scr[...] = m_new

    kcat = jnp.concatenate([k_ref[:, h, :] for h in range(C_HEADS)], axis=1).astype(BF16)
    vcat = jnp.concatenate([v_ref[:, h, :] for h in range(C_HEADS)], axis=1).astype(BF16)
    attend(kcat, vcat, head_rows(ck_ref), None)

    @pl.when(p == npg - 1)
    def _():
        pad = jnp.zeros((PAGE_SIZE - nq, C_WIDTH), F32)
        kn = jnp.concatenate([kn_ref[...], pad], axis=0).astype(BF16)
        vn = jnp.concatenate([vn_ref[...], pad], axis=0).astype(BF16)
        attend(kn, vn, ckn_rows, lane <= qi)
        out = acc_scr[...] / l_scr[...]
        gate = _silu(g_ref[...].astype(F32))
        for h in range(C_HEADS):
            cs = slice(h * hd, (h + 1) * hd)
            o_ref[:, cs] = (out[h * nq:(h + 1) * nq, :] * gate[:, cs]).astype(o_ref.dtype)


def _fox_decode(page_table, q_aug, cache_k, cache_v, ck_past, ck_new, k_new, v_new, p_all, *, layer, dec_seq):
    n_seq, npg = page_table.shape
    rows = q_aug.shape[0]
    nrow = C_HEADS * dec_seq
    page_spec = pl.BlockSpec((None, None, PAGE_SIZE, C_HEADS, C_HEAD_DIM),
                             lambda s, p, pt: (layer, pt[s, p], 0, 0, 0))
    return pl.pallas_call(
        functools.partial(_decode_kernel, dec_seq=dec_seq),
        grid_spec=pltpu.PrefetchScalarGridSpec(
            num_scalar_prefetch=1,
            grid=(n_seq, npg),
            in_specs=[
                pl.BlockSpec((dec_seq, C_HEADS * 2 * C_HEAD_DIM), lambda s, p, pt: (s, 0)),
                page_spec,
                page_spec,
                pl.BlockSpec((None, SUBLANES, PAGE_SIZE), lambda s, p, pt: (s, 0, p)),
                pl.BlockSpec((None, SUBLANES, LANES), lambda s, p, pt: (s, 0, 0)),
                pl.BlockSpec((dec_seq, C_WIDTH), lambda s, p, pt: (s, 0)),
                pl.BlockSpec((dec_seq, C_WIDTH), lambda s, p, pt: (s, 0)),
                pl.BlockSpec((dec_seq, C_WIDTH), lambda s, p, pt: (s, OFF_CG // C_WIDTH)),
            ],
            out_specs=pl.BlockSpec((dec_seq, C_WIDTH), lambda s, p, pt: (s, 0)),
            scratch_shapes=[
                pltpu.VMEM((nrow, C_WIDTH), BF16),
                pltpu.VMEM((nrow, 1), F32),
                pltpu.VMEM((nrow, 1), F32),
                pltpu.VMEM((nrow, C_HEAD_DIM), F32),
            ],
        ),
        out_shape=jax.ShapeDtypeStruct((rows, C_WIDTH), p_all.dtype),
        compiler_params=_cparams("arbitrary", "arbitrary"),
        name="fox_decode",
    )(page_table, q_aug, cache_k, cache_v, ck_past, ck_new, k_new, v_new, p_all)


def _merge_kernel(x_ref, a_ref, b_ref, c_ref, mg0_ref, mg1_ref, mg2_ref, bm_ref,
                  wa_ref, wb_ref, wc_ref, wo_ref, y_ref):
    d = D_MODEL

    def gated(o_ref, w_ref, mg_ref, k):
        gate = jax.nn.sigmoid(mg_ref[...].astype(F32) + bm_ref[:, k * d:(k + 1) * d])
        return gate * jnp.dot(o_ref[...].astype(BF16), w_ref[...], preferred_element_type=F32)

    merged = gated(a_ref, wa_ref, mg0_ref, 0) + gated(b_ref, wb_ref, mg1_ref, 1) + gated(c_ref, wc_ref, mg2_ref, 2)
    y_ref[...] = x_ref[...] + jnp.dot(merged.astype(BF16), wo_ref[...], preferred_element_type=F32)


def _merge(x2, a_o, b_o, c_o, p_all, b_merge, wa, wb, wc, wo):
    rows = x2.shape[0]
    tm = _pick_tile(rows, 256)
    d = D_MODEL
    bw = B_HEADS * B_DVP

    def const(shape):
        return pl.BlockSpec(shape, lambda i: (0, 0), pipeline_mode=pl.Buffered(1))

    return pl.pallas_call(
        _merge_kernel,
        grid=(rows // tm,),
        in_specs=[
            pl.BlockSpec((tm, d), lambda i: (i, 0)),
            pl.BlockSpec((tm, A_WIDTH), lambda i: (i, 0)),
            pl.BlockSpec((tm, bw), lambda i: (i, 0)),
            pl.BlockSpec((tm, C_WIDTH), lambda i: (i, 0)),
            pl.BlockSpec((tm, d), lambda i: (i, OFF_MG // d + 0)),
            pl.BlockSpec((tm, d), lambda i: (i, OFF_MG // d + 1)),
            pl.BlockSpec((tm, d), lambda i: (i, OFF_MG // d + 2)),
            const((1, N_BRANCH * d)),
            const((A_WIDTH, d)),
            const((bw, d)),
            const((C_WIDTH, d)),
            const((d, d)),
        ],
        out_specs=pl.BlockSpec((tm, d), lambda i: (i, 0)),
        out_shape=jax.ShapeDtypeStruct((rows, d), F32),
        compiler_params=_cparams("parallel"),
        name="merge_out",
    )(x2, a_o, b_o, c_o, p_all, p_all, p_all, b_merge.reshape(1, N_BRANCH * d), wa, wb, wc, wo)


def _pad_heads(x, n_heads, d, dp, axis):
    shp = x.shape
    x = x.reshape(shp[:axis] + (n_heads, d) + shp[axis + 1:])
    pad = [(0, 0)] * x.ndim
    pad[axis + 1] = (0, dp - d)
    x = jnp.pad(x, pad)
    return x.reshape(shp[:axis] + (n_heads * dp,) + shp[axis + 1:])


def _pack_w_in(w):
    pts = []
    acc = 0
    for s in SPLIT_SIZES[:-1]:
        acc += s
        pts.append(acc)
    (a_u, a_g, b_q, b_k, b_v, b_g, b_r, c_q, c_k, c_v, c_g, c_f, m_g) = jnp.split(w, pts, axis=1)
    lane_pad = lambda x: jnp.pad(x, ((0, 0), (0, LANES - x.shape[1])))
    cols = [m_g, c_g, c_q, c_k, c_v, a_u, a_g,
            _pad_heads(b_q, B_HEADS, B_DK, B_DKP, 1), _pad_heads(b_k, B_HEADS, B_DK, B_DKP, 1),
            _pad_heads(b_v, B_HEADS, B_DV, B_DVP, 1), _pad_heads(b_g, B_HEADS, B_DV, B_DVP, 1),
            lane_pad(b_r), lane_pad(c_f)]
    packed = jnp.concatenate(cols, axis=1)
    assert packed.shape[1] == OFF_SM + SM_WIDTH
    return jnp.pad(packed, ((0, 0), (0, NP - packed.shape[1]))).astype(BF16)


def _layer_params(l, norm_g, w_in, pool_w, pool_scale, gla_w_a2, gla_b_a, gla_norm, fox_b_f, fox_q_norm,
                  fox_k_norm, w_branch_a, w_branch_b, w_branch_c, b_merge, w_out):
    wa2 = _pad_heads(gla_w_a2[l], B_HEADS, B_DK, B_DKP, 1)
    return dict(
        norm_g=norm_g[l],
        w_packed=_pack_w_in(w_in[l]),
        pool_w=pool_w[l],
        pool_scale=pool_scale[l],
        wa2=jnp.pad(wa2, ((0, LANES - B_GATE_RANK), (0, 0))),
        ba=_pad_heads(gla_b_a[l].reshape(1, -1), B_HEADS, B_DK, B_DKP, 1),
        gn=_pad_heads(gla_norm[l].reshape(1, -1), B_HEADS, B_DV, B_DVP, 1),
        bf=jnp.pad(fox_b_f[l].reshape(1, -1), ((0, 0), (0, LANES - C_HEADS))),
        qn=fox_q_norm[l],
        kn=fox_k_norm[l],
        wa=w_branch_a[l].astype(BF16),
        wb=_pad_heads(w_branch_b[l], B_HEADS, B_DV, B_DVP, 0).astype(BF16),
        wc=w_branch_c[l].astype(BF16),
        b_merge=b_merge[l],
        wo=w_out[l].astype(BF16),
    )


def _state_to_kernel(s):
    st = jnp.swapaxes(s, -1, -2)
    return jnp.pad(st, ((0, 0), (0, 0), (0, B_DVP - B_DV), (0, B_DKP - B_DK)))


def _state_from_kernel(st):
    return jnp.swapaxes(st[:, :, :B_DV, :B_DK], -1, -2)


def kernel(x_prompt, x_sample, cache_k, cache_v, cache_logf, state_gla, state_pool, page_table, norm_g, w_in, pool_w, pool_scale, gla_w_a2, gla_b_a, gla_norm, fox_b_f, fox_q_norm, fox_k_norm, w_branch_a, w_branch_b, w_branch_c, b_merge, w_out):
    depth = w_in.shape[0]
    bp, seq, d = x_prompt.shape
    n_seq, dec_seq, _ = x_sample.shape
    assert d == D_MODEL and dec_seq == SUBLANES and seq % LANES == 0
    yp = x_prompt.reshape(bp * seq, d)
    ys = x_sample.reshape(n_seq * dec_seq, d)
    logf_t = jnp.pad(jnp.swapaxes(cache_logf, -1, -2), ((0, 0), (0, 0), (0, SUBLANES - C_HEADS), (0, 0)))
    zero_state = jnp.zeros((bp, B_HEADS, B_DVP, B_DKP), F32)
    outs = {k: [] for k in ("kp", "vp", "lp", "gp", "pp", "ks", "vs", "ls", "gs", "ps")}
    for l in range(depth):
        prm = _layer_params(l, norm_g, w_in, pool_w, pool_scale, gla_w_a2, gla_b_a, gla_norm, fox_b_f,
                            fox_q_norm, fox_k_norm, w_branch_a, w_branch_b, w_branch_c, b_merge, w_out)

        p_all, f_all = _inproj(yp, prm["norm_g"], prm["w_packed"], BF16)
        a_o = _pool_branch(p_all, prm["pool_w"], prm["pool_scale"], seq_len=seq)
        b_o, s_fin = _gla_branch(p_all, f_all, prm["wa2"], prm["ba"], prm["gn"], zero_state,
                                 n_seq=bp, seq_len=seq, decode=False)
        q_aug, k_aug, k_n, v_n, lf = _fox_prep(p_all, f_all, prm["qn"], prm["kn"], prm["bf"], seq_len=seq)
        c_o = _fox_flash(q_aug, k_aug, p_all, n_seq=bp, seq_len=seq)
        yp = _merge(yp, a_o, b_o, c_o, p_all, prm["b_merge"], prm["wa"], prm["wb"], prm["wc"], prm["wo"])
        outs["kp"].append(k_n.reshape(bp, seq, C_HEADS, C_HEAD_DIM))
        outs["vp"].append(v_n.reshape(bp, seq, C_HEADS, C_HEAD_DIM))
        outs["lp"].append(lf[:, :C_HEADS].reshape(bp, seq, C_HEADS))
        outs["gp"].append(_state_from_kernel(s_fin))
        a_u = p_all[:, OFF_AU:OFF_AU + A_WIDTH].reshape(bp, seq, A_WIDTH)
        outs["pp"].append(a_u[:, seq - A_BUF:].astype(F32))

        p_s, f_s = _inproj(ys, prm["norm_g"], prm["w_packed"], F32)
        pool_state = jnp.pad(state_pool[l], ((0, 0), (1, 0), (0, 0)))
        a_s = _pool_branch(p_s, prm["pool_w"], prm["pool_scale"], seq_len=dec_seq, state=pool_state)
        b_s, s_fin_s = _gla_branch(p_s, f_s, prm["wa2"], prm["ba"], prm["gn"], _state_to_kernel(state_gla[l]),
                                   n_seq=n_seq, seq_len=dec_seq, decode=True)
        q_aug_s, _, k_ns, v_ns, lf_s = _fox_prep(p_s, f_s, prm["qn"], prm["kn"], prm["bf"], seq_len=dec_seq)
        lfn_t = jnp.swapaxes(lf_s.reshape(n_seq, dec_seq, LANES)[:, :, :SUBLANES], 1, 2)
        lfn_t = jnp.pad(lfn_t, ((0, 0), (0, 0), (0, LANES - dec_seq)))
        ck_past, ck_new = _decode_bias(page_table, logf_t, lfn_t, layer=l)
        c_s = _fox_decode(page_table, q_aug_s, cache_k, cache_v, ck_past, ck_new, k_ns, v_ns, p_s,
                          layer=l, dec_seq=dec_seq)
        ys = _merge(ys, a_s, b_s, c_s, p_s, prm["b_merge"], prm["wa"], prm["wb"], prm["wc"], prm["wo"])
        outs["ks"].append(k_ns.reshape(n_seq, dec_seq, C_HEADS, C_HEAD_DIM))
        outs["vs"].append(v_ns.reshape(n_seq, dec_seq, C_HEADS, C_HEAD_DIM))
        outs["ls"].append(lf_s[:, :C_HEADS].reshape(n_seq, dec_seq, C_HEADS))
        outs["gs"].append(_state_from_kernel(s_fin_s))
        a_us = p_s[:, OFF_AU:OFF_AU + A_WIDTH].reshape(n_seq, dec_seq, A_WIDTH).astype(F32)
        outs["ps"].append(jnp.concatenate([state_pool[l], a_us], axis=1)[:, -A_BUF:])

    st = lambda k: jnp.stack(outs[k])
    return (yp.reshape(bp, seq, d), ys.reshape(n_seq, dec_seq, d),
            st("kp"), st("vp"), st("lp"), st("gp"), st("pp"),
            st("ks"), st("vs"), st("ls"), st("gs"), st("ps"))
```

```python
import functools
import math

import jax
import jax.numpy as jnp
from jax import lax
from jax.experimental import pallas as pl
from jax.experimental.pallas import tpu as pltpu

F32 = jnp.float32
BF16 = jnp.bfloat16
HIGHEST = lax.Precision.HIGHEST

D_MODEL = 2048
A_WIDTH = 512
A_WINDOWS = (2, 4, 8, 16)
A_GROUPS = 4
A_GROUP_DIM = 128
A_BUF = 15
B_HEADS = 4
B_DK = 96
B_DV = 192
B_GATE_RANK = 16
B_GATE_TEMP = 16.0
B_CHUNK = 32
C_HEADS = 6
C_HEAD_DIM = 128
C_WIDTH = 768
N_BRANCH = 3
EPS = 1e-6
PAGE_SIZE = 128
SPLIT_SIZES = (512, 512, 384, 384, 768, 768, 16, 768, 768, 768, 768, 6, 6144)

LANES = 128
SUBLANES = 8
VMEM_LIMIT_BYTES = 56 * 1024 * 1024

B_DKP = 128
B_DVP = 256

OFF_MG = 0
OFF_CG = 6144
OFF_CQ = OFF_CG + C_WIDTH
OFF_CK = OFF_CQ + C_WIDTH
OFF_CV = OFF_CK + C_WIDTH
OFF_AU = OFF_CV + C_WIDTH
OFF_AG = OFF_AU + A_WIDTH
OFF_BQ = OFF_AG + A_WIDTH
OFF_BK = OFF_BQ + B_HEADS * B_DKP
OFF_BV = OFF_BK + B_HEADS * B_DKP
OFF_BG = OFF_BV + B_HEADS * B_DVP
OFF_SM = OFF_BG + B_HEADS * B_DVP
SM_WIDTH = 2 * LANES
PROJ_TN = 1536
NP = 13824
assert OFF_SM + SM_WIDTH <= NP and NP % PROJ_TN == 0
SM_TILE = OFF_SM // PROJ_TN
SM_LOCAL = OFF_SM - SM_TILE * PROJ_TN
assert SM_LOCAL + SM_WIDTH <= PROJ_TN


def _pick_tile(n, pref, align=SUBLANES):
    if n <= pref:
        return n
    t = (pref // align) * align
    while t > align and n % t:
        t -= align
    assert n % t == 0, (n, pref)
    return t


def _cparams(*sem):
    return pltpu.CompilerParams(dimension_semantics=sem, vmem_limit_bytes=VMEM_LIMIT_BYTES)


def _log_sigmoid(x):
    return jnp.minimum(x, 0.0) - jnp.log1p(jnp.exp(-jnp.abs(x)))


def _silu(x):
    return x * jax.nn.sigmoid(x)


def _iota(shape, dim):
    return lax.broadcasted_iota(jnp.int32, shape, dim)


def _inproj_kernel(x_ref, g_ref, w_ref, p_ref, f_ref, h_scr, *, nj, rchunk):
    j = pl.program_id(1)

    @pl.when(j == 0)
    def _():
        def body(c, carry):
            r0 = pl.multiple_of(c * rchunk, rchunk)
            x = x_ref[pl.ds(r0, rchunk), :]
            r = lax.rsqrt(jnp.mean(x * x, axis=-1, keepdims=True) + EPS)
            h_scr[pl.ds(r0, rchunk), :] = ((x * r) * g_ref[...]).astype(BF16)
            return carry

        lax.fori_loop(0, x_ref.shape[0] // rchunk, body, 0)

    acc = jnp.dot(h_scr[...], w_ref[...], preferred_element_type=F32)
    p_ref[...] = acc.astype(p_ref.dtype)

    @pl.when(j == nj - 1)
    def _():
        f_ref[...] = acc[:, SM_LOCAL:SM_LOCAL + SM_WIDTH]


def _inproj(x2, norm_g, w_packed, act_dt):
    rows = x2.shape[0]
    tm = _pick_tile(rows, 1024)
    nj = NP // PROJ_TN
    assert SM_TILE == nj - 1
    rchunk = _pick_tile(tm, 128)
    return pl.pallas_call(
        functools.partial(_inproj_kernel, nj=nj, rchunk=rchunk),
        grid=(rows // tm, nj),
        in_specs=[
            pl.BlockSpec((tm, D_MODEL), lambda i, j: (i, 0)),
            pl.BlockSpec((1, D_MODEL), lambda i, j: (0, 0)),
            pl.BlockSpec((D_MODEL, PROJ_TN), lambda i, j: (0, j)),
        ],
        out_specs=[
            pl.BlockSpec((tm, PROJ_TN), lambda i, j: (i, j)),
            pl.BlockSpec((tm, SM_WIDTH), lambda i, j: (i, 0)),
        ],
        out_shape=[
            jax.ShapeDtypeStruct((rows, NP), act_dt),
            jax.ShapeDtypeStruct((rows, SM_WIDTH), F32),
        ],
        scratch_shapes=[pltpu.VMEM((tm, D_MODEL), BF16)],
        compiler_params=_cparams("parallel", "arbitrary"),
        name="inproj",
    )(x2, norm_g.reshape(1, D_MODEL), w_packed)


def _pool_kernel(u_ref, g_ref, halo_ref, pw_ref, sc_ref, o_ref, *, tiles_per_seq, decode):
    i = pl.program_id(0)
    tr = u_ref.shape[0]
    hr = halo_ref.shape[0]
    mm_dt = F32 if decode else BF16
    u = u_ref[...].astype(F32)
    halo = halo_ref[...].astype(F32)
    if not decode:
        halo = jnp.where(i % tiles_per_seq == 0, 0.0, halo)
    ext = jnp.concatenate([halo, u], axis=0).astype(mm_dt)
    r = _iota((tr, hr + tr), 0)
    c = _iota((tr, hr + tr), 1) - hr
    rowpos = _iota((tr, 1), 0)
    pos = ((i % tiles_per_seq) * tr + rowpos).astype(F32)
    outs = []
    for g, w in enumerate(A_WINDOWS):
        lo, hi = g * A_GROUP_DIM, (g + 1) * A_GROUP_DIM
        band = ((c >= r - (w - 1)) & (c <= r)).astype(mm_dt)
        if decode:
            win_sum = jnp.dot(band, ext[:, lo:hi], preferred_element_type=F32, precision=HIGHEST)
            count = float(w)
        else:
            win_sum = jnp.dot(band, ext[:, lo:hi], preferred_element_type=F32)
            count = jnp.minimum(pos + 1.0, float(w))
        pooled = win_sum / count - u[:, lo:hi]
        outs.append(jnp.dot(pooled.astype(mm_dt), pw_ref[g].astype(mm_dt), preferred_element_type=F32))
    mixed = jnp.concatenate(outs, axis=1) * sc_ref[...]
    o_ref[...] = (mixed * _silu(g_ref[...].astype(F32))).astype(o_ref.dtype)


def _pool_branch(p_all, pool_w, pool_scale, *, seq_len, state=None):
    rows = p_all.shape[0]
    decode = state is not None
    wdt = F32 if decode else BF16
    if decode:
        tr, tps = seq_len, 1
        halo_spec = pl.BlockSpec((None, 16, A_WIDTH), lambda i: (i, 0, 0))
        halo_arr = state
    else:
        tr = _pick_tile(seq_len, 256)
        tps = seq_len // tr
        hb = LANES
        assert tr % hb == 0
        halo_spec = pl.BlockSpec((hb, A_WIDTH), lambda i: (jnp.maximum(i * (tr // hb) - 1, 0), OFF_AU // A_WIDTH))
        halo_arr = p_all
    return pl.pallas_call(
        functools.partial(_pool_kernel, tiles_per_seq=tps, decode=decode),
        grid=(rows // tr,),
        in_specs=[
            pl.BlockSpec((tr, A_WIDTH), lambda i: (i, OFF_AU // A_WIDTH)),
            pl.BlockSpec((tr, A_WIDTH), lambda i: (i, OFF_AG // A_WIDTH)),
            halo_spec,
            pl.BlockSpec((A_GROUPS, A_GROUP_DIM, A_GROUP_DIM), lambda i: (0, 0, 0)),
            pl.BlockSpec((1, A_WIDTH), lambda i: (0, 0)),
        ],
        out_specs=pl.BlockSpec((tr, A_WIDTH), lambda i: (i, 0)),
        out_shape=jax.ShapeDtypeStruct((rows, A_WIDTH), p_all.dtype),
        compiler_params=_cparams("arbitrary"),
        name="pool_decode" if decode else "pool_prompt",
    )(p_all, p_all, halo_arr, pool_w.astype(wdt), pool_scale.reshape(1, A_WIDTH))


def _gla_kernel(q_ref, k_ref, v_ref, g_ref, r_ref, wa_ref, ba_ref, gn_ref, s0_ref,
                o_ref, so_ref, s_scr, *, chunk, decode):
    tb = pl.program_id(1)
    nb = pl.num_programs(1)
    tbk = q_ref.shape[0]
    mm_dt = F32 if decode else BF16

    @pl.when(tb == 0)
    def _():
        s_scr[...] = s0_ref[...]

    logit = jnp.dot(r_ref[...], wa_ref[...], preferred_element_type=F32, precision=HIGHEST) + ba_ref[...]
    log_a = _log_sigmoid(logit) / B_GATE_TEMP
    tri = (_iota((chunk, chunk), 0) >= _iota((chunk, chunk), 1))
    tri_f = tri.astype(F32)
    scale = B_DK ** -0.5
    for c in range(tbk // chunk):
        rs = slice(c * chunk, (c + 1) * chunk)
        b = jnp.dot(tri_f, log_a[rs], preferred_element_type=F32, precision=HIGHEST)
        b_last = b[chunk - 1:chunk, :]
        qc = q_ref[rs, :].astype(F32)
        kc = k_ref[rs, :].astype(F32)
        q_t = (qc * scale) * jnp.exp(b)
        k_t = kc * jnp.exp(-b)
        k_end = kc * jnp.exp(b_last - b)
        decay = jnp.exp(b_last)
        for h in range(B_HEADS):
            ks = slice(h * B_DKP, (h + 1) * B_DKP)
            vs = slice(h * B_DVP, (h + 1) * B_DVP)
            qh = q_t[:, ks].astype(mm_dt)
            att = lax.dot_general(qh, k_t[:, ks].astype(mm_dt), (((1,), (1,)), ((), ())),
                                  preferred_element_type=F32)
            att = jnp.where(tri, att, 0.0)
            vh = v_ref[rs, vs].astype(mm_dt)
            st = s_scr[h]
            o = jnp.dot(att.astype(mm_dt), vh, preferred_element_type=F32)
            o = o + lax.dot_general(qh, st.astype(mm_dt), (((1,), (1,)), ((), ())),
                                    preferred_element_type=F32)
            kv_t = lax.dot_general(vh, k_end[:, ks].astype(mm_dt), (((0,), (0,)), ((), ())),
                                   preferred_element_type=F32)
            s_scr[h] = st * decay[:, ks] + kv_t
            r = lax.rsqrt(jnp.sum(o * o, axis=-1, keepdims=True) * (1.0 / B_DV) + EPS)
            on = (o * r) * gn_ref[:, vs]
            o_ref[rs, vs] = (on * _silu(g_ref[rs, vs].astype(F32))).astype(o_ref.dtype)

    @pl.when(tb == nb - 1)
    def _():
        so_ref[...] = s_scr[...]


def _gla_branch(p_all, f_all, wa_p, ba_p, gn_p, s0_t, *, n_seq, seq_len, decode):
    rows = p_all.shape[0]
    chunk = math.gcd(seq_len, B_CHUNK)
    tbk = _pick_tile(seq_len, 256)
    nb = seq_len // tbk
    kw = B_HEADS * B_DKP
    vw = B_HEADS * B_DVP
    return pl.pallas_call(
        functools.partial(_gla_kernel, chunk=chunk, decode=decode),
        grid=(n_seq, nb),
        in_specs=[
            pl.BlockSpec((tbk, kw), lambda s, t: (s * nb + t, OFF_BQ // kw)),
            pl.BlockSpec((tbk, kw), lambda s, t: (s * nb + t, OFF_BK // kw)),
            pl.BlockSpec((tbk, vw), lambda s, t: (s * nb + t, OFF_BV // vw)),
            pl.BlockSpec((tbk, vw), lambda s, t: (s * nb + t, OFF_BG // vw)),
            pl.BlockSpec((tbk, LANES), lambda s, t: (s * nb + t, 0)),
            pl.BlockSpec((LANES, kw), lambda s, t: (0, 0)),
            pl.BlockSpec((1, kw), lambda s, t: (0, 0)),
            pl.BlockSpec((1, vw), lambda s, t: (0, 0)),
            pl.BlockSpec((None, B_HEADS, B_DVP, B_DKP), lambda s, t: (s, 0, 0, 0)),
        ],
        out_specs=[
            pl.BlockSpec((tbk, vw), lambda s, t: (s * nb + t, 0)),
            pl.BlockSpec((None, B_HEADS, B_DVP, B_DKP), lambda s, t: (s, 0, 0, 0)),
        ],
        out_shape=[
            jax.ShapeDtypeStruct((rows, vw), p_all.dtype),
            jax.ShapeDtypeStruct((n_seq, B_HEADS, B_DVP, B_DKP), F32),
        ],
        scratch_shapes=[pltpu.VMEM((B_HEADS, B_DVP, B_DKP), F32)],
        compiler_params=_cparams("arbitrary", "arbitrary"),
        name="gla_decode" if decode else "gla_prompt",
    )(p_all, p_all, p_all, p_all, f_all, wa_p, ba_p, gn_p, s0_t)


def _foxprep_kernel(cq_ref, ck_ref, cv_ref, cf_ref, qn_ref, kn_ref, bf_ref,
                    qa_ref, ka_ref, ko_ref, vo_ref, lf_ref, carry_scr, *, tiles_per_seq):
    i = pl.program_id(0)
    tr = cq_ref.shape[0]
    hd = C_HEAD_DIM
    lane = _iota((tr, LANES), 1)

    lf = jnp.where(lane < C_HEADS, _log_sigmoid(cf_ref[...] + bf_ref[...]), 0.0)
    lf_ref[...] = lf

    @pl.when(i % tiles_per_seq == 0)
    def _():
        carry_scr[...] = jnp.zeros_like(carry_scr)

    tri = (_iota((tr, tr), 0) >= _iota((tr, tr), 1)).astype(F32)
    cum = jnp.dot(tri, lf, preferred_element_type=F32, precision=HIGHEST) + carry_scr[...]
    carry_scr[...] = cum[tr - 1:tr, :]

    def qk_norm(x, g_ref):
        r = lax.rsqrt(jnp.mean(x * x, axis=-1, keepdims=True) + EPS)
        return (x * r) * g_ref[...]

    one = jnp.ones((tr, LANES), F32)
    zero = jnp.zeros((tr, LANES), F32)
    for h in range(C_HEADS):
        cs = slice(h * hd, (h + 1) * hd)
        ch = jnp.sum(jnp.where(lane == h, cum, 0.0), axis=-1, keepdims=True)
        hi = ch.astype(BF16).astype(F32)
        r1 = ch - hi
        mid = r1.astype(BF16).astype(F32)
        lo = r1 - mid
        qn = qk_norm(cq_ref[:, cs].astype(F32), qn_ref)
        kn = qk_norm(ck_ref[:, cs].astype(F32), kn_ref)
        aux_q = jnp.where(lane == 0, hi, jnp.where(lane == 1, mid, jnp.where(lane == 2, lo,
                          jnp.where(lane < 6, one, zero))))
        aux_k = jnp.where(lane < 3, one, jnp.where(lane == 3, -hi, jnp.where(lane == 4, -mid,
                          jnp.where(lane == 5, -lo, zero))))
        a0 = 2 * h * hd
        qa_ref[:, a0:a0 + hd] = (qn * (hd ** -0.5)).astype(qa_ref.dtype)
        qa_ref[:, a0 + hd:a0 + 2 * hd] = aux_q.astype(qa_ref.dtype)
        ka_ref[:, a0:a0 + hd] = kn.astype(ka_ref.dtype)
        ka_ref[:, a0 + hd:a0 + 2 * hd] = aux_k.astype(ka_ref.dtype)
        ko_ref[:, cs] = kn
    vo_ref[...] = cv_ref[...].astype(F32)


def _fox_prep(p_all, f_all, fox_q_norm, fox_k_norm, bf_p, *, seq_len):
    rows = p_all.shape[0]
    tr = _pick_tile(seq_len, 256)
    tps = seq_len // tr
    hd = C_HEAD_DIM
    cw = C_WIDTH
    aw = C_HEADS * 2 * hd
    return pl.pallas_call(
        functools.partial(_foxprep_kernel, tiles_per_seq=tps),
        grid=(rows // tr,),
        in_specs=[
            pl.BlockSpec((tr, cw), lambda i: (i, OFF_CQ // cw)),
            pl.BlockSpec((tr, cw), lambda i: (i, OFF_CK // cw)),
            pl.BlockSpec((tr, cw), lambda i: (i, OFF_CV // cw)),
            pl.BlockSpec((tr, LANES), lambda i: (i, 1)),
            pl.BlockSpec((1, hd), lambda i: (0, 0)),
            pl.BlockSpec((1, hd), lambda i: (0, 0)),
            pl.BlockSpec((1, LANES), lambda i: (0, 0)),
        ],
        out_specs=[
            pl.BlockSpec((tr, aw), lambda i: (i, 0)),
            pl.BlockSpec((tr, aw), lambda i: (i, 0)),
            pl.BlockSpec((tr, cw), lambda i: (i, 0)),
            pl.BlockSpec((tr, cw), lambda i: (i, 0)),
            pl.BlockSpec((tr, LANES), lambda i: (i, 0)),
        ],
        out_shape=[
            jax.ShapeDtypeStruct((rows, aw), p_all.dtype),
            jax.ShapeDtypeStruct((rows, aw), p_all.dtype),
            jax.ShapeDtypeStruct((rows, cw), F32),
            jax.ShapeDtypeStruct((rows, cw), F32),
            jax.ShapeDtypeStruct((rows, LANES), F32),
        ],
        scratch_shapes=[pltpu.VMEM((1, LANES), F32)],
        compiler_params=_cparams("arbitrary"),
        name="fox_prep",
    )(p_all, p_all, p_all, f_all, fox_q_norm.reshape(1, hd), fox_k_norm.reshape(1, hd), bf_p)


def _flash_kernel(q_ref, k_ref, v_ref, g_ref, o_ref, m_scr, l_scr, acc_scr):
    i = pl.program_id(2)
    t = q_ref.shape[0]
    m_scr[...] = jnp.full_like(m_scr, -jnp.inf)
    l_scr[...] = jnp.zeros_like(l_scr)
    acc_scr[...] = jnp.zeros_like(acc_scr)

    def step(j, diagonal):
        k0 = pl.multiple_of(j * t, t)
        s = lax.dot_general(q_ref[...], k_ref[pl.ds(k0, t), :], (((1,), (1,)), ((), ())),
                            preferred_element_type=F32)
        if diagonal:
            s = jnp.where(_iota((t, t), 0) >= _iota((t, t), 1), s, -jnp.inf)
        m_prev = m_scr[...]
        m_new = jnp.maximum(m_prev, jnp.max(s, axis=-1, keepdims=True))
        alpha = jnp.exp(m_prev - m_new)
        p = jnp.exp(s - m_new)
        l_scr[...] = alpha * l_scr[...] + jnp.sum(p, axis=-1, keepdims=True)
        acc_scr[...] = alpha * acc_scr[...] + jnp.dot(p.astype(BF16), v_ref[pl.ds(k0, t), :],
                                                      preferred_element_type=F32)
        m_scr[...] = m_new

    def pair(jj, carry):
        step(2 * jj, False)
        step(2 * jj + 1, False)
        return carry

    lax.fori_loop(0, i // 2, pair, 0)

    @pl.when(i % 2 == 1)
    def _():
        step(i - 1, False)

    step(i, True)
    out = acc_scr[...] / l_scr[...]
    o_ref[...] = (out * _silu(g_ref[...].astype(F32))).astype(BF16)


def _fox_flash(q_aug, k_aug, p_all, *, n_seq, seq_len):
    rows = q_aug.shape[0]
    t = _pick_tile(seq_len, 512)
    nt = seq_len // t
    hd = C_HEAD_DIM
    aw = 2 * hd
    return pl.pallas_call(
        _flash_kernel,
        grid=(n_seq, C_HEADS, nt),
        in_specs=[
            pl.BlockSpec((t, aw), lambda b, h, i: (b * nt + i, h)),
            pl.BlockSpec((seq_len, aw), lambda b, h, i: (b, h)),
            pl.BlockSpec((seq_len, hd), lambda b, h, i: (b, OFF_CV // hd + h)),
            pl.BlockSpec((t, hd), lambda b, h, i: (b * nt + i, OFF_CG // hd + h)),
        ],
        out_specs=pl.BlockSpec((t, hd), lambda b, h, i: (b * nt + i, h)),
        out_shape=jax.ShapeDtypeStruct((rows, C_WIDTH), BF16),
        scratch_shapes=[pltpu.VMEM((t, 1), F32), pltpu.VMEM((t, 1), F32), pltpu.VMEM((t, hd), F32)],
        compiler_params=_cparams("parallel", "parallel", "arbitrary"),
        name="fox_flash",
    )(q_aug, k_aug, p_all, p_all)


def _decbias_kernel(pt_ref, *refs, gpp):
    lf_refs = refs[:gpp]
    lfn_ref, ck_ref, ckn_ref, carry_scr = refs[gpp:]
    s = pl.program_id(0)
    t = pl.program_id(1)
    nt = pl.num_programs(1)
    upper = (_iota((LANES, LANES), 0) <= _iota((LANES, LANES), 1)).astype(F32)
    sub = _iota((SUBLANES, LANES), 0)

    @pl.when(t == 0)
    def _():
        carry_scr[...] = jnp.zeros_like(carry_scr)

    carry = carry_scr[...]
    for g in range(gpp):
        prow = jnp.bitwise_and(pt_ref[s, t * gpp + g], SUBLANES - 1)
        lf8 = jnp.zeros((SUBLANES, LANES), F32)
        for h in range(C_HEADS):
            lf8 = jnp.where(sub == h, jnp.broadcast_to(lf_refs[g][h, pl.ds(prow, 1), :], (SUBLANES, LANES)), lf8)
        cum = jnp.dot(lf8, upper, preferred_element_type=F32, precision=HIGHEST) + carry
        ck_ref[:, g * PAGE_SIZE:(g + 1) * PAGE_SIZE] = cum
        carry = jnp.broadcast_to(cum[:, LANES - 1:LANES], cum.shape)
    carry_scr[...] = carry

    @pl.when(t == nt - 1)
    def _():
        ckn_ref[...] = jnp.dot(lfn_ref[...], upper, preferred_element_type=F32, precision=HIGHEST) + carry


def _decode_bias(page_table, logf_hp, lfn_t, *, layer):
    n_seq, npg = page_table.shape
    gpp = _pick_tile(npg, 16, 1)

    def lf_spec(g):
        return pl.BlockSpec((None, C_HEADS, SUBLANES, PAGE_SIZE),
                            lambda s, t, pt: (layer, 0, jnp.right_shift(pt[s, t * gpp + g], 3), 0))

    return pl.pallas_call(
        functools.partial(_decbias_kernel, gpp=gpp),
        grid_spec=pltpu.PrefetchScalarGridSpec(
            num_scalar_prefetch=1,
            grid=(n_seq, npg // gpp),
            in_specs=[lf_spec(g) for g in range(gpp)]
            + [pl.BlockSpec((None, SUBLANES, LANES), lambda s, t, pt: (s, 0, 0))],
            out_specs=[
                pl.BlockSpec((None, SUBLANES, gpp * PAGE_SIZE), lambda s, t, pt: (s, 0, t)),
                pl.BlockSpec((None, SUBLANES, LANES), lambda s, t, pt: (s, 0, 0)),
            ],
            scratch_shapes=[pltpu.VMEM((SUBLANES, LANES), F32)],
        ),
        out_shape=[
            jax.ShapeDtypeStruct((n_seq, SUBLANES, npg * PAGE_SIZE), F32),
            jax.ShapeDtypeStruct((n_seq, SUBLANES, LANES), F32),
        ],
        compiler_params=_cparams("arbitrary", "arbitrary"),
        name="decode_bias",
    )(page_table, *([logf_hp] * gpp), lfn_t)


DEC_QROWS = 16


def _decode_kernel(pt_ref, qa_ref, *refs, gpp, dec_seq):
    del pt_ref
    k_refs = refs[:gpp]
    v_refs = refs[gpp:2 * gpp]
    ck_ref, ckn_ref, kn_ref, vn_ref, g_ref, o_ref, q_scr, m_scr, l_scr, acc_scr = refs[2 * gpp:]
    t = pl.program_id(1)
    nt = pl.num_programs(1)
    nq = dec_seq
    qp = DEC_QROWS
    nrow = C_HEADS * qp
    hd = C_HEAD_DIM
    row = _iota((nrow, LANES), 0)
    lane = _iota((nrow, LANES), 1)
    qi = jnp.bitwise_and(row, qp - 1)

    @pl.when(t == 0)
    def _():
        qa = qa_ref[...].astype(F32)
        zero = jnp.zeros((qp - nq, hd), F32)
        q_scr[...] = jnp.concatenate(
            [jnp.concatenate([qa[:, h * 2 * hd:h * 2 * hd + hd], zero], axis=0) for h in range(C_HEADS)],
            axis=0).astype(BF16)
        m_scr[...] = jnp.full_like(m_scr, -jnp.inf)
        l_scr[...] = jnp.zeros_like(l_scr)
        acc_scr[...] = jnp.zeros_like(acc_scr)

    def head_rows(get_row):
        return jnp.concatenate(
            [jnp.broadcast_to(get_row(h), (qp, get_row(h).shape[1])) for h in range(C_HEADS)], axis=0)

    ckn_rows = head_rows(lambda h: ckn_ref[h:h + 1, :])
    c_q = jnp.sum(jnp.where(lane == qi, ckn_rows, 0.0), axis=-1, keepdims=True)

    def attend(k_of, v_of, ck_rows, mask):
        s = jnp.concatenate(
            [lax.dot_general(q_scr[h * qp:(h + 1) * qp, :], k_of(h), (((1,), (1,)), ((), ())),
                             preferred_element_type=F32) for h in range(C_HEADS)], axis=0)
        s = s + (c_q - ck_rows)
        if mask is not None:
            s = jnp.where(mask, s, -jnp.inf)
        m_prev = m_scr[...]
        m_new = jnp.maximum(m_prev, jnp.max(s, axis=-1, keepdims=True))
        alpha = jnp.exp(m_prev - m_new)
        pr = jnp.exp(s - m_new)
        l_scr[...] = alpha * l_scr[...] + jnp.sum(pr, axis=-1, keepdims=True)
        prb = pr.astype(BF16)
        o = jnp.concatenate(
            [jnp.dot(prb[h * qp:(h + 1) * qp, :], v_of(h), preferred_element_type=F32) for h in range(C_HEADS)],
            axis=0)
        acc_scr[...] = alpha * acc_scr[...] + o
        m_scr[...] = m_new

    attend(lambda h: jnp.concatenate([k_refs[g][h] for g in range(gpp)], axis=0).astype(BF16),
           lambda h: jnp.concatenate([v_refs[g][h] for g in range(gpp)], axis=0).astype(BF16),
           head_rows(lambda h: ck_ref[h:h + 1, :]), None)

    @pl.when(t == nt - 1)
    def _():
        pad = jnp.zeros((PAGE_SIZE - nq, hd), F32)
        kn = kn_ref[...]
        vn = vn_ref[...]
        attend(lambda h: jnp.concatenate([kn[:, h * hd:(h + 1) * hd], pad], axis=0).astype(BF16),
               lambda h: jnp.concatenate([vn[:, h * hd:(h + 1) * hd], pad], axis=0).astype(BF16),
               ckn_rows, lane <= qi)
        out = acc_scr[...] / l_scr[...]
        gate = _silu(g_ref[...].astype(F32))
        for h in range(C_HEADS):
            cs = slice(h * hd, (h + 1) * hd)
            o_ref[:, cs] = (out[h * qp:h * qp + nq, :] * gate[:, cs]).astype(o_ref.dtype)


def _fox_decode(page_table, q_aug, cache_k_hm, cache_v_hm, ck_past, ck_new, k_new, v_new, p_all, *, layer, dec_seq):
    n_seq, npg = page_table.shape
    rows = q_aug.shape[0]
    nrow = C_HEADS * DEC_QROWS
    gpp = _pick_tile(npg, 8, 1)

    def page_spec(g):
        return pl.BlockSpec((None, None, C_HEADS, PAGE_SIZE, C_HEAD_DIM),
                            lambda s, t, pt: (layer, pt[s, t * gpp + g], 0, 0, 0))

    return pl.pallas_call(
        functools.partial(_decode_kernel, gpp=gpp, dec_seq=dec_seq),
        grid_spec=pltpu.PrefetchScalarGridSpec(
            num_scalar_prefetch=1,
            grid=(n_seq, npg // gpp),
            in_specs=[pl.BlockSpec((dec_seq, C_HEADS * 2 * C_HEAD_DIM), lambda s, t, pt: (s, 0))]
            + [page_spec(g) for g in range(gpp)]
            + [page_spec(g) for g in range(gpp)]
            + [
                pl.BlockSpec((None, SUBLANES, gpp * PAGE_SIZE), lambda s, t, pt: (s, 0, t)),
                pl.BlockSpec((None, SUBLANES, LANES), lambda s, t, pt: (s, 0, 0)),
                pl.BlockSpec((dec_seq, C_WIDTH), lambda s, t, pt: (s, 0)),
                pl.BlockSpec((dec_seq, C_WIDTH), lambda s, t, pt: (s, 0)),
                pl.BlockSpec((dec_seq, C_WIDTH), lambda s, t, pt: (s, OFF_CG // C_WIDTH)),
            ],
            out_specs=pl.BlockSpec((dec_seq, C_WIDTH), lambda s, t, pt: (s, 0)),
            scratch_shapes=[
                pltpu.VMEM((nrow, C_HEAD_DIM), BF16),
                pltpu.VMEM((nrow, 1), F32),
                pltpu.VMEM((nrow, 1), F32),
                pltpu.VMEM((nrow, C_HEAD_DIM), F32),
            ],
        ),
        out_shape=jax.ShapeDtypeStruct((rows, C_WIDTH), p_all.dtype),
        compiler_params=_cparams("arbitrary", "arbitrary"),
        name="fox_decode",
    )(page_table, q_aug, *([cache_k_hm] * gpp), *([cache_v_hm] * gpp), ck_past, ck_new, k_new, v_new, p_all)


def _merge_kernel(x_ref, a_ref, b_ref, c_ref, mg0_ref, mg1_ref, mg2_ref, bm_ref,
                  wa_ref, wb_ref, wc_ref, wo_ref, y_ref):
    d = D_MODEL

    def gated(o_ref, w_ref, mg_ref, k):
        gate = jax.nn.sigmoid(mg_ref[...].astype(F32) + bm_ref[:, k * d:(k + 1) * d])
        return gate * jnp.dot(o_ref[...].astype(BF16), w_ref[...], preferred_element_type=F32)

    merged = gated(a_ref, wa_ref, mg0_ref, 0) + gated(b_ref, wb_ref, mg1_ref, 1) + gated(c_ref, wc_ref, mg2_ref, 2)
    y_ref[...] = x_ref[...] + jnp.dot(merged.astype(BF16), wo_ref[...], preferred_element_type=F32)


def _merge(x2, a_o, b_o, c_o, p_all, b_merge, wa, wb, wc, wo):
    rows = x2.shape[0]
    tm = _pick_tile(rows, 256)
    d = D_MODEL
    bw = B_HEADS * B_DVP

    def const(shape):
        return pl.BlockSpec(shape, lambda i: (0, 0), pipeline_mode=pl.Buffered(1))

    return pl.pallas_call(
        _merge_kernel,
        grid=(rows // tm,),
        in_specs=[
            pl.BlockSpec((tm, d), lambda i: (i, 0)),
            pl.BlockSpec((tm, A_WIDTH), lambda i: (i, 0)),
            pl.BlockSpec((tm, bw), lambda i: (i, 0)),
            pl.BlockSpec((tm, C_WIDTH), lambda i: (i, 0)),
            pl.BlockSpec((tm, d), lambda i: (i, OFF_MG // d + 0)),
            pl.BlockSpec((tm, d), lambda i: (i, OFF_MG // d + 1)),
            pl.BlockSpec((tm, d), lambda i: (i, OFF_MG // d + 2)),
            const((1, N_BRANCH * d)),
            const((A_WIDTH, d)),
            const((bw, d)),
            const((C_WIDTH, d)),
            const((d, d)),
        ],
        out_specs=pl.BlockSpec((tm, d), lambda i: (i, 0)),
        out_shape=jax.ShapeDtypeStruct((rows, d), F32),
        compiler_params=_cparams("parallel"),
        name="merge_out",
    )(x2, a_o, b_o, c_o, p_all, p_all, p_all, b_merge.reshape(1, N_BRANCH * d), wa, wb, wc, wo)


def _pad_heads(x, n_heads, d, dp, axis):
    shp = x.shape
    x = x.reshape(shp[:axis] + (n_heads, d) + shp[axis + 1:])
    pad = [(0, 0)] * x.ndim
    pad[axis + 1] = (0, dp - d)
    x = jnp.pad(x, pad)
    return x.reshape(shp[:axis] + (n_heads * dp,) + shp[axis + 1:])


def _pack_w_in(w):
    pts = []
    acc = 0
    for s in SPLIT_SIZES[:-1]:
        acc += s
        pts.append(acc)
    (a_u, a_g, b_q, b_k, b_v, b_g, b_r, c_q, c_k, c_v, c_g, c_f, m_g) = jnp.split(w, pts, axis=1)
    lane_pad = lambda x: jnp.pad(x, ((0, 0), (0, LANES - x.shape[1])))
    cols = [m_g, c_g, c_q, c_k, c_v, a_u, a_g,
            _pad_heads(b_q, B_HEADS, B_DK, B_DKP, 1), _pad_heads(b_k, B_HEADS, B_DK, B_DKP, 1),
            _pad_heads(b_v, B_HEADS, B_DV, B_DVP, 1), _pad_heads(b_g, B_HEADS, B_DV, B_DVP, 1),
            lane_pad(b_r), lane_pad(c_f)]
    packed = jnp.concatenate(cols, axis=1)
    assert packed.shape[1] == OFF_SM + SM_WIDTH
    return jnp.pad(packed, ((0, 0), (0, NP - packed.shape[1]))).astype(BF16)


def _layer_params(l, norm_g, w_in, pool_w, pool_scale, gla_w_a2, gla_b_a, gla_norm, fox_b_f, fox_q_norm,
                  fox_k_norm, w_branch_a, w_branch_b, w_branch_c, b_merge, w_out):
    wa2 = _pad_heads(gla_w_a2[l], B_HEADS, B_DK, B_DKP, 1)
    return dict(
        norm_g=norm_g[l],
        w_packed=_pack_w_in(w_in[l]),
        pool_w=pool_w[l],
        pool_scale=pool_scale[l],
        wa2=jnp.pad(wa2, ((0, LANES - B_GATE_RANK), (0, 0))),
        ba=_pad_heads(gla_b_a[l].reshape(1, -1), B_HEADS, B_DK, B_DKP, 1),
        gn=_pad_heads(gla_norm[l].reshape(1, -1), B_HEADS, B_DV, B_DVP, 1),
        bf=jnp.pad(fox_b_f[l].reshape(1, -1), ((0, 0), (0, LANES - C_HEADS))),
        qn=fox_q_norm[l],
        kn=fox_k_norm[l],
        wa=w_branch_a[l].astype(BF16),
        wb=_pad_heads(w_branch_b[l], B_HEADS, B_DV, B_DVP, 0).astype(BF16),
        wc=w_branch_c[l].astype(BF16),
        b_merge=b_merge[l],
        wo=w_out[l].astype(BF16),
    )


def _state_to_kernel(s):
    st = jnp.swapaxes(s, -1, -2)
    return jnp.pad(st, ((0, 0), (0, 0), (0, B_DVP - B_DV), (0, B_DKP - B_DK)))


def _state_from_kernel(st):
    return jnp.swapaxes(st[:, :, :B_DV, :B_DK], -1, -2)


def kernel(x_prompt, x_sample, cache_k, cache_v, cache_logf, state_gla, state_pool, page_table, norm_g, w_in, pool_w, pool_scale, gla_w_a2, gla_b_a, gla_norm, fox_b_f, fox_q_norm, fox_k_norm, w_branch_a, w_branch_b, w_branch_c, b_merge, w_out):
    depth = w_in.shape[0]
    bp, seq, d = x_prompt.shape
    n_seq, dec_seq, _ = x_sample.shape
    assert d == D_MODEL and dec_seq == SUBLANES and seq % LANES == 0
    yp = x_prompt.reshape(bp * seq, d)
    ys = x_sample.reshape(n_seq * dec_seq, d)
    cache_k_hm = jnp.transpose(cache_k, (0, 1, 3, 2, 4))
    cache_v_hm = jnp.transpose(cache_v, (0, 1, 3, 2, 4))
    logf_hp = jnp.transpose(cache_logf, (0, 3, 1, 2))
    zero_state = jnp.zeros((bp, B_HEADS, B_DVP, B_DKP), F32)
    outs = {k: [] for k in ("kp", "vp", "lp", "gp", "pp", "ks", "vs", "ls", "gs", "ps")}
    for l in range(depth):
        prm = _layer_params(l, norm_g, w_in, pool_w, pool_scale, gla_w_a2, gla_b_a, gla_norm, fox_b_f,
                            fox_q_norm, fox_k_norm, w_branch_a, w_branch_b, w_branch_c, b_merge, w_out)

        p_all, f_all = _inproj(yp, prm["norm_g"], prm["w_packed"], BF16)
        a_o = _pool_branch(p_all, prm["pool_w"], prm["pool_scale"], seq_len=seq)
        b_o, s_fin = _gla_branch(p_all, f_all, prm["wa2"], prm["ba"], prm["gn"], zero_state,
                                 n_seq=bp, seq_len=seq, decode=False)
        q_aug, k_aug, k_n, v_n, lf = _fox_prep(p_all, f_all, prm["qn"], prm["kn"], prm["bf"], seq_len=seq)
        c_o = _fox_flash(q_aug, k_aug, p_all, n_seq=bp, seq_len=seq)
        yp = _merge(yp, a_o, b_o, c_o, p_all, prm["b_merge"], prm["wa"], prm["wb"], prm["wc"], prm["wo"])
        outs["kp"].append(k_n.reshape(bp, seq, C_HEADS, C_HEAD_DIM))
        outs["vp"].append(v_n.reshape(bp, seq, C_HEADS, C_HEAD_DIM))
        outs["lp"].append(lf[:, :C_HEADS].reshape(bp, seq, C_HEADS))
        outs["gp"].append(_state_from_kernel(s_fin))
        a_u = p_all[:, OFF_AU:OFF_AU + A_WIDTH].reshape(bp, seq, A_WIDTH)
        outs["pp"].append(a_u[:, seq - A_BUF:].astype(F32))

        p_s, f_s = _inproj(ys, prm["norm_g"], prm["w_packed"], F32)
        pool_state = jnp.pad(state_pool[l], ((0, 0), (1, 0), (0, 0)))
        a_s = _pool_branch(p_s, prm["pool_w"], prm["pool_scale"], seq_len=dec_seq, state=pool_state)
        b_s, s_fin_s = _gla_branch(p_s, f_s, prm["wa2"], prm["ba"], prm["gn"], _state_to_kernel(state_gla[l]),
                                   n_seq=n_seq, seq_len=dec_seq, decode=True)
        q_aug_s, _, k_ns, v_ns, lf_s = _fox_prep(p_s, f_s, prm["qn"], prm["kn"], prm["bf"], seq_len=dec_seq)
        lfn_t = jnp.swapaxes(lf_s.reshape(n_seq, dec_seq, LANES)[:, :, :SUBLANES], 1, 2)
        lfn_t = jnp.pad(lfn_t, ((0, 0), (0, 0), (0, LANES - dec_seq)))
        ck_past, ck_new = _decode_bias(page_table, logf_hp, lfn_t, layer=l)
        c_s = _fox_decode(page_table, q_aug_s, cache_k_hm, cache_v_hm, ck_past, ck_new, k_ns, v_ns, p_s,
                          layer=l, dec_seq=dec_seq)
        ys = _merge(ys, a_s, b_s, c_s, p_s, prm["b_merge"], prm["wa"], prm["wb"], prm["wc"], prm["wo"])
        outs["ks"].append(k_ns.reshape(n_seq, dec_seq, C_HEADS, C_HEAD_DIM))
        outs["vs"].append(v_ns.reshape(n_seq, dec_seq, C_HEADS, C_HEAD_DIM))
        outs["ls"].append(lf_s[:, :C_HEADS].reshape(n_seq, dec_seq, C_HEADS))
        outs["gs"].append(_state_from_kernel(s_fin_s))
        a_us = p_s[:, OFF_AU:OFF_AU + A_WIDTH].reshape(n_seq, dec_seq, A_WIDTH).astype(F32)
        outs["ps"].append(jnp.concatenate([state_pool[l], a_us], axis=1)[:, -A_BUF:])

    st = lambda k: jnp.stack(outs[k])
    return (yp.reshape(bp, seq, d), ys.reshape(n_seq, dec_seq, d),
            st("kp"), st("vp"), st("lp"), st("gp"), st("pp"),
            st("ks"), st("vs"), st("ls"), st("gs"), st("ps"))
```

```python
import functools
import math

import jax
import jax.numpy as jnp
from jax import lax
from jax.experimental import pallas as pl
from jax.experimental.pallas import tpu as pltpu

F32 = jnp.float32
BF16 = jnp.bfloat16
HIGHEST = lax.Precision.HIGHEST

D_MODEL = 2048
A_WIDTH = 512
A_WINDOWS = (2, 4, 8, 16)
A_GROUPS = 4
A_GROUP_DIM = 128
A_BUF = 15
B_HEADS = 4
B_DK = 96
B_DV = 192
B_GATE_RANK = 16
B_GATE_TEMP = 16.0
B_CHUNK = 32
C_HEADS = 6
C_HEAD_DIM = 128
C_WIDTH = 768
N_BRANCH = 3
EPS = 1e-6
PAGE_SIZE = 128
SPLIT_SIZES = (512, 512, 384, 384, 768, 768, 16, 768, 768, 768, 768, 6, 6144)

LANES = 128
SUBLANES = 8
VMEM_LIMIT_BYTES = 56 * 1024 * 1024

B_DKP = 128
B_DVP = 256

OFF_MG = 0
OFF_CG = 6144
OFF_CQ = OFF_CG + C_WIDTH
OFF_CK = OFF_CQ + C_WIDTH
OFF_CV = OFF_CK + C_WIDTH
OFF_AU = OFF_CV + C_WIDTH
OFF_AG = OFF_AU + A_WIDTH
OFF_BQ = OFF_AG + A_WIDTH
OFF_BK = OFF_BQ + B_HEADS * B_DKP
OFF_BV = OFF_BK + B_HEADS * B_DKP
OFF_BG = OFF_BV + B_HEADS * B_DVP
OFF_SM = OFF_BG + B_HEADS * B_DVP
SM_WIDTH = 2 * LANES
PROJ_TN = 1536
NP = 13824
assert OFF_SM + SM_WIDTH <= NP and NP % PROJ_TN == 0
SM_TILE = OFF_SM // PROJ_TN
SM_LOCAL = OFF_SM - SM_TILE * PROJ_TN
assert SM_LOCAL + SM_WIDTH <= PROJ_TN


def _pick_tile(n, pref, align=SUBLANES):
    if n <= pref:
        return n
    t = (pref // align) * align
    while t > align and n % t:
        t -= align
    assert n % t == 0, (n, pref)
    return t


def _cparams(*sem):
    return pltpu.CompilerParams(dimension_semantics=sem, vmem_limit_bytes=VMEM_LIMIT_BYTES)


def _log_sigmoid(x):
    return jnp.minimum(x, 0.0) - jnp.log1p(jnp.exp(-jnp.abs(x)))


def _silu(x):
    return x * jax.nn.sigmoid(x)


def _iota(shape, dim):
    return lax.broadcasted_iota(jnp.int32, shape, dim)


def _inproj_kernel(x_ref, g_ref, w_ref, p_ref, f_ref, h_scr, *, nj, rchunk):
    j = pl.program_id(1)

    @pl.when(j == 0)
    def _():
        def body(c, carry):
            r0 = pl.multiple_of(c * rchunk, rchunk)
            x = x_ref[pl.ds(r0, rchunk), :]
            r = lax.rsqrt(jnp.mean(x * x, axis=-1, keepdims=True) + EPS)
            h_scr[pl.ds(r0, rchunk), :] = ((x * r) * g_ref[...]).astype(BF16)
            return carry

        lax.fori_loop(0, x_ref.shape[0] // rchunk, body, 0)

    acc = lax.dot_general(h_scr[...], w_ref[...], (((1,), (1,)), ((), ())), preferred_element_type=F32)
    p_ref[...] = acc.astype(p_ref.dtype)

    @pl.when(j == nj - 1)
    def _():
        f_ref[...] = acc[:, SM_LOCAL:SM_LOCAL + SM_WIDTH]


def _inproj(x2, norm_g, w_packed, layer, act_dt):
    rows = x2.shape[0]
    tm = _pick_tile(rows, 1024)
    nj = NP // PROJ_TN
    assert SM_TILE == nj - 1
    rchunk = _pick_tile(tm, 128)
    return pl.pallas_call(
        functools.partial(_inproj_kernel, nj=nj, rchunk=rchunk),
        grid=(rows // tm, nj),
        in_specs=[
            pl.BlockSpec((tm, D_MODEL), lambda i, j: (i, 0)),
            pl.BlockSpec((1, D_MODEL), lambda i, j: (0, 0)),
            pl.BlockSpec((None, PROJ_TN, D_MODEL), lambda i, j: (layer, j, 0)),
        ],
        out_specs=[
            pl.BlockSpec((tm, PROJ_TN), lambda i, j: (i, j)),
            pl.BlockSpec((tm, SM_WIDTH), lambda i, j: (i, 0)),
        ],
        out_shape=[
            jax.ShapeDtypeStruct((rows, NP), act_dt),
            jax.ShapeDtypeStruct((rows, SM_WIDTH), F32),
        ],
        scratch_shapes=[pltpu.VMEM((tm, D_MODEL), BF16)],
        compiler_params=_cparams("parallel", "arbitrary"),
        name="inproj",
    )(x2, norm_g.reshape(1, D_MODEL), w_packed)


def _pool_kernel(u_ref, g_ref, halo_ref, pw_ref, sc_ref, o_ref, *, tiles_per_seq, decode):
    i = pl.program_id(0)
    tr = u_ref.shape[0]
    hr = halo_ref.shape[0]
    mm_dt = F32 if decode else BF16
    u = u_ref[...].astype(F32)
    halo = halo_ref[...].astype(F32)
    if not decode:
        halo = jnp.where(i % tiles_per_seq == 0, 0.0, halo)
    ext = jnp.concatenate([halo, u], axis=0).astype(mm_dt)
    r = _iota((tr, hr + tr), 0)
    c = _iota((tr, hr + tr), 1) - hr
    rowpos = _iota((tr, 1), 0)
    pos = ((i % tiles_per_seq) * tr + rowpos).astype(F32)
    outs = []
    for g, w in enumerate(A_WINDOWS):
        lo, hi = g * A_GROUP_DIM, (g + 1) * A_GROUP_DIM
        band = ((c >= r - (w - 1)) & (c <= r)).astype(mm_dt)
        if decode:
            win_sum = jnp.dot(band, ext[:, lo:hi], preferred_element_type=F32, precision=HIGHEST)
            count = float(w)
        else:
            win_sum = jnp.dot(band, ext[:, lo:hi], preferred_element_type=F32)
            count = jnp.minimum(pos + 1.0, float(w))
        pooled = win_sum / count - u[:, lo:hi]
        outs.append(jnp.dot(pooled.astype(mm_dt), pw_ref[g].astype(mm_dt), preferred_element_type=F32))
    mixed = jnp.concatenate(outs, axis=1) * sc_ref[...]
    o_ref[...] = (mixed * _silu(g_ref[...].astype(F32))).astype(o_ref.dtype)


def _pool_branch(p_all, pool_w, pool_scale, *, seq_len, state=None):
    rows = p_all.shape[0]
    decode = state is not None
    wdt = F32 if decode else BF16
    if decode:
        tr, tps = seq_len, 1
        halo_spec = pl.BlockSpec((None, 16, A_WIDTH), lambda i: (i, 0, 0))
        halo_arr = state
    else:
        tr = _pick_tile(seq_len, 256)
        tps = seq_len // tr
        hb = LANES
        assert tr % hb == 0
        halo_spec = pl.BlockSpec((hb, A_WIDTH), lambda i: (jnp.maximum(i * (tr // hb) - 1, 0), OFF_AU // A_WIDTH))
        halo_arr = p_all
    return pl.pallas_call(
        functools.partial(_pool_kernel, tiles_per_seq=tps, decode=decode),
        grid=(rows // tr,),
        in_specs=[
            pl.BlockSpec((tr, A_WIDTH), lambda i: (i, OFF_AU // A_WIDTH)),
            pl.BlockSpec((tr, A_WIDTH), lambda i: (i, OFF_AG // A_WIDTH)),
            halo_spec,
            pl.BlockSpec((A_GROUPS, A_GROUP_DIM, A_GROUP_DIM), lambda i: (0, 0, 0)),
            pl.BlockSpec((1, A_WIDTH), lambda i: (0, 0)),
        ],
        out_specs=pl.BlockSpec((tr, A_WIDTH), lambda i: (i, 0)),
        out_shape=jax.ShapeDtypeStruct((rows, A_WIDTH), p_all.dtype),
        compiler_params=_cparams("arbitrary"),
        name="pool_decode" if decode else "pool_prompt",
    )(p_all, p_all, halo_arr, pool_w.astype(wdt), pool_scale.reshape(1, A_WIDTH))


def _gla_kernel(q_ref, k_ref, v_ref, g_ref, r_ref, wa_ref, ba_ref, gn_ref, s0_ref,
                o_ref, so_ref, s_scr, *, chunk, decode):
    tb = pl.program_id(1)
    nb = pl.num_programs(1)
    tbk = q_ref.shape[0]
    mm_dt = F32 if decode else BF16

    @pl.when(tb == 0)
    def _():
        s_scr[...] = s0_ref[...]

    logit = jnp.dot(r_ref[...], wa_ref[...], preferred_element_type=F32, precision=HIGHEST) + ba_ref[...]
    log_a = _log_sigmoid(logit) / B_GATE_TEMP
    tri = (_iota((chunk, chunk), 0) >= _iota((chunk, chunk), 1))
    tri_f = tri.astype(F32)
    scale = B_DK ** -0.5
    for c in range(tbk // chunk):
        rs = slice(c * chunk, (c + 1) * chunk)
        b = jnp.dot(tri_f, log_a[rs], preferred_element_type=F32, precision=HIGHEST)
        b_last = b[chunk - 1:chunk, :]
        qc = q_ref[rs, :].astype(F32)
        kc = k_ref[rs, :].astype(F32)
        q_t = (qc * scale) * jnp.exp(b)
        k_t = kc * jnp.exp(-b)
        k_end = kc * jnp.exp(b_last - b)
        decay = jnp.exp(b_last)
        for h in range(B_HEADS):
            ks = slice(h * B_DKP, (h + 1) * B_DKP)
            vs = slice(h * B_DVP, (h + 1) * B_DVP)
            qh = q_t[:, ks].astype(mm_dt)
            att = lax.dot_general(qh, k_t[:, ks].astype(mm_dt), (((1,), (1,)), ((), ())),
                                  preferred_element_type=F32)
            att = jnp.where(tri, att, 0.0)
            vh = v_ref[rs, vs].astype(mm_dt)
            st = s_scr[h]
            o = jnp.dot(att.astype(mm_dt), vh, preferred_element_type=F32)
            o = o + lax.dot_general(qh, st.astype(mm_dt), (((1,), (1,)), ((), ())),
                                    preferred_element_type=F32)
            kv_t = lax.dot_general(vh, k_end[:, ks].astype(mm_dt), (((0,), (0,)), ((), ())),
                                   preferred_element_type=F32)
            s_scr[h] = st * decay[:, ks] + kv_t
            r = lax.rsqrt(jnp.sum(o * o, axis=-1, keepdims=True) * (1.0 / B_DV) + EPS)
            on = (o * r) * gn_ref[:, vs]
            o_ref[rs, vs] = (on * _silu(g_ref[rs, vs].astype(F32))).astype(o_ref.dtype)

    @pl.when(tb == nb - 1)
    def _():
        so_ref[...] = s_scr[...]


def _gla_branch(p_all, f_all, wa_p, ba_p, gn_p, s0_t, *, n_seq, seq_len, decode):
    rows = p_all.shape[0]
    chunk = math.gcd(seq_len, B_CHUNK)
    tbk = _pick_tile(seq_len, 256)
    nb = seq_len // tbk
    kw = B_HEADS * B_DKP
    vw = B_HEADS * B_DVP
    return pl.pallas_call(
        functools.partial(_gla_kernel, chunk=chunk, decode=decode),
        grid=(n_seq, nb),
        in_specs=[
            pl.BlockSpec((tbk, kw), lambda s, t: (s * nb + t, OFF_BQ // kw)),
            pl.BlockSpec((tbk, kw), lambda s, t: (s * nb + t, OFF_BK // kw)),
            pl.BlockSpec((tbk, vw), lambda s, t: (s * nb + t, OFF_BV // vw)),
            pl.BlockSpec((tbk, vw), lambda s, t: (s * nb + t, OFF_BG // vw)),
            pl.BlockSpec((tbk, LANES), lambda s, t: (s * nb + t, 0)),
            pl.BlockSpec((LANES, kw), lambda s, t: (0, 0)),
            pl.BlockSpec((1, kw), lambda s, t: (0, 0)),
            pl.BlockSpec((1, vw), lambda s, t: (0, 0)),
            pl.BlockSpec((None, B_HEADS, B_DVP, B_DKP), lambda s, t: (s, 0, 0, 0)),
        ],
        out_specs=[
            pl.BlockSpec((tbk, vw), lambda s, t: (s * nb + t, 0)),
            pl.BlockSpec((None, B_HEADS, B_DVP, B_DKP), lambda s, t: (s, 0, 0, 0)),
        ],
        out_shape=[
            jax.ShapeDtypeStruct((rows, vw), p_all.dtype),
            jax.ShapeDtypeStruct((n_seq, B_HEADS, B_DVP, B_DKP), F32),
        ],
        scratch_shapes=[pltpu.VMEM((B_HEADS, B_DVP, B_DKP), F32)],
        compiler_params=_cparams("arbitrary", "arbitrary"),
        name="gla_decode" if decode else "gla_prompt",
    )(p_all, p_all, p_all, p_all, f_all, wa_p, ba_p, gn_p, s0_t)


def _foxprep_kernel(cq_ref, ck_ref, cv_ref, cf_ref, qn_ref, kn_ref, bf_ref,
                    qa_ref, ka_ref, ko_ref, vo_ref, lf_ref, carry_scr, *, tiles_per_seq):
    i = pl.program_id(0)
    tr = cq_ref.shape[0]
    hd = C_HEAD_DIM
    lane = _iota((tr, LANES), 1)

    lf = jnp.where(lane < C_HEADS, _log_sigmoid(cf_ref[...] + bf_ref[...]), 0.0)
    lf_ref[...] = lf

    @pl.when(i % tiles_per_seq == 0)
    def _():
        carry_scr[...] = jnp.zeros_like(carry_scr)

    tri = (_iota((tr, tr), 0) >= _iota((tr, tr), 1)).astype(F32)
    cum = jnp.dot(tri, lf, preferred_element_type=F32, precision=HIGHEST) + carry_scr[...]
    carry_scr[...] = cum[tr - 1:tr, :]

    def qk_norm(x, g_ref):
        r = lax.rsqrt(jnp.mean(x * x, axis=-1, keepdims=True) + EPS)
        return (x * r) * g_ref[...]

    one = jnp.ones((tr, LANES), F32)
    zero = jnp.zeros((tr, LANES), F32)
    for h in range(C_HEADS):
        cs = slice(h * hd, (h + 1) * hd)
        ch = jnp.sum(jnp.where(lane == h, cum, 0.0), axis=-1, keepdims=True)
        hi = ch.astype(BF16).astype(F32)
        r1 = ch - hi
        mid = r1.astype(BF16).astype(F32)
        lo = r1 - mid
        qn = qk_norm(cq_ref[:, cs].astype(F32), qn_ref)
        kn = qk_norm(ck_ref[:, cs].astype(F32), kn_ref)
        aux_q = jnp.where(lane == 0, hi, jnp.where(lane == 1, mid, jnp.where(lane == 2, lo,
                          jnp.where(lane < 6, one, zero))))
        aux_k = jnp.where(lane < 3, one, jnp.where(lane == 3, -hi, jnp.where(lane == 4, -mid,
                          jnp.where(lane == 5, -lo, zero))))
        a0 = 2 * h * hd
        qa_ref[:, a0:a0 + hd] = (qn * (hd ** -0.5)).astype(qa_ref.dtype)
        qa_ref[:, a0 + hd:a0 + 2 * hd] = aux_q.astype(qa_ref.dtype)
        ka_ref[:, a0:a0 + hd] = kn.astype(ka_ref.dtype)
        ka_ref[:, a0 + hd:a0 + 2 * hd] = aux_k.astype(ka_ref.dtype)
        ko_ref[:, cs] = kn
    vo_ref[...] = cv_ref[...].astype(F32)


def _fox_prep(p_all, f_all, fox_q_norm, fox_k_norm, bf_p, *, seq_len):
    rows = p_all.shape[0]
    tr = _pick_tile(seq_len, 256)
    tps = seq_len // tr
    hd = C_HEAD_DIM
    cw = C_WIDTH
    aw = C_HEADS * 2 * hd
    return pl.pallas_call(
        functools.partial(_foxprep_kernel, tiles_per_seq=tps),
        grid=(rows // tr,),
        in_specs=[
            pl.BlockSpec((tr, cw), lambda i: (i, OFF_CQ // cw)),
            pl.BlockSpec((tr, cw), lambda i: (i, OFF_CK // cw)),
            pl.BlockSpec((tr, cw), lambda i: (i, OFF_CV // cw)),
            pl.BlockSpec((tr, LANES), lambda i: (i, 1)),
            pl.BlockSpec((1, hd), lambda i: (0, 0)),
            pl.BlockSpec((1, hd), lambda i: (0, 0)),
            pl.BlockSpec((1, LANES), lambda i: (0, 0)),
        ],
        out_specs=[
            pl.BlockSpec((tr, aw), lambda i: (i, 0)),
            pl.BlockSpec((tr, aw), lambda i: (i, 0)),
            pl.BlockSpec((tr, cw), lambda i: (i, 0)),
            pl.BlockSpec((tr, cw), lambda i: (i, 0)),
            pl.BlockSpec((tr, LANES), lambda i: (i, 0)),
        ],
        out_shape=[
            jax.ShapeDtypeStruct((rows, aw), p_all.dtype),
            jax.ShapeDtypeStruct((rows, aw), p_all.dtype),
            jax.ShapeDtypeStruct((rows, cw), F32),
            jax.ShapeDtypeStruct((rows, cw), F32),
            jax.ShapeDtypeStruct((rows, LANES), F32),
        ],
        scratch_shapes=[pltpu.VMEM((1, LANES), F32)],
        compiler_params=_cparams("arbitrary"),
        name="fox_prep",
    )(p_all, p_all, p_all, f_all, fox_q_norm.reshape(1, hd), fox_k_norm.reshape(1, hd), bf_p)


def _flash_kernel(q_ref, k_ref, v_ref, g_ref, o_ref, m_scr, l_scr, acc_scr):
    i = pl.program_id(2)
    t = q_ref.shape[0]
    m_scr[...] = jnp.full_like(m_scr, -jnp.inf)
    l_scr[...] = jnp.zeros_like(l_scr)
    acc_scr[...] = jnp.zeros_like(acc_scr)

    def step(j, diagonal):
        k0 = pl.multiple_of(j * t, t)
        s = lax.dot_general(q_ref[...], k_ref[pl.ds(k0, t), :], (((1,), (1,)), ((), ())),
                            preferred_element_type=F32)
        if diagonal:
            s = jnp.where(_iota((t, t), 0) >= _iota((t, t), 1), s, -jnp.inf)
        m_prev = m_scr[...]
        m_new = jnp.maximum(m_prev, jnp.max(s, axis=-1, keepdims=True))
        alpha = jnp.exp(m_prev - m_new)
        p = jnp.exp(s - m_new)
        l_scr[...] = alpha * l_scr[...] + jnp.sum(p, axis=-1, keepdims=True)
        acc_scr[...] = alpha * acc_scr[...] + jnp.dot(p.astype(BF16), v_ref[pl.ds(k0, t), :],
                                                      preferred_element_type=F32)
        m_scr[...] = m_new

    def pair(jj, carry):
        step(2 * jj, False)
        step(2 * jj + 1, False)
        return carry

    lax.fori_loop(0, i // 2, pair, 0)

    @pl.when(i % 2 == 1)
    def _():
        step(i - 1, False)

    step(i, True)
    out = acc_scr[...] / l_scr[...]
    o_ref[...] = (out * _silu(g_ref[...].astype(F32))).astype(BF16)


def _fox_flash(q_aug, k_aug, p_all, *, n_seq, seq_len):
    rows = q_aug.shape[0]
    t = _pick_tile(seq_len, 512)
    nt = seq_len // t
    hd = C_HEAD_DIM
    aw = 2 * hd
    return pl.pallas_call(
        _flash_kernel,
        grid=(n_seq, C_HEADS, nt),
        in_specs=[
            pl.BlockSpec((t, aw), lambda b, h, i: (b * nt + i, h)),
            pl.BlockSpec((seq_len, aw), lambda b, h, i: (b, h)),
            pl.BlockSpec((seq_len, hd), lambda b, h, i: (b, OFF_CV // hd + h)),
            pl.BlockSpec((t, hd), lambda b, h, i: (b * nt + i, OFF_CG // hd + h)),
        ],
        out_specs=pl.BlockSpec((t, hd), lambda b, h, i: (b * nt + i, h)),
        out_shape=jax.ShapeDtypeStruct((rows, C_WIDTH), BF16),
        scratch_shapes=[pltpu.VMEM((t, 1), F32), pltpu.VMEM((t, 1), F32), pltpu.VMEM((t, hd), F32)],
        compiler_params=_cparams("parallel", "parallel", "arbitrary"),
        name="fox_flash",
    )(q_aug, k_aug, p_all, p_all)


def _decbias_kernel(pt_ref, *refs, gpp):
    lf_refs = refs[:gpp]
    lfn_ref, ck_ref, ckn_ref, carry_scr = refs[gpp:]
    s = pl.program_id(0)
    t = pl.program_id(1)
    nt = pl.num_programs(1)
    upper = (_iota((LANES, LANES), 0) <= _iota((LANES, LANES), 1)).astype(F32)
    sub = _iota((SUBLANES, LANES), 0)

    @pl.when(t == 0)
    def _():
        carry_scr[...] = jnp.zeros_like(carry_scr)

    carry = carry_scr[...]
    for g in range(gpp):
        prow = jnp.bitwise_and(pt_ref[s, t * gpp + g], SUBLANES - 1)
        lf8 = jnp.zeros((SUBLANES, LANES), F32)
        for h in range(C_HEADS):
            lf8 = jnp.where(sub == h, jnp.broadcast_to(lf_refs[g][h, pl.ds(prow, 1), :], (SUBLANES, LANES)), lf8)
        cum = jnp.dot(lf8, upper, preferred_element_type=F32, precision=HIGHEST) + carry
        ck_ref[:, g * PAGE_SIZE:(g + 1) * PAGE_SIZE] = cum
        carry = jnp.broadcast_to(cum[:, LANES - 1:LANES], cum.shape)
    carry_scr[...] = carry

    @pl.when(t == nt - 1)
    def _():
        ckn_ref[...] = jnp.dot(lfn_ref[...], upper, preferred_element_type=F32, precision=HIGHEST) + carry


def _decode_bias(page_table, logf_hp, lfn_t, *, layer):
    n_seq, npg = page_table.shape
    gpp = _pick_tile(npg, 16, 1)

    def lf_spec(g):
        return pl.BlockSpec((None, C_HEADS, SUBLANES, PAGE_SIZE),
                            lambda s, t, pt: (layer, 0, jnp.right_shift(pt[s, t * gpp + g], 3), 0))

    return pl.pallas_call(
        functools.partial(_decbias_kernel, gpp=gpp),
        grid_spec=pltpu.PrefetchScalarGridSpec(
            num_scalar_prefetch=1,
            grid=(n_seq, npg // gpp),
            in_specs=[lf_spec(g) for g in range(gpp)]
            + [pl.BlockSpec((None, SUBLANES, LANES), lambda s, t, pt: (s, 0, 0))],
            out_specs=[
                pl.BlockSpec((None, SUBLANES, gpp * PAGE_SIZE), lambda s, t, pt: (s, 0, t)),
                pl.BlockSpec((None, SUBLANES, LANES), lambda s, t, pt: (s, 0, 0)),
            ],
            scratch_shapes=[pltpu.VMEM((SUBLANES, LANES), F32)],
        ),
        out_shape=[
            jax.ShapeDtypeStruct((n_seq, SUBLANES, npg * PAGE_SIZE), F32),
            jax.ShapeDtypeStruct((n_seq, SUBLANES, LANES), F32),
        ],
        compiler_params=_cparams("arbitrary", "arbitrary"),
        name="decode_bias",
    )(page_table, *([logf_hp] * gpp), lfn_t)


DEC_QROWS = 16


def _decode_kernel(pt_ref, qa_ref, *refs, gpp, dec_seq):
    del pt_ref
    k_refs = refs[:gpp]
    v_refs = refs[gpp:2 * gpp]
    ck_ref, ckn_ref, kn_ref, vn_ref, g_ref, o_ref, q_scr, m_scr, l_scr, acc_scr = refs[2 * gpp:]
    t = pl.program_id(1)
    nt = pl.num_programs(1)
    nq = dec_seq
    qp = DEC_QROWS
    nrow = C_HEADS * qp
    hd = C_HEAD_DIM
    row = _iota((nrow, LANES), 0)
    lane = _iota((nrow, LANES), 1)
    qi = jnp.bitwise_and(row, qp - 1)

    @pl.when(t == 0)
    def _():
        qa = qa_ref[...].astype(F32)
        zero = jnp.zeros((qp - nq, hd), F32)
        q_scr[...] = jnp.concatenate(
            [jnp.concatenate([qa[:, h * 2 * hd:h * 2 * hd + hd], zero], axis=0) for h in range(C_HEADS)],
            axis=0).astype(BF16)
        m_scr[...] = jnp.full_like(m_scr, -jnp.inf)
        l_scr[...] = jnp.zeros_like(l_scr)
        acc_scr[...] = jnp.zeros_like(acc_scr)

    def head_rows(get_row):
        return jnp.concatenate(
            [jnp.broadcast_to(get_row(h), (qp, get_row(h).shape[1])) for h in range(C_HEADS)], axis=0)

    ckn_rows = head_rows(lambda h: ckn_ref[h:h + 1, :])
    c_q = jnp.sum(jnp.where(lane == qi, ckn_rows, 0.0), axis=-1, keepdims=True)

    def attend(k_of, v_of, ck_rows, mask):
        s = jnp.concatenate(
            [lax.dot_general(q_scr[h * qp:(h + 1) * qp, :], k_of(h), (((1,), (1,)), ((), ())),
                             preferred_element_type=F32) for h in range(C_HEADS)], axis=0)
        s = s + (c_q - ck_rows)
        if mask is not None:
            s = jnp.where(mask, s, -jnp.inf)
        m_prev = m_scr[...]
        m_new = jnp.maximum(m_prev, jnp.max(s, axis=-1, keepdims=True))
        alpha = jnp.exp(m_prev - m_new)
        pr = jnp.exp(s - m_new)
        l_scr[...] = alpha * l_scr[...] + jnp.sum(pr, axis=-1, keepdims=True)
        prb = pr.astype(BF16)
        o = jnp.concatenate(
            [jnp.dot(prb[h * qp:(h + 1) * qp, :], v_of(h), preferred_element_type=F32) for h in range(C_HEADS)],
            axis=0)
        acc_scr[...] = alpha * acc_scr[...] + o
        m_scr[...] = m_new

    attend(lambda h: jnp.concatenate([k_refs[g][h] for g in range(gpp)], axis=0).astype(BF16),
           lambda h: jnp.concatenate([v_refs[g][h] for g in range(gpp)], axis=0).astype(BF16),
           head_rows(lambda h: ck_ref[h:h + 1, :]), None)

    @pl.when(t == nt - 1)
    def _():
        pad = jnp.zeros((PAGE_SIZE - nq, hd), F32)
        kn = kn_ref[...]
        vn = vn_ref[...]
        attend(lambda h: jnp.concatenate([kn[:, h * hd:(h + 1) * hd], pad], axis=0).astype(BF16),
               lambda h: jnp.concatenate([vn[:, h * hd:(h + 1) * hd], pad], axis=0).astype(BF16),
               ckn_rows, lane <= qi)
        out = acc_scr[...] / l_scr[...]
        gate = _silu(g_ref[...].astype(F32))
        for h in range(C_HEADS):
            cs = slice(h * hd, (h + 1) * hd)
            o_ref[:, cs] = (out[h * qp:h * qp + nq, :] * gate[:, cs]).astype(o_ref.dtype)


def _fox_decode(page_table, q_aug, cache_k_hm, cache_v_hm, ck_past, ck_new, k_new, v_new, p_all, *, layer, dec_seq):
    n_seq, npg = page_table.shape
    rows = q_aug.shape[0]
    nrow = C_HEADS * DEC_QROWS
    gpp = _pick_tile(npg, 8, 1)

    def page_spec(g):
        return pl.BlockSpec((None, None, C_HEADS, PAGE_SIZE, C_HEAD_DIM),
                            lambda s, t, pt: (layer, pt[s, t * gpp + g], 0, 0, 0))

    return pl.pallas_call(
        functools.partial(_decode_kernel, gpp=gpp, dec_seq=dec_seq),
        grid_spec=pltpu.PrefetchScalarGridSpec(
            num_scalar_prefetch=1,
            grid=(n_seq, npg // gpp),
            in_specs=[pl.BlockSpec((dec_seq, C_HEADS * 2 * C_HEAD_DIM), lambda s, t, pt: (s, 0))]
            + [page_spec(g) for g in range(gpp)]
            + [page_spec(g) for g in range(gpp)]
            + [
                pl.BlockSpec((None, SUBLANES, gpp * PAGE_SIZE), lambda s, t, pt: (s, 0, t)),
                pl.BlockSpec((None, SUBLANES, LANES), lambda s, t, pt: (s, 0, 0)),
                pl.BlockSpec((dec_seq, C_WIDTH), lambda s, t, pt: (s, 0)),
                pl.BlockSpec((dec_seq, C_WIDTH), lambda s, t, pt: (s, 0)),
                pl.BlockSpec((dec_seq, C_WIDTH), lambda s, t, pt: (s, OFF_CG // C_WIDTH)),
            ],
            out_specs=pl.BlockSpec((dec_seq, C_WIDTH), lambda s, t, pt: (s, 0)),
            scratch_shapes=[
                pltpu.VMEM((nrow, C_HEAD_DIM), BF16),
                pltpu.VMEM((nrow, 1), F32),
                pltpu.VMEM((nrow, 1), F32),
                pltpu.VMEM((nrow, C_HEAD_DIM), F32),
            ],
        ),
        out_shape=jax.ShapeDtypeStruct((rows, C_WIDTH), p_all.dtype),
        compiler_params=_cparams("arbitrary", "arbitrary"),
        name="fox_decode",
    )(page_table, q_aug, *([cache_k_hm] * gpp), *([cache_v_hm] * gpp), ck_past, ck_new, k_new, v_new, p_all)


def _merge_kernel(x_ref, a_ref, b_ref, c_ref, mg0_ref, mg1_ref, mg2_ref, bm_ref,
                  wa_ref, wb_ref, wc_ref, wo_ref, y_ref):
    d = D_MODEL

    def gated(o_ref, w_ref, mg_ref, k):
        gate = jax.nn.sigmoid(mg_ref[...].astype(F32) + bm_ref[:, k * d:(k + 1) * d])
        return gate * jnp.dot(o_ref[...].astype(BF16), w_ref[...], preferred_element_type=F32)

    merged = gated(a_ref, wa_ref, mg0_ref, 0) + gated(b_ref, wb_ref, mg1_ref, 1) + gated(c_ref, wc_ref, mg2_ref, 2)
    y_ref[...] = x_ref[...] + jnp.dot(merged.astype(BF16), wo_ref[...], preferred_element_type=F32)


def _merge(x2, a_o, b_o, c_o, p_all, b_merge, wa, wb, wc, wo):
    rows = x2.shape[0]
    tm = _pick_tile(rows, 256)
    d = D_MODEL
    bw = B_HEADS * B_DVP

    def const(shape):
        return pl.BlockSpec(shape, lambda i: (0, 0), pipeline_mode=pl.Buffered(1))

    return pl.pallas_call(
        _merge_kernel,
        grid=(rows // tm,),
        in_specs=[
            pl.BlockSpec((tm, d), lambda i: (i, 0)),
            pl.BlockSpec((tm, A_WIDTH), lambda i: (i, 0)),
            pl.BlockSpec((tm, bw), lambda i: (i, 0)),
            pl.BlockSpec((tm, C_WIDTH), lambda i: (i, 0)),
            pl.BlockSpec((tm, d), lambda i: (i, OFF_MG // d + 0)),
            pl.BlockSpec((tm, d), lambda i: (i, OFF_MG // d + 1)),
            pl.BlockSpec((tm, d), lambda i: (i, OFF_MG // d + 2)),
            const((1, N_BRANCH * d)),
            const((A_WIDTH, d)),
            const((bw, d)),
            const((C_WIDTH, d)),
            const((d, d)),
        ],
        out_specs=pl.BlockSpec((tm, d), lambda i: (i, 0)),
        out_shape=jax.ShapeDtypeStruct((rows, d), F32),
        compiler_params=_cparams("parallel"),
        name="merge_out",
    )(x2, a_o, b_o, c_o, p_all, p_all, p_all, b_merge.reshape(1, N_BRANCH * d), wa, wb, wc, wo)


def _pad_heads(x, n_heads, d, dp, axis):
    shp = x.shape
    x = x.reshape(shp[:axis] + (n_heads, d) + shp[axis + 1:])
    pad = [(0, 0)] * x.ndim
    pad[axis + 1] = (0, dp - d)
    x = jnp.pad(x, pad)
    return x.reshape(shp[:axis] + (n_heads * dp,) + shp[axis + 1:])


def _pack_plan():
    src = {}
    acc = 0
    for name, size in zip(("a_u", "a_g", "b_q", "b_k", "b_v", "b_g", "b_r", "c_q", "c_k", "c_v", "c_g", "c_f", "m_g"),
                          SPLIT_SIZES):
        src[name] = acc
        acc += size
    d_in = acc
    segs = [(OFF_MG, "m_g", 1, 6144, 6144), (OFF_CG, "c_g", 1, C_WIDTH, C_WIDTH), (OFF_CQ, "c_q", 1, C_WIDTH, C_WIDTH),
            (OFF_CK, "c_k", 1, C_WIDTH, C_WIDTH), (OFF_CV, "c_v", 1, C_WIDTH, C_WIDTH),
            (OFF_AU, "a_u", 1, A_WIDTH, A_WIDTH), (OFF_AG, "a_g", 1, A_WIDTH, A_WIDTH),
            (OFF_BQ, "b_q", B_HEADS, B_DK, B_DKP), (OFF_BK, "b_k", B_HEADS, B_DK, B_DKP),
            (OFF_BV, "b_v", B_HEADS, B_DV, B_DVP), (OFF_BG, "b_g", B_HEADS, B_DV, B_DVP),
            (OFF_SM, "b_r", 1, B_GATE_RANK, LANES), (OFF_SM + LANES, "c_f", 1, C_HEADS, LANES)]
    starts = [0] * (NP // LANES)
    nvalid = [0] * (NP // LANES)
    for off, name, heads, dreal, dpad in segs:
        for h in range(heads):
            for b in range(dpad // LANES):
                blk = (off + h * dpad) // LANES + b
                n = max(0, min(LANES, dreal - b * LANES))
                s = src[name] + h * dreal + b * LANES
                assert s + LANES <= d_in
                starts[blk], nvalid[blk] = (s if n else 0), n
    return starts, nvalid


def _packw_kernel(st_ref, nv_ref, w_ref, o_ref):
    del st_ref
    nv = nv_ref[pl.program_id(0)]
    live = _iota((LANES, D_MODEL), 0) < nv
    for l in range(o_ref.shape[0]):
        o_ref[l] = jnp.where(live, w_ref[:, l, :], 0.0).astype(BF16)


def _pack_w_in(w_in):
    depth = w_in.shape[0]
    w_t = jnp.transpose(w_in, (2, 0, 1))
    starts, nvalid = _pack_plan()
    return pl.pallas_call(
        _packw_kernel,
        grid_spec=pltpu.PrefetchScalarGridSpec(
            num_scalar_prefetch=2,
            grid=(NP // LANES,),
            in_specs=[pl.BlockSpec((pl.Element(LANES), pl.Element(depth), pl.Element(D_MODEL)),
                                   lambda j, st, nv: (st[j], 0, 0))],
            out_specs=pl.BlockSpec((depth, LANES, D_MODEL), lambda j, st, nv: (0, j, 0)),
        ),
        out_shape=jax.ShapeDtypeStruct((depth, NP, D_MODEL), BF16),
        compiler_params=_cparams("parallel"),
        name="pack_w_in",
    )(jnp.asarray(starts, jnp.int32), jnp.asarray(nvalid, jnp.int32), w_t)


def _layer_params(l, norm_g, pool_w, pool_scale, gla_w_a2, gla_b_a, gla_norm, fox_b_f, fox_q_norm,
                  fox_k_norm, w_branch_a, w_branch_b, w_branch_c, b_merge, w_out):
    wa2 = _pad_heads(gla_w_a2[l], B_HEADS, B_DK, B_DKP, 1)
    return dict(
        norm_g=norm_g[l],
        pool_w=pool_w[l],
        pool_scale=pool_scale[l],
        wa2=jnp.pad(wa2, ((0, LANES - B_GATE_RANK), (0, 0))),
        ba=_pad_heads(gla_b_a[l].reshape(1, -1), B_HEADS, B_DK, B_DKP, 1),
        gn=_pad_heads(gla_norm[l].reshape(1, -1), B_HEADS, B_DV, B_DVP, 1),
        bf=jnp.pad(fox_b_f[l].reshape(1, -1), ((0, 0), (0, LANES - C_HEADS))),
        qn=fox_q_norm[l],
        kn=fox_k_norm[l],
        wa=w_branch_a[l].astype(BF16),
        wb=_pad_heads(w_branch_b[l], B_HEADS, B_DV, B_DVP, 0).astype(BF16),
        wc=w_branch_c[l].astype(BF16),
        b_merge=b_merge[l],
        wo=w_out[l].astype(BF16),
    )


def _state_to_kernel(s):
    st = jnp.swapaxes(s, -1, -2)
    return jnp.pad(st, ((0, 0), (0, 0), (0, B_DVP - B_DV), (0, B_DKP - B_DK)))


def _state_from_kernel(st):
    return jnp.swapaxes(st[:, :, :B_DV, :B_DK], -1, -2)


def kernel(x_prompt, x_sample, cache_k, cache_v, cache_logf, state_gla, state_pool, page_table, norm_g, w_in, pool_w, pool_scale, gla_w_a2, gla_b_a, gla_norm, fox_b_f, fox_q_norm, fox_k_norm, w_branch_a, w_branch_b, w_branch_c, b_merge, w_out):
    depth = w_in.shape[0]
    bp, seq, d = x_prompt.shape
    n_seq, dec_seq, _ = x_sample.shape
    assert d == D_MODEL and dec_seq == SUBLANES and seq % LANES == 0
    yp = x_prompt.reshape(bp * seq, d)
    ys = x_sample.reshape(n_seq * dec_seq, d)
    cache_k_hm = jnp.transpose(cache_k, (0, 1, 3, 2, 4))
    cache_v_hm = jnp.transpose(cache_v, (0, 1, 3, 2, 4))
    logf_hp = jnp.transpose(cache_logf, (0, 3, 1, 2))
    zero_state = jnp.zeros((bp, B_HEADS, B_DVP, B_DKP), F32)
    w_packed = _pack_w_in(w_in)
    outs = {k: [] for k in ("kp", "vp", "lp", "gp", "pp", "ks", "vs", "ls", "gs", "ps")}
    for l in range(depth):
        prm = _layer_params(l, norm_g, pool_w, pool_scale, gla_w_a2, gla_b_a, gla_norm, fox_b_f,
                            fox_q_norm, fox_k_norm, w_branch_a, w_branch_b, w_branch_c, b_merge, w_out)

        p_all, f_all = _inproj(yp, prm["norm_g"], w_packed, l, BF16)
        a_o = _pool_branch(p_all, prm["pool_w"], prm["pool_scale"], seq_len=seq)
        b_o, s_fin = _gla_branch(p_all, f_all, prm["wa2"], prm["ba"], prm["gn"], zero_state,
                                 n_seq=bp, seq_len=seq, decode=False)
        q_aug, k_aug, k_n, v_n, lf = _fox_prep(p_all, f_all, prm["qn"], prm["kn"], prm["bf"], seq_len=seq)
        c_o = _fox_flash(q_aug, k_aug, p_all, n_seq=bp, seq_len=seq)
        yp = _merge(yp, a_o, b_o, c_o, p_all, prm["b_merge"], prm["wa"], prm["wb"], prm["wc"], prm["wo"])
        outs["kp"].append(k_n.reshape(bp, seq, C_HEADS, C_HEAD_DIM))
        outs["vp"].append(v_n.reshape(bp, seq, C_HEADS, C_HEAD_DIM))
        outs["lp"].append(lf[:, :C_HEADS].reshape(bp, seq, C_HEADS))
        outs["gp"].append(_state_from_kernel(s_fin))
        a_u = p_all[:, OFF_AU:OFF_AU + A_WIDTH].reshape(bp, seq, A_WIDTH)
        outs["pp"].append(a_u[:, seq - A_BUF:].astype(F32))

        p_s, f_s = _inproj(ys, prm["norm_g"], w_packed, l, F32)
        pool_state = jnp.pad(state_pool[l], ((0, 0), (1, 0), (0, 0)))
        a_s = _pool_branch(p_s, prm["pool_w"], prm["pool_scale"], seq_len=dec_seq, state=pool_state)
        b_s, s_fin_s = _gla_branch(p_s, f_s, prm["wa2"], prm["ba"], prm["gn"], _state_to_kernel(state_gla[l]),
                                   n_seq=n_seq, seq_len=dec_seq, decode=True)
        q_aug_s, _, k_ns, v_ns, lf_s = _fox_prep(p_s, f_s, prm["qn"], prm["kn"], prm["bf"], seq_len=dec_seq)
        lfn_t = jnp.swapaxes(lf_s.reshape(n_seq, dec_seq, LANES)[:, :, :SUBLANES], 1, 2)
        lfn_t = jnp.pad(lfn_t, ((0, 0), (0, 0), (0, LANES - dec_seq)))
        ck_past, ck_new = _decode_bias(page_table, logf_hp, lfn_t, layer=l)
        c_s = _fox_decode(page_table, q_aug_s, cache_k_hm, cache_v_hm, ck_past, ck_new, k_ns, v_ns, p_s,
                          layer=l, dec_seq=dec_seq)
        ys = _merge(ys, a_s, b_s, c_s, p_s, prm["b_merge"], prm["wa"], prm["wb"], prm["wc"], prm["wo"])
        outs["ks"].append(k_ns.reshape(n_seq, dec_seq, C_HEADS, C_HEAD_DIM))
        outs["vs"].append(v_ns.reshape(n_seq, dec_seq, C_HEADS, C_HEAD_DIM))
        outs["ls"].append(lf_s[:, :C_HEADS].reshape(n_seq, dec_seq, C_HEADS))
        outs["gs"].append(_state_from_kernel(s_fin_s))
        a_us = p_s[:, OFF_AU:OFF_AU + A_WIDTH].reshape(n_seq, dec_seq, A_WIDTH).astype(F32)
        outs["ps"].append(jnp.concatenate([state_pool[l], a_us], axis=1)[:, -A_BUF:])

    st = lambda k: jnp.stack(outs[k])
    return (yp.reshape(bp, seq, d), ys.reshape(n_seq, dec_seq, d),
            st("kp"), st("vp"), st("lp"), st("gp"), st("pp"),
            st("ks"), st("vs"), st("ls"), st("gs"), st("ps"))
```

```python
import functools
import math

import jax
import jax.numpy as jnp
from jax import lax
from jax.experimental import pallas as pl
from jax.experimental.pallas import tpu as pltpu

F32 = jnp.float32
BF16 = jnp.bfloat16
HIGHEST = lax.Precision.HIGHEST

D_MODEL = 2048
A_WIDTH = 512
A_WINDOWS = (2, 4, 8, 16)
A_GROUPS = 4
A_GROUP_DIM = 128
A_BUF = 15
B_HEADS = 4
B_DK = 96
B_DV = 192
B_GATE_RANK = 16
B_GATE_TEMP = 16.0
B_CHUNK = 32
C_HEADS = 6
C_HEAD_DIM = 128
C_WIDTH = 768
N_BRANCH = 3
EPS = 1e-6
PAGE_SIZE = 128
SPLIT_SIZES = (512, 512, 384, 384, 768, 768, 16, 768, 768, 768, 768, 6, 6144)

LANES = 128
SUBLANES = 8
VMEM_LIMIT_BYTES = 56 * 1024 * 1024

B_DKP = 128
B_DVP = 256

OFF_MG = 0
OFF_CG = 6144
OFF_CQ = OFF_CG + C_WIDTH
OFF_CK = OFF_CQ + C_WIDTH
OFF_CV = OFF_CK + C_WIDTH
OFF_AU = OFF_CV + C_WIDTH
OFF_AG = OFF_AU + A_WIDTH
OFF_BQ = OFF_AG + A_WIDTH
OFF_BK = OFF_BQ + B_HEADS * B_DKP
OFF_BV = OFF_BK + B_HEADS * B_DKP
OFF_BG = OFF_BV + B_HEADS * B_DVP
OFF_SM = OFF_BG + B_HEADS * B_DVP
SM_WIDTH = 2 * LANES
PROJ_TN = 1536
NP = 13824
assert OFF_SM + SM_WIDTH <= NP and NP % PROJ_TN == 0
SM_TILE = OFF_SM // PROJ_TN
SM_LOCAL = OFF_SM - SM_TILE * PROJ_TN
assert SM_LOCAL + SM_WIDTH <= PROJ_TN


def _pick_tile(n, pref, align=SUBLANES):
    if n <= pref:
        return n
    t = (pref // align) * align
    while t > align and n % t:
        t -= align
    assert n % t == 0, (n, pref)
    return t


def _cparams(*sem):
    return pltpu.CompilerParams(dimension_semantics=sem, vmem_limit_bytes=VMEM_LIMIT_BYTES)


def _log_sigmoid(x):
    return jnp.minimum(x, 0.0) - jnp.log1p(jnp.exp(-jnp.abs(x)))


def _silu(x):
    return x * jax.nn.sigmoid(x)


def _iota(shape, dim):
    return lax.broadcasted_iota(jnp.int32, shape, dim)


def _inproj_kernel(x_ref, g_ref, w_ref, p_ref, f_ref, h_scr, *, nj, rchunk):
    j = pl.program_id(1)

    @pl.when(j == 0)
    def _():
        def body(c, carry):
            r0 = pl.multiple_of(c * rchunk, rchunk)
            x = x_ref[pl.ds(r0, rchunk), :]
            r = lax.rsqrt(jnp.mean(x * x, axis=-1, keepdims=True) + EPS)
            h_scr[pl.ds(r0, rchunk), :] = ((x * r) * g_ref[...]).astype(BF16)
            return carry

        lax.fori_loop(0, x_ref.shape[0] // rchunk, body, 0)

    acc = lax.dot_general(h_scr[...], w_ref[...], (((1,), (1,)), ((), ())), preferred_element_type=F32)
    p_ref[...] = acc.astype(p_ref.dtype)

    @pl.when(j == nj - 1)
    def _():
        f_ref[...] = acc[:, SM_LOCAL:SM_LOCAL + SM_WIDTH]


def _inproj(x2, norm_g, w_packed, layer, act_dt):
    rows = x2.shape[0]
    tm = _pick_tile(rows, 1024)
    nj = NP // PROJ_TN
    assert SM_TILE == nj - 1
    rchunk = _pick_tile(tm, 128)
    return pl.pallas_call(
        functools.partial(_inproj_kernel, nj=nj, rchunk=rchunk),
        grid=(rows // tm, nj),
        in_specs=[
            pl.BlockSpec((tm, D_MODEL), lambda i, j: (i, 0)),
            pl.BlockSpec((1, D_MODEL), lambda i, j: (0, 0)),
            pl.BlockSpec((None, PROJ_TN, D_MODEL), lambda i, j: (layer, j, 0)),
        ],
        out_specs=[
            pl.BlockSpec((tm, PROJ_TN), lambda i, j: (i, j)),
            pl.BlockSpec((tm, SM_WIDTH), lambda i, j: (i, 0)),
        ],
        out_shape=[
            jax.ShapeDtypeStruct((rows, NP), act_dt),
            jax.ShapeDtypeStruct((rows, SM_WIDTH), F32),
        ],
        scratch_shapes=[pltpu.VMEM((tm, D_MODEL), BF16)],
        compiler_params=_cparams("parallel", "arbitrary"),
        name="inproj",
    )(x2, norm_g.reshape(1, D_MODEL), w_packed)


def _pool_kernel(u_ref, g_ref, halo_ref, pw_ref, sc_ref, o_ref, *, tiles_per_seq, decode):
    i = pl.program_id(0)
    tr = u_ref.shape[0]
    hr = halo_ref.shape[0]
    mm_dt = F32 if decode else BF16
    u = u_ref[...].astype(F32)
    halo = halo_ref[...].astype(F32)
    if not decode:
        halo = jnp.where(i % tiles_per_seq == 0, 0.0, halo)
    ext = jnp.concatenate([halo, u], axis=0).astype(mm_dt)
    r = _iota((tr, hr + tr), 0)
    c = _iota((tr, hr + tr), 1) - hr
    rowpos = _iota((tr, 1), 0)
    pos = ((i % tiles_per_seq) * tr + rowpos).astype(F32)
    outs = []
    for g, w in enumerate(A_WINDOWS):
        lo, hi = g * A_GROUP_DIM, (g + 1) * A_GROUP_DIM
        band = ((c >= r - (w - 1)) & (c <= r)).astype(mm_dt)
        if decode:
            win_sum = jnp.dot(band, ext[:, lo:hi], preferred_element_type=F32, precision=HIGHEST)
            count = float(w)
        else:
            win_sum = jnp.dot(band, ext[:, lo:hi], preferred_element_type=F32)
            count = jnp.minimum(pos + 1.0, float(w))
        pooled = win_sum / count - u[:, lo:hi]
        outs.append(jnp.dot(pooled.astype(mm_dt), pw_ref[g].astype(mm_dt), preferred_element_type=F32))
    mixed = jnp.concatenate(outs, axis=1) * sc_ref[...]
    o_ref[...] = (mixed * _silu(g_ref[...].astype(F32))).astype(o_ref.dtype)


def _pool_branch(p_all, pool_w, pool_scale, *, seq_len, state=None):
    rows = p_all.shape[0]
    decode = state is not None
    wdt = F32 if decode else BF16
    if decode:
        tr, tps = seq_len, 1
        halo_spec = pl.BlockSpec((None, 16, A_WIDTH), lambda i: (i, 0, 0))
        halo_arr = state
    else:
        tr = _pick_tile(seq_len, 256)
        tps = seq_len // tr
        hb = LANES
        assert tr % hb == 0
        halo_spec = pl.BlockSpec((hb, A_WIDTH), lambda i: (jnp.maximum(i * (tr // hb) - 1, 0), OFF_AU // A_WIDTH))
        halo_arr = p_all
    return pl.pallas_call(
        functools.partial(_pool_kernel, tiles_per_seq=tps, decode=decode),
        grid=(rows // tr,),
        in_specs=[
            pl.BlockSpec((tr, A_WIDTH), lambda i: (i, OFF_AU // A_WIDTH)),
            pl.BlockSpec((tr, A_WIDTH), lambda i: (i, OFF_AG // A_WIDTH)),
            halo_spec,
            pl.BlockSpec((A_GROUPS, A_GROUP_DIM, A_GROUP_DIM), lambda i: (0, 0, 0)),
            pl.BlockSpec((1, A_WIDTH), lambda i: (0, 0)),
        ],
        out_specs=pl.BlockSpec((tr, A_WIDTH), lambda i: (i, 0)),
        out_shape=jax.ShapeDtypeStruct((rows, A_WIDTH), p_all.dtype),
        compiler_params=_cparams("arbitrary"),
        name="pool_decode" if decode else "pool_prompt",
    )(p_all, p_all, halo_arr, pool_w.astype(wdt), pool_scale.reshape(1, A_WIDTH))


def _gla_kernel(q_ref, k_ref, v_ref, g_ref, r_ref, wa_ref, ba_ref, gn_ref, s0_ref,
                o_ref, so_ref, s_scr, *, chunk, decode):
    tb = pl.program_id(1)
    nb = pl.num_programs(1)
    tbk = q_ref.shape[0]
    mm_dt = F32 if decode else BF16

    @pl.when(tb == 0)
    def _():
        s_scr[...] = s0_ref[...]

    logit = jnp.dot(r_ref[...], wa_ref[...], preferred_element_type=F32, precision=HIGHEST) + ba_ref[...]
    log_a = _log_sigmoid(logit) / B_GATE_TEMP
    tri = (_iota((chunk, chunk), 0) >= _iota((chunk, chunk), 1))
    tri_f = tri.astype(F32)
    scale = B_DK ** -0.5
    for c in range(tbk // chunk):
        rs = slice(c * chunk, (c + 1) * chunk)
        b = jnp.dot(tri_f, log_a[rs], preferred_element_type=F32, precision=HIGHEST)
        b_last = b[chunk - 1:chunk, :]
        qc = q_ref[rs, :].astype(F32)
        kc = k_ref[rs, :].astype(F32)
        q_t = (qc * scale) * jnp.exp(b)
        k_t = kc * jnp.exp(-b)
        k_end = kc * jnp.exp(b_last - b)
        decay = jnp.exp(b_last)
        for h in range(B_HEADS):
            ks = slice(h * B_DKP, (h + 1) * B_DKP)
            vs = slice(h * B_DVP, (h + 1) * B_DVP)
            qh = q_t[:, ks].astype(mm_dt)
            att = lax.dot_general(qh, k_t[:, ks].astype(mm_dt), (((1,), (1,)), ((), ())),
                                  preferred_element_type=F32)
            att = jnp.where(tri, att, 0.0)
            vh = v_ref[rs, vs].astype(mm_dt)
            st = s_scr[h]
            o = jnp.dot(att.astype(mm_dt), vh, preferred_element_type=F32)
            o = o + lax.dot_general(qh, st.astype(mm_dt), (((1,), (1,)), ((), ())),
                                    preferred_element_type=F32)
            kv_t = lax.dot_general(vh, k_end[:, ks].astype(mm_dt), (((0,), (0,)), ((), ())),
                                   preferred_element_type=F32)
            s_scr[h] = st * decay[:, ks] + kv_t
            r = lax.rsqrt(jnp.sum(o * o, axis=-1, keepdims=True) * (1.0 / B_DV) + EPS)
            on = (o * r) * gn_ref[:, vs]
            o_ref[rs, vs] = (on * _silu(g_ref[rs, vs].astype(F32))).astype(o_ref.dtype)

    @pl.when(tb == nb - 1)
    def _():
        so_ref[...] = s_scr[...]


def _gla_branch(p_all, f_all, wa_p, ba_p, gn_p, s0_t, *, n_seq, seq_len, decode):
    rows = p_all.shape[0]
    chunk = math.gcd(seq_len, B_CHUNK)
    tbk = _pick_tile(seq_len, 256)
    nb = seq_len // tbk
    kw = B_HEADS * B_DKP
    vw = B_HEADS * B_DVP
    return pl.pallas_call(
        functools.partial(_gla_kernel, chunk=chunk, decode=decode),
        grid=(n_seq, nb),
        in_specs=[
            pl.BlockSpec((tbk, kw), lambda s, t: (s * nb + t, OFF_BQ // kw)),
            pl.BlockSpec((tbk, kw), lambda s, t: (s * nb + t, OFF_BK // kw)),
            pl.BlockSpec((tbk, vw), lambda s, t: (s * nb + t, OFF_BV // vw)),
            pl.BlockSpec((tbk, vw), lambda s, t: (s * nb + t, OFF_BG // vw)),
            pl.BlockSpec((tbk, LANES), lambda s, t: (s * nb + t, 0)),
            pl.BlockSpec((LANES, kw), lambda s, t: (0, 0)),
            pl.BlockSpec((1, kw), lambda s, t: (0, 0)),
            pl.BlockSpec((1, vw), lambda s, t: (0, 0)),
            pl.BlockSpec((None, B_HEADS, B_DVP, B_DKP), lambda s, t: (s, 0, 0, 0)),
        ],
        out_specs=[
            pl.BlockSpec((tbk, vw), lambda s, t: (s * nb + t, 0)),
            pl.BlockSpec((None, B_HEADS, B_DVP, B_DKP), lambda s, t: (s, 0, 0, 0)),
        ],
        out_shape=[
            jax.ShapeDtypeStruct((rows, vw), p_all.dtype),
            jax.ShapeDtypeStruct((n_seq, B_HEADS, B_DVP, B_DKP), F32),
        ],
        scratch_shapes=[pltpu.VMEM((B_HEADS, B_DVP, B_DKP), F32)],
        compiler_params=_cparams("arbitrary", "arbitrary"),
        name="gla_decode" if decode else "gla_prompt",
    )(p_all, p_all, p_all, p_all, f_all, wa_p, ba_p, gn_p, s0_t)


def _foxprep_kernel(cq_ref, ck_ref, cv_ref, cf_ref, qn_ref, kn_ref, bf_ref,
                    qa_ref, ka_ref, ko_ref, vo_ref, lf_ref, carry_scr, *, tiles_per_seq):
    i = pl.program_id(0)
    tr = cq_ref.shape[0]
    hd = C_HEAD_DIM
    lane = _iota((tr, LANES), 1)

    lf = jnp.where(lane < C_HEADS, _log_sigmoid(cf_ref[...] + bf_ref[...]), 0.0)
    lf_ref[...] = lf

    @pl.when(i % tiles_per_seq == 0)
    def _():
        carry_scr[...] = jnp.zeros_like(carry_scr)

    tri = (_iota((tr, tr), 0) >= _iota((tr, tr), 1)).astype(F32)
    cum = jnp.dot(tri, lf, preferred_element_type=F32, precision=HIGHEST) + carry_scr[...]
    carry_scr[...] = cum[tr - 1:tr, :]

    def qk_norm(x, g_ref):
        r = lax.rsqrt(jnp.mean(x * x, axis=-1, keepdims=True) + EPS)
        return (x * r) * g_ref[...]

    one = jnp.ones((tr, LANES), F32)
    zero = jnp.zeros((tr, LANES), F32)
    for h in range(C_HEADS):
        cs = slice(h * hd, (h + 1) * hd)
        ch = jnp.sum(jnp.where(lane == h, cum, 0.0), axis=-1, keepdims=True)
        hi = ch.astype(BF16).astype(F32)
        r1 = ch - hi
        mid = r1.astype(BF16).astype(F32)
        lo = r1 - mid
        qn = qk_norm(cq_ref[:, cs].astype(F32), qn_ref)
        kn = qk_norm(ck_ref[:, cs].astype(F32), kn_ref)
        aux_q = jnp.where(lane == 0, hi, jnp.where(lane == 1, mid, jnp.where(lane == 2, lo,
                          jnp.where(lane < 6, one, zero))))
        aux_k = jnp.where(lane < 3, one, jnp.where(lane == 3, -hi, jnp.where(lane == 4, -mid,
                          jnp.where(lane == 5, -lo, zero))))
        a0 = 2 * h * hd
        qa_ref[:, a0:a0 + hd] = (qn * (hd ** -0.5)).astype(qa_ref.dtype)
        qa_ref[:, a0 + hd:a0 + 2 * hd] = aux_q.astype(qa_ref.dtype)
        ka_ref[:, a0:a0 + hd] = kn.astype(ka_ref.dtype)
        ka_ref[:, a0 + hd:a0 + 2 * hd] = aux_k.astype(ka_ref.dtype)
        ko_ref[h] = kn
        vo_ref[h] = cv_ref[:, cs].astype(F32)


def _fox_prep(p_all, f_all, fox_q_norm, fox_k_norm, bf_p, *, seq_len):
    rows = p_all.shape[0]
    tr = _pick_tile(seq_len, 256)
    tps = seq_len // tr
    hd = C_HEAD_DIM
    cw = C_WIDTH
    aw = C_HEADS * 2 * hd
    return pl.pallas_call(
        functools.partial(_foxprep_kernel, tiles_per_seq=tps),
        grid=(rows // tr,),
        in_specs=[
            pl.BlockSpec((tr, cw), lambda i: (i, OFF_CQ // cw)),
            pl.BlockSpec((tr, cw), lambda i: (i, OFF_CK // cw)),
            pl.BlockSpec((tr, cw), lambda i: (i, OFF_CV // cw)),
            pl.BlockSpec((tr, LANES), lambda i: (i, 1)),
            pl.BlockSpec((1, hd), lambda i: (0, 0)),
            pl.BlockSpec((1, hd), lambda i: (0, 0)),
            pl.BlockSpec((1, LANES), lambda i: (0, 0)),
        ],
        out_specs=[
            pl.BlockSpec((tr, aw), lambda i: (i, 0)),
            pl.BlockSpec((tr, aw), lambda i: (i, 0)),
            pl.BlockSpec((None, C_HEADS, tr, hd), lambda i: (i // tps, 0, i % tps, 0)),
            pl.BlockSpec((None, C_HEADS, tr, hd), lambda i: (i // tps, 0, i % tps, 0)),
            pl.BlockSpec((tr, LANES), lambda i: (i, 0)),
        ],
        out_shape=[
            jax.ShapeDtypeStruct((rows, aw), p_all.dtype),
            jax.ShapeDtypeStruct((rows, aw), p_all.dtype),
            jax.ShapeDtypeStruct((rows // seq_len, C_HEADS, seq_len, hd), F32),
            jax.ShapeDtypeStruct((rows // seq_len, C_HEADS, seq_len, hd), F32),
            jax.ShapeDtypeStruct((rows, LANES), F32),
        ],
        scratch_shapes=[pltpu.VMEM((1, LANES), F32)],
        compiler_params=_cparams("arbitrary"),
        name="fox_prep",
    )(p_all, p_all, p_all, f_all, fox_q_norm.reshape(1, hd), fox_k_norm.reshape(1, hd), bf_p)


def _flash_kernel(q_ref, k_ref, v_ref, g_ref, o_ref, m_scr, l_scr, acc_scr, *, rs):
    i = pl.program_id(2)
    t = q_ref.shape[0]
    hd = C_HEAD_DIM
    m_scr[...] = jnp.full_like(m_scr, -jnp.inf)
    l_scr[...] = jnp.zeros_like(l_scr)
    acc_scr[...] = jnp.zeros_like(acc_scr)
    ones = jnp.ones((t, LANES), BF16)

    def step(j, diagonal):
        k0 = pl.multiple_of(j * t, t)
        kb = k_ref[pl.ds(k0, t), :]
        vb = jnp.concatenate([v_ref[pl.ds(k0, t), :], ones], axis=1)
        for r in range(t // rs):
            rows = slice(r * rs, (r + 1) * rs)
            nk = (r + 1) * rs if diagonal else t
            s = lax.dot_general(q_ref[rows, :], kb[:nk], (((1,), (1,)), ((), ())), preferred_element_type=F32)
            if diagonal:
                s = jnp.where(_iota((rs, nk), 0) + r * rs >= _iota((rs, nk), 1), s, -jnp.inf)
            m_prev = m_scr[rows, :]
            m_new = jnp.maximum(m_prev, jnp.max(s, axis=-1, keepdims=True))
            alpha = jnp.exp(m_prev - m_new)
            p = jnp.exp(s - pltpu.repeat(m_new, nk // LANES, axis=1))
            pv = jnp.dot(p.astype(BF16), vb[:nk], preferred_element_type=F32)
            l_scr[rows, :] = alpha * l_scr[rows, :] + pv[:, hd:]
            acc_scr[rows, :] = alpha * acc_scr[rows, :] + pv[:, :hd]
            m_scr[rows, :] = m_new

    def pair(jj, carry):
        step(2 * jj, False)
        step(2 * jj + 1, False)
        return carry

    lax.fori_loop(0, i // 2, pair, 0)

    @pl.when(i % 2 == 1)
    def _():
        step(i - 1, False)

    step(i, True)
    out = acc_scr[...] / l_scr[...]
    o_ref[...] = (out * _silu(g_ref[...].astype(F32))).astype(BF16)


def _fox_flash(q_aug, k_aug, p_all, *, n_seq, seq_len):
    rows = q_aug.shape[0]
    t = _pick_tile(seq_len, 1024, LANES)
    rs = _pick_tile(t, 256, LANES)
    nt = seq_len // t
    hd = C_HEAD_DIM
    aw = 2 * hd
    return pl.pallas_call(
        functools.partial(_flash_kernel, rs=rs),
        grid=(n_seq, C_HEADS, nt),
        in_specs=[
            pl.BlockSpec((t, aw), lambda b, h, i: (b * nt + i, h)),
            pl.BlockSpec((seq_len, aw), lambda b, h, i: (b, h)),
            pl.BlockSpec((seq_len, hd), lambda b, h, i: (b, OFF_CV // hd + h)),
            pl.BlockSpec((t, hd), lambda b, h, i: (b * nt + i, OFF_CG // hd + h)),
        ],
        out_specs=pl.BlockSpec((t, hd), lambda b, h, i: (b * nt + i, h)),
        out_shape=jax.ShapeDtypeStruct((rows, C_WIDTH), BF16),
        scratch_shapes=[pltpu.VMEM((t, LANES), F32), pltpu.VMEM((t, LANES), F32), pltpu.VMEM((t, hd), F32)],
        compiler_params=_cparams("parallel", "parallel", "arbitrary"),
        name="fox_flash",
    )(q_aug, k_aug, p_all, p_all)


def _decbias_kernel(pt_ref, lf_ref, lfn_ref, ck_ref, ckn_ref, g_scr):
    s = pl.program_id(0)
    npg = g_scr.shape[1]

    def gather(p, carry):
        page = pt_ref[s, p]
        for h in range(C_HEADS):
            g_scr[h, pl.ds(p, 1), :] = lf_ref[h, pl.ds(page, 1), :]
        return carry

    lax.fori_loop(0, npg, gather, 0)
    upper = (_iota((LANES, LANES), 0) <= _iota((LANES, LANES), 1)).astype(F32)
    earlier = (_iota((npg, npg), 1) < _iota((npg, npg), 0)).astype(F32)
    lfn_cum = jnp.dot(lfn_ref[...], upper, preferred_element_type=F32, precision=HIGHEST)
    ckn_ref[...] = jnp.zeros_like(ckn_ref)
    for h in range(C_HEADS):
        cum = jnp.dot(g_scr[h], upper, preferred_element_type=F32, precision=HIGHEST)
        tot = jnp.broadcast_to(cum[:, LANES - 1:LANES], cum.shape)
        before = jnp.dot(earlier, tot, preferred_element_type=F32, precision=HIGHEST)
        ck_ref[h] = cum + before
        ckn_ref[h:h + 1, :] = lfn_cum[h:h + 1, :] + (before[npg - 1:npg, :] + tot[npg - 1:npg, :])


def _decode_bias(page_table, logf_hp, lfn_t, *, layer):
    n_seq, npg = page_table.shape
    n_pool = logf_hp.shape[2]
    return pl.pallas_call(
        _decbias_kernel,
        grid_spec=pltpu.PrefetchScalarGridSpec(
            num_scalar_prefetch=1,
            grid=(n_seq,),
            in_specs=[
                pl.BlockSpec((None, C_HEADS, n_pool, PAGE_SIZE), lambda s, pt: (layer, 0, 0, 0)),
                pl.BlockSpec((None, SUBLANES, LANES), lambda s, pt: (s, 0, 0)),
            ],
            out_specs=[
                pl.BlockSpec((None, C_HEADS, npg, PAGE_SIZE), lambda s, pt: (s, 0, 0, 0)),
                pl.BlockSpec((None, SUBLANES, LANES), lambda s, pt: (s, 0, 0)),
            ],
            scratch_shapes=[pltpu.VMEM((C_HEADS, npg, PAGE_SIZE), F32)],
        ),
        out_shape=[
            jax.ShapeDtypeStruct((n_seq, C_HEADS, npg, PAGE_SIZE), F32),
            jax.ShapeDtypeStruct((n_seq, SUBLANES, LANES), F32),
        ],
        compiler_params=_cparams("arbitrary"),
        name="decode_bias",
    )(page_table, logf_hp, lfn_t)


DEC_QROWS = 16


def _decode_kernel(pt_ref, qa_ref, *refs, gpp, dec_seq):
    del pt_ref
    k_refs = refs[:gpp]
    v_refs = refs[gpp:2 * gpp]
    ck_ref, ckn_ref, kn_ref, vn_ref, g_ref, o_ref, q_scr, m_scr, l_scr, acc_scr = refs[2 * gpp:]
    t = pl.program_id(1)
    nt = pl.num_programs(1)
    nq = dec_seq
    qp = DEC_QROWS
    nrow = C_HEADS * qp
    hd = C_HEAD_DIM
    row = _iota((nrow, LANES), 0)
    lane = _iota((nrow, LANES), 1)
    qi = jnp.bitwise_and(row, qp - 1)

    @pl.when(t == 0)
    def _():
        qa = qa_ref[...].astype(F32)
        zero = jnp.zeros((qp - nq, hd), F32)
        q_scr[...] = jnp.concatenate(
            [jnp.concatenate([qa[:, h * 2 * hd:h * 2 * hd + hd], zero], axis=0) for h in range(C_HEADS)],
            axis=0).astype(BF16)
        m_scr[...] = jnp.full_like(m_scr, -jnp.inf)
        l_scr[...] = jnp.zeros_like(l_scr)
        acc_scr[...] = jnp.zeros_like(acc_scr)

    def head_rows(get_row):
        return jnp.concatenate(
            [jnp.broadcast_to(get_row(h), (qp, get_row(h).shape[1])) for h in range(C_HEADS)], axis=0)

    ckn_rows = head_rows(lambda h: ckn_ref[h:h + 1, :])
    c_q = jnp.sum(jnp.where(lane == qi, ckn_rows, 0.0), axis=-1, keepdims=True)

    def attend(k_of, v_of, ck_rows, mask):
        s = jnp.concatenate(
            [lax.dot_general(q_scr[h * qp:(h + 1) * qp, :], k_of(h), (((1,), (1,)), ((), ())),
                             preferred_element_type=F32) for h in range(C_HEADS)], axis=0)
        s = s + (c_q - ck_rows)
        if mask is not None:
            s = jnp.where(mask, s, -jnp.inf)
        m_prev = m_scr[...]
        m_new = jnp.maximum(m_prev, jnp.max(s, axis=-1, keepdims=True))
        alpha = jnp.exp(m_prev - m_new)
        pr = jnp.exp(s - m_new)
        l_scr[...] = alpha * l_scr[...] + jnp.sum(pr, axis=-1, keepdims=True)
        prb = pr.astype(BF16)
        o = jnp.concatenate(
            [jnp.dot(prb[h * qp:(h + 1) * qp, :], v_of(h), preferred_element_type=F32) for h in range(C_HEADS)],
            axis=0)
        acc_scr[...] = alpha * acc_scr[...] + o
        m_scr[...] = m_new

    attend(lambda h: jnp.concatenate([k_refs[g][h] for g in range(gpp)], axis=0).astype(BF16),
           lambda h: jnp.concatenate([v_refs[g][h] for g in range(gpp)], axis=0).astype(BF16),
           head_rows(lambda h: jnp.concatenate([ck_ref[h, g:g + 1, :] for g in range(gpp)], axis=1)), None)

    @pl.when(t == nt - 1)
    def _():
        pad = jnp.zeros((PAGE_SIZE - nq, hd), F32)
        attend(lambda h: jnp.concatenate([kn_ref[h], pad], axis=0).astype(BF16),
               lambda h: jnp.concatenate([vn_ref[h], pad], axis=0).astype(BF16),
               ckn_rows, lane <= qi)
        out = acc_scr[...] / l_scr[...]
        gate = _silu(g_ref[...].astype(F32))
        for h in range(C_HEADS):
            cs = slice(h * hd, (h + 1) * hd)
            o_ref[:, cs] = (out[h * qp:h * qp + nq, :] * gate[:, cs]).astype(o_ref.dtype)


def _fox_decode(page_table, q_aug, cache_k_hm, cache_v_hm, ck_past, ck_new, k_new, v_new, p_all, *, layer, dec_seq):
    n_seq, npg = page_table.shape
    rows = q_aug.shape[0]
    nrow = C_HEADS * DEC_QROWS
    gpp = _pick_tile(npg, 8, 1)

    def page_spec(g):
        return pl.BlockSpec((None, None, C_HEADS, PAGE_SIZE, C_HEAD_DIM),
                            lambda s, t, pt: (layer, pt[s, t * gpp + g], 0, 0, 0))

    return pl.pallas_call(
        functools.partial(_decode_kernel, gpp=gpp, dec_seq=dec_seq),
        grid_spec=pltpu.PrefetchScalarGridSpec(
            num_scalar_prefetch=1,
            grid=(n_seq, npg // gpp),
            in_specs=[pl.BlockSpec((dec_seq, C_HEADS * 2 * C_HEAD_DIM), lambda s, t, pt: (s, 0))]
            + [page_spec(g) for g in range(gpp)]
            + [page_spec(g) for g in range(gpp)]
            + [
                pl.BlockSpec((None, C_HEADS, gpp, PAGE_SIZE), lambda s, t, pt: (s, 0, t, 0)),
                pl.BlockSpec((None, SUBLANES, LANES), lambda s, t, pt: (s, 0, 0)),
                pl.BlockSpec((None, C_HEADS, dec_seq, C_HEAD_DIM), lambda s, t, pt: (s, 0, 0, 0)),
                pl.BlockSpec((None, C_HEADS, dec_seq, C_HEAD_DIM), lambda s, t, pt: (s, 0, 0, 0)),
                pl.BlockSpec((dec_seq, C_WIDTH), lambda s, t, pt: (s, OFF_CG // C_WIDTH)),
            ],
            out_specs=pl.BlockSpec((dec_seq, C_WIDTH), lambda s, t, pt: (s, 0)),
            scratch_shapes=[
                pltpu.VMEM((nrow, C_HEAD_DIM), BF16),
                pltpu.VMEM((nrow, 1), F32),
                pltpu.VMEM((nrow, 1), F32),
                pltpu.VMEM((nrow, C_HEAD_DIM), F32),
            ],
        ),
        out_shape=jax.ShapeDtypeStruct((rows, C_WIDTH), p_all.dtype),
        compiler_params=_cparams("arbitrary", "arbitrary"),
        name="fox_decode",
    )(page_table, q_aug, *([cache_k_hm] * gpp), *([cache_v_hm] * gpp), ck_past, ck_new, k_new, v_new, p_all)


def _merge_kernel(x_ref, a_ref, b_ref, c_ref, mg0_ref, mg1_ref, mg2_ref, bm_ref,
                  wa_ref, wb_ref, wc_ref, wo_ref, y_ref):
    d = D_MODEL

    def gated(o_ref, w_ref, mg_ref, k):
        gate = jax.nn.sigmoid(mg_ref[...].astype(F32) + bm_ref[:, k * d:(k + 1) * d])
        return gate * jnp.dot(o_ref[...].astype(BF16), w_ref[...], preferred_element_type=F32)

    merged = gated(a_ref, wa_ref, mg0_ref, 0) + gated(b_ref, wb_ref, mg1_ref, 1) + gated(c_ref, wc_ref, mg2_ref, 2)
    y_ref[...] = x_ref[...] + jnp.dot(merged.astype(BF16), wo_ref[...], preferred_element_type=F32)


def _merge(x2, a_o, b_o, c_o, p_all, b_merge, wa, wb, wc, wo):
    rows = x2.shape[0]
    tm = _pick_tile(rows, 256)
    d = D_MODEL
    bw = B_HEADS * B_DVP

    def const(shape):
        return pl.BlockSpec(shape, lambda i: (0, 0), pipeline_mode=pl.Buffered(1))

    return pl.pallas_call(
        _merge_kernel,
        grid=(rows // tm,),
        in_specs=[
            pl.BlockSpec((tm, d), lambda i: (i, 0)),
            pl.BlockSpec((tm, A_WIDTH), lambda i: (i, 0)),
            pl.BlockSpec((tm, bw), lambda i: (i, 0)),
            pl.BlockSpec((tm, C_WIDTH), lambda i: (i, 0)),
            pl.BlockSpec((tm, d), lambda i: (i, OFF_MG // d + 0)),
            pl.BlockSpec((tm, d), lambda i: (i, OFF_MG // d + 1)),
            pl.BlockSpec((tm, d), lambda i: (i, OFF_MG // d + 2)),
            const((1, N_BRANCH * d)),
            const((A_WIDTH, d)),
            const((bw, d)),
            const((C_WIDTH, d)),
            const((d, d)),
        ],
        out_specs=pl.BlockSpec((tm, d), lambda i: (i, 0)),
        out_shape=jax.ShapeDtypeStruct((rows, d), F32),
        compiler_params=_cparams("parallel"),
        name="merge_out",
    )(x2, a_o, b_o, c_o, p_all, p_all, p_all, b_merge.reshape(1, N_BRANCH * d), wa, wb, wc, wo)


def _pad_heads(x, n_heads, d, dp, axis):
    shp = x.shape
    x = x.reshape(shp[:axis] + (n_heads, d) + shp[axis + 1:])
    pad = [(0, 0)] * x.ndim
    pad[axis + 1] = (0, dp - d)
    x = jnp.pad(x, pad)
    return x.reshape(shp[:axis] + (n_heads * dp,) + shp[axis + 1:])


def _pack_plan():
    src = {}
    acc = 0
    for name, size in zip(("a_u", "a_g", "b_q", "b_k", "b_v", "b_g", "b_r", "c_q", "c_k", "c_v", "c_g", "c_f", "m_g"),
                          SPLIT_SIZES):
        src[name] = acc
        acc += size
    d_in = acc
    segs = [(OFF_MG, "m_g", 1, 6144, 6144), (OFF_CG, "c_g", 1, C_WIDTH, C_WIDTH), (OFF_CQ, "c_q", 1, C_WIDTH, C_WIDTH),
            (OFF_CK, "c_k", 1, C_WIDTH, C_WIDTH), (OFF_CV, "c_v", 1, C_WIDTH, C_WIDTH),
            (OFF_AU, "a_u", 1, A_WIDTH, A_WIDTH), (OFF_AG, "a_g", 1, A_WIDTH, A_WIDTH),
            (OFF_BQ, "b_q", B_HEADS, B_DK, B_DKP), (OFF_BK, "b_k", B_HEADS, B_DK, B_DKP),
            (OFF_BV, "b_v", B_HEADS, B_DV, B_DVP), (OFF_BG, "b_g", B_HEADS, B_DV, B_DVP),
            (OFF_SM, "b_r", 1, B_GATE_RANK, LANES), (OFF_SM + LANES, "c_f", 1, C_HEADS, LANES)]
    starts = [0] * (NP // LANES)
    nvalid = [0] * (NP // LANES)
    for off, name, heads, dreal, dpad in segs:
        for h in range(heads):
            for b in range(dpad // LANES):
                blk = (off + h * dpad) // LANES + b
                n = max(0, min(LANES, dreal - b * LANES))
                s = src[name] + h * dreal + b * LANES
                assert s + LANES <= d_in
                starts[blk], nvalid[blk] = (s if n else 0), n
    return starts, nvalid


def _packw_kernel(st_ref, nv_ref, w_ref, o_ref):
    del st_ref
    nv = nv_ref[pl.program_id(0)]
    live = _iota((LANES, D_MODEL), 0) < nv
    for l in range(o_ref.shape[0]):
        o_ref[l] = jnp.where(live, w_ref[:, l, :], 0.0).astype(BF16)


def _pack_w_in(w_in):
    depth = w_in.shape[0]
    w_t = jnp.transpose(w_in, (2, 0, 1))
    starts, nvalid = _pack_plan()
    return pl.pallas_call(
        _packw_kernel,
        grid_spec=pltpu.PrefetchScalarGridSpec(
            num_scalar_prefetch=2,
            grid=(NP // LANES,),
            in_specs=[pl.BlockSpec((pl.Element(LANES), pl.Element(depth), pl.Element(D_MODEL)),
                                   lambda j, st, nv: (st[j], 0, 0))],
            out_specs=pl.BlockSpec((depth, LANES, D_MODEL), lambda j, st, nv: (0, j, 0)),
        ),
        out_shape=jax.ShapeDtypeStruct((depth, NP, D_MODEL), BF16),
        compiler_params=_cparams("parallel"),
        name="pack_w_in",
    )(jnp.asarray(starts, jnp.int32), jnp.asarray(nvalid, jnp.int32), w_t)


def _layer_params(l, norm_g, pool_w, pool_scale, gla_w_a2, gla_b_a, gla_norm, fox_b_f, fox_q_norm,
                  fox_k_norm, w_branch_a, w_branch_b, w_branch_c, b_merge, w_out):
    wa2 = _pad_heads(gla_w_a2[l], B_HEADS, B_DK, B_DKP, 1)
    return dict(
        norm_g=norm_g[l],
        pool_w=pool_w[l],
        pool_scale=pool_scale[l],
        wa2=jnp.pad(wa2, ((0, LANES - B_GATE_RANK), (0, 0))),
        ba=_pad_heads(gla_b_a[l].reshape(1, -1), B_HEADS, B_DK, B_DKP, 1),
        gn=_pad_heads(gla_norm[l].reshape(1, -1), B_HEADS, B_DV, B_DVP, 1),
        bf=jnp.pad(fox_b_f[l].reshape(1, -1), ((0, 0), (0, LANES - C_HEADS))),
        qn=fox_q_norm[l],
        kn=fox_k_norm[l],
        wa=w_branch_a[l].astype(BF16),
        wb=_pad_heads(w_branch_b[l], B_HEADS, B_DV, B_DVP, 0).astype(BF16),
        wc=w_branch_c[l].astype(BF16),
        b_merge=b_merge[l],
        wo=w_out[l].astype(BF16),
    )


def _state_to_kernel(s):
    st = jnp.swapaxes(s, -1, -2)
    return jnp.pad(st, ((0, 0), (0, 0), (0, B_DVP - B_DV), (0, B_DKP - B_DK)))


def _state_from_kernel(st):
    return jnp.swapaxes(st[:, :, :B_DV, :B_DK], -1, -2)


def kernel(x_prompt, x_sample, cache_k, cache_v, cache_logf, state_gla, state_pool, page_table, norm_g, w_in, pool_w, pool_scale, gla_w_a2, gla_b_a, gla_norm, fox_b_f, fox_q_norm, fox_k_norm, w_branch_a, w_branch_b, w_branch_c, b_merge, w_out):
    depth = w_in.shape[0]
    bp, seq, d = x_prompt.shape
    n_seq, dec_seq, _ = x_sample.shape
    assert d == D_MODEL and dec_seq == SUBLANES and seq % LANES == 0
    yp = x_prompt.reshape(bp * seq, d)
    ys = x_sample.reshape(n_seq * dec_seq, d)
    cache_k_hm = jnp.transpose(cache_k, (0, 1, 3, 2, 4))
    cache_v_hm = jnp.transpose(cache_v, (0, 1, 3, 2, 4))
    logf_hp = jnp.transpose(cache_logf, (0, 3, 1, 2))
    zero_state = jnp.zeros((bp, B_HEADS, B_DVP, B_DKP), F32)
    w_packed = _pack_w_in(w_in)
    outs = {k: [] for k in ("kp", "vp", "lp", "gp", "pp", "ks", "vs", "ls", "gs", "ps")}
    for l in range(depth):
        prm = _layer_params(l, norm_g, pool_w, pool_scale, gla_w_a2, gla_b_a, gla_norm, fox_b_f,
                            fox_q_norm, fox_k_norm, w_branch_a, w_branch_b, w_branch_c, b_merge, w_out)

        p_all, f_all = _inproj(yp, prm["norm_g"], w_packed, l, BF16)
        a_o = _pool_branch(p_all, prm["pool_w"], prm["pool_scale"], seq_len=seq)
        b_o, s_fin = _gla_branch(p_all, f_all, prm["wa2"], prm["ba"], prm["gn"], zero_state,
                                 n_seq=bp, seq_len=seq, decode=False)
        q_aug, k_aug, k_n, v_n, lf = _fox_prep(p_all, f_all, prm["qn"], prm["kn"], prm["bf"], seq_len=seq)
        c_o = _fox_flash(q_aug, k_aug, p_all, n_seq=bp, seq_len=seq)
        yp = _merge(yp, a_o, b_o, c_o, p_all, prm["b_merge"], prm["wa"], prm["wb"], prm["wc"], prm["wo"])
        outs["kp"].append(jnp.swapaxes(k_n, 1, 2))
        outs["vp"].append(jnp.swapaxes(v_n, 1, 2))
        outs["lp"].append(lf[:, :C_HEADS].reshape(bp, seq, C_HEADS))
        outs["gp"].append(_state_from_kernel(s_fin))
        a_u = p_all[:, OFF_AU:OFF_AU + A_WIDTH].reshape(bp, seq, A_WIDTH)
        outs["pp"].append(a_u[:, seq - A_BUF:].astype(F32))

        p_s, f_s = _inproj(ys, prm["norm_g"], w_packed, l, F32)
        pool_state = jnp.pad(state_pool[l], ((0, 0), (1, 0), (0, 0)))
        a_s = _pool_branch(p_s, prm["pool_w"], prm["pool_scale"], seq_len=dec_seq, state=pool_state)
        b_s, s_fin_s = _gla_branch(p_s, f_s, prm["wa2"], prm["ba"], prm["gn"], _state_to_kernel(state_gla[l]),
                                   n_seq=n_seq, seq_len=dec_seq, decode=True)
        q_aug_s, _, k_ns, v_ns, lf_s = _fox_prep(p_s, f_s, prm["qn"], prm["kn"], prm["bf"], seq_len=dec_seq)
        lfn_t = jnp.swapaxes(lf_s.reshape(n_seq, dec_seq, LANES)[:, :, :SUBLANES], 1, 2)
        lfn_t = jnp.pad(lfn_t, ((0, 0), (0, 0), (0, LANES - dec_seq)))
        ck_past, ck_new = _decode_bias(page_table, logf_hp, lfn_t, layer=l)
        c_s = _fox_decode(page_table, q_aug_s, cache_k_hm, cache_v_hm, ck_past, ck_new, k_ns, v_ns, p_s,
                          layer=l, dec_seq=dec_seq)
        ys = _merge(ys, a_s, b_s, c_s, p_s, prm["b_merge"], prm["wa"], prm["wb"], prm["wc"], prm["wo"])
        outs["ks"].append(jnp.swapaxes(k_ns, 1, 2))
        outs["vs"].append(jnp.swapaxes(v_ns, 1, 2))
        outs["ls"].append(lf_s[:, :C_HEADS].reshape(n_seq, dec_seq, C_HEADS))
        outs["gs"].append(_state_from_kernel(s_fin_s))
        a_us = p_s[:, OFF_AU:OFF_AU + A_WIDTH].reshape(n_seq, dec_seq, A_WIDTH).astype(F32)
        outs["ps"].append(jnp.concatenate([state_pool[l], a_us], axis=1)[:, -A_BUF:])

    st = lambda k: jnp.stack(outs[k])
    return (yp.reshape(bp, seq, d), ys.reshape(n_seq, dec_seq, d),
            st("kp"), st("vp"), st("lp"), st("gp"), st("pp"),
            st("ks"), st("vs"), st("ls"), st("gs"), st("ps"))
```

```python
import functools
import math

import jax
import jax.numpy as jnp
from jax import lax
from jax.experimental import pallas as pl
from jax.experimental.pallas import tpu as pltpu

F32 = jnp.float32
BF16 = jnp.bfloat16
HIGHEST = lax.Precision.HIGHEST

D_MODEL = 2048
A_WIDTH = 512
A_WINDOWS = (2, 4, 8, 16)
A_GROUPS = 4
A_GROUP_DIM = 128
A_BUF = 15
B_HEADS = 4
B_DK = 96
B_DV = 192
B_GATE_RANK = 16
B_GATE_TEMP = 16.0
B_CHUNK = 32
C_HEADS = 6
C_HEAD_DIM = 128
C_WIDTH = 768
N_BRANCH = 3
EPS = 1e-6
PAGE_SIZE = 128
SPLIT_SIZES = (512, 512, 384, 384, 768, 768, 16, 768, 768, 768, 768, 6, 6144)

LANES = 128
SUBLANES = 8
VMEM_LIMIT_BYTES = 56 * 1024 * 1024

B_DKP = 128
B_DVP = 256

OFF_MG = 0
OFF_CG = 6144
OFF_CQ = OFF_CG + C_WIDTH
OFF_CK = OFF_CQ + C_WIDTH
OFF_CV = OFF_CK + C_WIDTH
OFF_AU = OFF_CV + C_WIDTH
OFF_AG = OFF_AU + A_WIDTH
OFF_BQ = OFF_AG + A_WIDTH
OFF_BK = OFF_BQ + B_HEADS * B_DKP
OFF_BV = OFF_BK + B_HEADS * B_DKP
OFF_BG = OFF_BV + B_HEADS * B_DVP
OFF_SM = OFF_BG + B_HEADS * B_DVP
SM_WIDTH = 2 * LANES
PROJ_TN = 1536
NP = 13824
assert OFF_SM + SM_WIDTH <= NP and NP % PROJ_TN == 0
SM_TILE = OFF_SM // PROJ_TN
SM_LOCAL = OFF_SM - SM_TILE * PROJ_TN
assert SM_LOCAL + SM_WIDTH <= PROJ_TN


def _pick_tile(n, pref, align=SUBLANES):
    if n <= pref:
        return n
    t = (pref // align) * align
    while t > align and n % t:
        t -= align
    assert n % t == 0, (n, pref)
    return t


def _cparams(*sem):
    return pltpu.CompilerParams(dimension_semantics=sem, vmem_limit_bytes=VMEM_LIMIT_BYTES)


def _log_sigmoid(x):
    return jnp.minimum(x, 0.0) - jnp.log1p(jnp.exp(-jnp.abs(x)))


def _silu(x):
    return x * jax.nn.sigmoid(x)


def _iota(shape, dim):
    return lax.broadcasted_iota(jnp.int32, shape, dim)


def _inproj_kernel(x_ref, xs_ref, g_ref, w_ref, p_ref, f_ref, ps_ref, fs_ref, h_scr, hs_scr, *, nj, rchunk):
    i = pl.program_id(0)
    j = pl.program_id(1)
    last_i = pl.num_programs(0) - 1

    def rms(x):
        r = lax.rsqrt(jnp.mean(x * x, axis=-1, keepdims=True) + EPS)
        return ((x * r) * g_ref[...]).astype(BF16)

    @pl.when(j == 0)
    def _():
        def body(c, carry):
            r0 = pl.multiple_of(c * rchunk, rchunk)
            h_scr[pl.ds(r0, rchunk), :] = rms(x_ref[pl.ds(r0, rchunk), :])
            return carry

        lax.fori_loop(0, x_ref.shape[0] // rchunk, body, 0)

        @pl.when(i == last_i)
        def _():
            hs_scr[...] = rms(xs_ref[...])

    dims = (((1,), (1,)), ((), ()))
    acc = lax.dot_general(h_scr[...], w_ref[...], dims, preferred_element_type=F32)
    p_ref[...] = acc.astype(p_ref.dtype)

    @pl.when(j == nj - 1)
    def _():
        f_ref[...] = acc[:, SM_LOCAL:SM_LOCAL + SM_WIDTH]

    @pl.when(i == last_i)
    def _():
        acc_s = lax.dot_general(hs_scr[...], w_ref[...], dims, preferred_element_type=F32)
        ps_ref[...] = acc_s

        @pl.when(j == nj - 1)
        def _():
            fs_ref[...] = acc_s[:, SM_LOCAL:SM_LOCAL + SM_WIDTH]


def _inproj(x2, xs2, norm_g, w_packed, layer):
    rows = x2.shape[0]
    rows_s = xs2.shape[0]
    tm = _pick_tile(rows, 1024)
    ni = rows // tm
    nj = NP // PROJ_TN
    assert SM_TILE == nj - 1
    rchunk = _pick_tile(tm, 128)
    return pl.pallas_call(
        functools.partial(_inproj_kernel, nj=nj, rchunk=rchunk),
        grid=(ni, nj),
        in_specs=[
            pl.BlockSpec((tm, D_MODEL), lambda i, j: (i, 0)),
            pl.BlockSpec((rows_s, D_MODEL), lambda i, j: (0, 0)),
            pl.BlockSpec((1, D_MODEL), lambda i, j: (0, 0)),
            pl.BlockSpec((None, PROJ_TN, D_MODEL), lambda i, j: (layer, j, 0)),
        ],
        out_specs=[
            pl.BlockSpec((tm, PROJ_TN), lambda i, j: (i, j)),
            pl.BlockSpec((tm, SM_WIDTH), lambda i, j: (i, 0)),
            pl.BlockSpec((rows_s, PROJ_TN), lambda i, j: (0, jnp.where(i == ni - 1, j, 0))),
            pl.BlockSpec((rows_s, SM_WIDTH), lambda i, j: (0, 0)),
        ],
        out_shape=[
            jax.ShapeDtypeStruct((rows, NP), BF16),
            jax.ShapeDtypeStruct((rows, SM_WIDTH), F32),
            jax.ShapeDtypeStruct((rows_s, NP), F32),
            jax.ShapeDtypeStruct((rows_s, SM_WIDTH), F32),
        ],
        scratch_shapes=[pltpu.VMEM((tm, D_MODEL), BF16), pltpu.VMEM((rows_s, D_MODEL), BF16)],
        compiler_params=_cparams("arbitrary", "arbitrary"),
        name="inproj",
    )(x2, xs2, norm_g.reshape(1, D_MODEL), w_packed)


def _pool_kernel(u_ref, g_ref, halo_ref, pw_ref, sc_ref, o_ref, *, tiles_per_seq, decode):
    i = pl.program_id(0)
    tr = u_ref.shape[0]
    hr = halo_ref.shape[0]
    mm_dt = F32 if decode else BF16
    u = u_ref[...].astype(F32)
    halo = halo_ref[...].astype(F32)
    if not decode:
        halo = jnp.where(i % tiles_per_seq == 0, 0.0, halo)
    ext = jnp.concatenate([halo, u], axis=0).astype(mm_dt)
    r = _iota((tr, hr + tr), 0)
    c = _iota((tr, hr + tr), 1) - hr
    rowpos = _iota((tr, 1), 0)
    pos = ((i % tiles_per_seq) * tr + rowpos).astype(F32)
    outs = []
    for g, w in enumerate(A_WINDOWS):
        lo, hi = g * A_GROUP_DIM, (g + 1) * A_GROUP_DIM
        band = ((c >= r - (w - 1)) & (c <= r)).astype(mm_dt)
        if decode:
            win_sum = jnp.dot(band, ext[:, lo:hi], preferred_element_type=F32, precision=HIGHEST)
            count = float(w)
        else:
            win_sum = jnp.dot(band, ext[:, lo:hi], preferred_element_type=F32)
            count = jnp.minimum(pos + 1.0, float(w))
        pooled = win_sum / count - u[:, lo:hi]
        outs.append(jnp.dot(pooled.astype(mm_dt), pw_ref[g].astype(mm_dt), preferred_element_type=F32))
    mixed = jnp.concatenate(outs, axis=1) * sc_ref[...]
    o_ref[...] = (mixed * _silu(g_ref[...].astype(F32))).astype(o_ref.dtype)


def _pool_branch(p_all, pool_w, pool_scale, *, seq_len, state=None):
    rows = p_all.shape[0]
    decode = state is not None
    wdt = F32 if decode else BF16
    if decode:
        tr, tps = seq_len, 1
        halo_spec = pl.BlockSpec((None, 16, A_WIDTH), lambda i: (i, 0, 0))
        halo_arr = state
    else:
        tr = _pick_tile(seq_len, 256)
        tps = seq_len // tr
        hb = LANES
        assert tr % hb == 0
        halo_spec = pl.BlockSpec((hb, A_WIDTH), lambda i: (jnp.maximum(i * (tr // hb) - 1, 0), OFF_AU // A_WIDTH))
        halo_arr = p_all
    return pl.pallas_call(
        functools.partial(_pool_kernel, tiles_per_seq=tps, decode=decode),
        grid=(rows // tr,),
        in_specs=[
            pl.BlockSpec((tr, A_WIDTH), lambda i: (i, OFF_AU // A_WIDTH)),
            pl.BlockSpec((tr, A_WIDTH), lambda i: (i, OFF_AG // A_WIDTH)),
            halo_spec,
            pl.BlockSpec((A_GROUPS, A_GROUP_DIM, A_GROUP_DIM), lambda i: (0, 0, 0)),
            pl.BlockSpec((1, A_WIDTH), lambda i: (0, 0)),
        ],
        out_specs=pl.BlockSpec((tr, A_WIDTH), lambda i: (i, 0)),
        out_shape=jax.ShapeDtypeStruct((rows, A_WIDTH), p_all.dtype),
        compiler_params=_cparams("arbitrary"),
        name="pool_decode" if decode else "pool_prompt",
    )(p_all, p_all, halo_arr, pool_w.astype(wdt), pool_scale.reshape(1, A_WIDTH))


def _gla_kernel(q_ref, k_ref, v_ref, g_ref, r_ref, wa_ref, ba_ref, gn_ref, s0_ref,
                o_ref, so_ref, s_scr, *, chunk, decode):
    tb = pl.program_id(1)
    nb = pl.num_programs(1)
    tbk = q_ref.shape[0]
    mm_dt = F32 if decode else BF16

    @pl.when(tb == 0)
    def _():
        s_scr[...] = s0_ref[...]

    logit = jnp.dot(r_ref[...], wa_ref[...], preferred_element_type=F32, precision=HIGHEST) + ba_ref[...]
    log_a = _log_sigmoid(logit) / B_GATE_TEMP
    tri = (_iota((chunk, chunk), 0) >= _iota((chunk, chunk), 1))
    tri_f = tri.astype(F32)
    scale = B_DK ** -0.5
    kw = log_a.shape[1]
    if not decode:
        la_hi = log_a.astype(BF16)
        la_r1 = log_a - la_hi.astype(F32)
        la_mid = la_r1.astype(BF16)
        la_lo = (la_r1 - la_mid.astype(F32)).astype(BF16)
        la_terms = jnp.concatenate([la_hi, la_mid, la_lo], axis=1)
        tri_b = tri.astype(BF16)
    for c in range(tbk // chunk):
        rs = slice(c * chunk, (c + 1) * chunk)
        if decode:
            b = jnp.dot(tri_f, log_a[rs], preferred_element_type=F32, precision=HIGHEST)
        else:
            b3 = jnp.dot(tri_b, la_terms[rs], preferred_element_type=F32)
            b = (b3[:, :kw] + b3[:, kw:2 * kw]) + b3[:, 2 * kw:]
        b_last = b[chunk - 1:chunk, :]
        qc = q_ref[rs, :].astype(F32)
        kc = k_ref[rs, :].astype(F32)
        q_t = (qc * scale) * jnp.exp(b)
        k_t = kc * jnp.exp(-b)
        k_end = kc * jnp.exp(b_last - b)
        decay = jnp.exp(b_last)
        for h in range(B_HEADS):
            ks = slice(h * B_DKP, (h + 1) * B_DKP)
            vs = slice(h * B_DVP, (h + 1) * B_DVP)
            qh = q_t[:, ks].astype(mm_dt)
            att = lax.dot_general(qh, k_t[:, ks].astype(mm_dt), (((1,), (1,)), ((), ())),
                                  preferred_element_type=F32)
            att = jnp.where(tri, att, 0.0)
            vh = v_ref[rs, vs].astype(mm_dt)
            st = s_scr[h]
            o = jnp.dot(att.astype(mm_dt), vh, preferred_element_type=F32)
            o = o + lax.dot_general(qh, st.astype(mm_dt), (((1,), (1,)), ((), ())),
                                    preferred_element_type=F32)
            kv_t = lax.dot_general(vh, k_end[:, ks].astype(mm_dt), (((0,), (0,)), ((), ())),
                                   preferred_element_type=F32)
            s_scr[h] = st * decay[:, ks] + kv_t
            r = lax.rsqrt(jnp.sum(o * o, axis=-1, keepdims=True) * (1.0 / B_DV) + EPS)
            on = (o * r) * gn_ref[:, vs]
            o_ref[rs, vs] = (on * _silu(g_ref[rs, vs].astype(F32))).astype(o_ref.dtype)

    @pl.when(tb == nb - 1)
    def _():
        so_ref[...] = s_scr[...]


def _gla_branch(p_all, f_all, wa_p, ba_p, gn_p, s0_t, *, n_seq, seq_len, decode):
    rows = p_all.shape[0]
    chunk = math.gcd(seq_len, B_CHUNK)
    tbk = _pick_tile(seq_len, 256)
    nb = seq_len // tbk
    kw = B_HEADS * B_DKP
    vw = B_HEADS * B_DVP
    return pl.pallas_call(
        functools.partial(_gla_kernel, chunk=chunk, decode=decode),
        grid=(n_seq, nb),
        in_specs=[
            pl.BlockSpec((tbk, kw), lambda s, t: (s * nb + t, OFF_BQ // kw)),
            pl.BlockSpec((tbk, kw), lambda s, t: (s * nb + t, OFF_BK // kw)),
            pl.BlockSpec((tbk, vw), lambda s, t: (s * nb + t, OFF_BV // vw)),
            pl.BlockSpec((tbk, vw), lambda s, t: (s * nb + t, OFF_BG // vw)),
            pl.BlockSpec((tbk, LANES), lambda s, t: (s * nb + t, 0)),
            pl.BlockSpec((LANES, kw), lambda s, t: (0, 0)),
            pl.BlockSpec((1, kw), lambda s, t: (0, 0)),
            pl.BlockSpec((1, vw), lambda s, t: (0, 0)),
            pl.BlockSpec((None, B_HEADS, B_DVP, B_DKP), lambda s, t: (s, 0, 0, 0)),
        ],
        out_specs=[
            pl.BlockSpec((tbk, vw), lambda s, t: (s * nb + t, 0)),
            pl.BlockSpec((None, B_HEADS, B_DVP, B_DKP), lambda s, t: (s, 0, 0, 0)),
        ],
        out_shape=[
            jax.ShapeDtypeStruct((rows, vw), p_all.dtype),
            jax.ShapeDtypeStruct((n_seq, B_HEADS, B_DVP, B_DKP), F32),
        ],
        scratch_shapes=[pltpu.VMEM((B_HEADS, B_DVP, B_DKP), F32)],
        compiler_params=_cparams("arbitrary", "arbitrary"),
        name="gla_decode" if decode else "gla_prompt",
    )(p_all, p_all, p_all, p_all, f_all, wa_p, ba_p, gn_p, s0_t)


def _foxprep_kernel(cq_ref, ck_ref, cv_ref, cf_ref, qn_ref, kn_ref, bf_ref,
                    qa_ref, ka_ref, ko_ref, vo_ref, lf_ref, carry_scr, *, tiles_per_seq):
    i = pl.program_id(0)
    tr = cq_ref.shape[0]
    hd = C_HEAD_DIM
    lane = _iota((tr, LANES), 1)

    lf = jnp.where(lane < C_HEADS, _log_sigmoid(cf_ref[...] + bf_ref[...]), 0.0)
    lf_ref[...] = lf

    @pl.when(i % tiles_per_seq == 0)
    def _():
        carry_scr[...] = jnp.zeros_like(carry_scr)

    tri = (_iota((tr, tr), 0) >= _iota((tr, tr), 1)).astype(F32)
    cum = jnp.dot(tri, lf, preferred_element_type=F32, precision=HIGHEST) + carry_scr[...]
    carry_scr[...] = cum[tr - 1:tr, :]

    def qk_norm(x, g_ref):
        r = lax.rsqrt(jnp.mean(x * x, axis=-1, keepdims=True) + EPS)
        return (x * r) * g_ref[...]

    one = jnp.ones((tr, LANES), F32)
    zero = jnp.zeros((tr, LANES), F32)
    for h in range(C_HEADS):
        cs = slice(h * hd, (h + 1) * hd)
        ch = jnp.sum(jnp.where(lane == h, cum, 0.0), axis=-1, keepdims=True)
        hi = ch.astype(BF16).astype(F32)
        r1 = ch - hi
        mid = r1.astype(BF16).astype(F32)
        lo = r1 - mid
        qn = qk_norm(cq_ref[:, cs].astype(F32), qn_ref)
        kn = qk_norm(ck_ref[:, cs].astype(F32), kn_ref)
        aux_q = jnp.where(lane == 0, hi, jnp.where(lane == 1, mid, jnp.where(lane == 2, lo,
                          jnp.where(lane < 6, one, zero))))
        aux_k = jnp.where(lane < 3, one, jnp.where(lane == 3, -hi, jnp.where(lane == 4, -mid,
                          jnp.where(lane == 5, -lo, zero))))
        a0 = 2 * h * hd
        qa_ref[:, a0:a0 + hd] = (qn * (hd ** -0.5)).astype(qa_ref.dtype)
        qa_ref[:, a0 + hd:a0 + 2 * hd] = aux_q.astype(qa_ref.dtype)
        ka_ref[:, a0:a0 + hd] = kn.astype(ka_ref.dtype)
        ka_ref[:, a0 + hd:a0 + 2 * hd] = aux_k.astype(ka_ref.dtype)
        ko_ref[h] = kn
        vo_ref[h] = cv_ref[:, cs].astype(F32)


def _fox_prep(p_all, f_all, fox_q_norm, fox_k_norm, bf_p, *, seq_len):
    rows = p_all.shape[0]
    tr = _pick_tile(seq_len, 256)
    tps = seq_len // tr
    hd = C_HEAD_DIM
    cw = C_WIDTH
    aw = C_HEADS * 2 * hd
    return pl.pallas_call(
        functools.partial(_foxprep_kernel, tiles_per_seq=tps),
        grid=(rows // tr,),
        in_specs=[
            pl.BlockSpec((tr, cw), lambda i: (i, OFF_CQ // cw)),
            pl.BlockSpec((tr, cw), lambda i: (i, OFF_CK // cw)),
            pl.BlockSpec((tr, cw), lambda i: (i, OFF_CV // cw)),
            pl.BlockSpec((tr, LANES), lambda i: (i, 1)),
            pl.BlockSpec((1, hd), lambda i: (0, 0)),
            pl.BlockSpec((1, hd), lambda i: (0, 0)),
            pl.BlockSpec((1, LANES), lambda i: (0, 0)),
        ],
        out_specs=[
            pl.BlockSpec((tr, aw), lambda i: (i, 0)),
            pl.BlockSpec((tr, aw), lambda i: (i, 0)),
            pl.BlockSpec((None, C_HEADS, tr, hd), lambda i: (i // tps, 0, i % tps, 0)),
            pl.BlockSpec((None, C_HEADS, tr, hd), lambda i: (i // tps, 0, i % tps, 0)),
            pl.BlockSpec((tr, LANES), lambda i: (i, 0)),
        ],
        out_shape=[
            jax.ShapeDtypeStruct((rows, aw), p_all.dtype),
            jax.ShapeDtypeStruct((rows, aw), p_all.dtype),
            jax.ShapeDtypeStruct((rows // seq_len, C_HEADS, seq_len, hd), F32),
            jax.ShapeDtypeStruct((rows // seq_len, C_HEADS, seq_len, hd), F32),
            jax.ShapeDtypeStruct((rows, LANES), F32),
        ],
        scratch_shapes=[pltpu.VMEM((1, LANES), F32)],
        compiler_params=_cparams("arbitrary"),
        name="fox_prep",
    )(p_all, p_all, p_all, f_all, fox_q_norm.reshape(1, hd), fox_k_norm.reshape(1, hd), bf_p)


def _flash_kernel(q_ref, k_ref, v_ref, g_ref, o_ref, m_scr, l_scr, acc_scr, *, rs):
    i = pl.program_id(2)
    t = q_ref.shape[0]
    hd = C_HEAD_DIM
    m_scr[...] = jnp.full_like(m_scr, -jnp.inf)
    l_scr[...] = jnp.zeros_like(l_scr)
    acc_scr[...] = jnp.zeros_like(acc_scr)
    ones = jnp.ones((t, LANES), BF16)

    def step(j, diagonal):
        k0 = pl.multiple_of(j * t, t)
        kb = k_ref[pl.ds(k0, t), :]
        vb = jnp.concatenate([v_ref[pl.ds(k0, t), :], ones], axis=1)
        for r in range(t // rs):
            rows = slice(r * rs, (r + 1) * rs)
            nk = (r + 1) * rs if diagonal else t
            s = lax.dot_general(q_ref[rows, :], kb[:nk], (((1,), (1,)), ((), ())), preferred_element_type=F32)
            if diagonal:
                s = jnp.where(_iota((rs, nk), 0) + r * rs >= _iota((rs, nk), 1), s, -jnp.inf)
            m_prev = m_scr[rows, :]
            m_new = jnp.maximum(m_prev, jnp.max(s, axis=-1, keepdims=True))
            alpha = jnp.exp(m_prev - m_new)
            p = jnp.exp(s - jnp.concatenate([m_new] * (nk // LANES), axis=1))
            pv = jnp.dot(p.astype(BF16), vb[:nk], preferred_element_type=F32)
            l_scr[rows, :] = alpha * l_scr[rows, :] + pv[:, hd:]
            acc_scr[rows, :] = alpha * acc_scr[rows, :] + pv[:, :hd]
            m_scr[rows, :] = m_new

    def pair(jj, carry):
        step(2 * jj, False)
        step(2 * jj + 1, False)
        return carry

    lax.fori_loop(0, i // 2, pair, 0)

    @pl.when(i % 2 == 1)
    def _():
        step(i - 1, False)

    step(i, True)
    out = acc_scr[...] / l_scr[...]
    o_ref[...] = (out * _silu(g_ref[...].astype(F32))).astype(BF16)


def _fox_flash(q_aug, k_aug, p_all, *, n_seq, seq_len):
    rows = q_aug.shape[0]
    t = _pick_tile(seq_len, 1024, LANES)
    rs = _pick_tile(t, 256, LANES)
    nt = seq_len // t
    hd = C_HEAD_DIM
    aw = 2 * hd
    return pl.pallas_call(
        functools.partial(_flash_kernel, rs=rs),
        grid=(n_seq, C_HEADS, nt),
        in_specs=[
            pl.BlockSpec((t, aw), lambda b, h, i: (b * nt + i, h)),
            pl.BlockSpec((seq_len, aw), lambda b, h, i: (b, h)),
            pl.BlockSpec((seq_len, hd), lambda b, h, i: (b, OFF_CV // hd + h)),
            pl.BlockSpec((t, hd), lambda b, h, i: (b * nt + i, OFF_CG // hd + h)),
        ],
        out_specs=pl.BlockSpec((t, hd), lambda b, h, i: (b * nt + i, h)),
        out_shape=jax.ShapeDtypeStruct((rows, C_WIDTH), BF16),
        scratch_shapes=[pltpu.VMEM((t, LANES), F32), pltpu.VMEM((t, LANES), F32), pltpu.VMEM((t, hd), F32)],
        compiler_params=_cparams("parallel", "parallel", "arbitrary"),
        name="fox_flash",
    )(q_aug, k_aug, p_all, p_all)


def _decbias_kernel(pt_ref, lf_ref, lfn_ref, ck_ref, ckn_ref, g_scr):
    s = pl.program_id(0)
    npg = g_scr.shape[1]

    def gather(p, carry):
        page = pt_ref[s, p]
        for h in range(C_HEADS):
            g_scr[h, pl.ds(p, 1), :] = lf_ref[h, pl.ds(page, 1), :]
        return carry

    lax.fori_loop(0, npg, gather, 0)
    upper = (_iota((LANES, LANES), 0) <= _iota((LANES, LANES), 1)).astype(F32)
    earlier = (_iota((npg, npg), 1) < _iota((npg, npg), 0)).astype(F32)
    lfn_cum = jnp.dot(lfn_ref[...], upper, preferred_element_type=F32, precision=HIGHEST)
    ckn_ref[...] = jnp.zeros_like(ckn_ref)
    for h in range(C_HEADS):
        cum = jnp.dot(g_scr[h], upper, preferred_element_type=F32, precision=HIGHEST)
        tot = jnp.broadcast_to(cum[:, LANES - 1:LANES], cum.shape)
        before = jnp.dot(earlier, tot, preferred_element_type=F32, precision=HIGHEST)
        ck_ref[h] = cum + before
        ckn_ref[h:h + 1, :] = lfn_cum[h:h + 1, :] + (before[npg - 1:npg, :] + tot[npg - 1:npg, :])


def _decode_bias(page_table, logf_hp, lfn_t, *, layer):
    n_seq, npg = page_table.shape
    n_pool = logf_hp.shape[2]
    return pl.pallas_call(
        _decbias_kernel,
        grid_spec=pltpu.PrefetchScalarGridSpec(
            num_scalar_prefetch=1,
            grid=(n_seq,),
            in_specs=[
                pl.BlockSpec((None, C_HEADS, n_pool, PAGE_SIZE), lambda s, pt: (layer, 0, 0, 0)),
                pl.BlockSpec((None, SUBLANES, LANES), lambda s, pt: (s, 0, 0)),
            ],
            out_specs=[
                pl.BlockSpec((None, C_HEADS, npg, PAGE_SIZE), lambda s, pt: (s, 0, 0, 0)),
                pl.BlockSpec((None, SUBLANES, LANES), lambda s, pt: (s, 0, 0)),
            ],
            scratch_shapes=[pltpu.VMEM((C_HEADS, npg, PAGE_SIZE), F32)],
        ),
        out_shape=[
            jax.ShapeDtypeStruct((n_seq, C_HEADS, npg, PAGE_SIZE), F32),
            jax.ShapeDtypeStruct((n_seq, SUBLANES, LANES), F32),
        ],
        compiler_params=_cparams("arbitrary"),
        name="decode_bias",
    )(page_table, logf_hp, lfn_t)


DEC_QROWS = 16


def _decode_kernel(pt_ref, qa_ref, *refs, gpp, dec_seq):
    del pt_ref
    k_refs = refs[:gpp]
    v_refs = refs[gpp:2 * gpp]
    ck_ref, ckn_ref, kn_ref, vn_ref, g_ref, o_ref, q_scr, m_scr, l_scr, acc_scr = refs[2 * gpp:]
    t = pl.program_id(1)
    nt = pl.num_programs(1)
    nq = dec_seq
    qp = DEC_QROWS
    nrow = C_HEADS * qp
    hd = C_HEAD_DIM
    row = _iota((nrow, LANES), 0)
    lane = _iota((nrow, LANES), 1)
    qi = jnp.bitwise_and(row, qp - 1)

    @pl.when(t == 0)
    def _():
        qa = qa_ref[...].astype(F32)
        zero = jnp.zeros((qp - nq, hd), F32)
        q_scr[...] = jnp.concatenate(
            [jnp.concatenate([qa[:, h * 2 * hd:h * 2 * hd + hd], zero], axis=0) for h in range(C_HEADS)],
            axis=0).astype(BF16)
        m_scr[...] = jnp.full_like(m_scr, -jnp.inf)
        l_scr[...] = jnp.zeros_like(l_scr)
        acc_scr[...] = jnp.zeros_like(acc_scr)

    def head_rows(get_row):
        return jnp.concatenate(
            [jnp.broadcast_to(get_row(h), (qp, get_row(h).shape[1])) for h in range(C_HEADS)], axis=0)

    ckn_rows = head_rows(lambda h: ckn_ref[h:h + 1, :])
    c_q = jnp.sum(jnp.where(lane == qi, ckn_rows, 0.0), axis=-1, keepdims=True)

    def attend(k_of, v_of, ck_rows, mask):
        s = jnp.concatenate(
            [lax.dot_general(q_scr[h * qp:(h + 1) * qp, :], k_of(h), (((1,), (1,)), ((), ())),
                             preferred_element_type=F32) for h in range(C_HEADS)], axis=0)
        s = s + (c_q - ck_rows)
        if mask is not None:
            s = jnp.where(mask, s, -jnp.inf)
        m_prev = m_scr[...]
        m_new = jnp.maximum(m_prev, jnp.max(s, axis=-1, keepdims=True))
        alpha = jnp.exp(m_prev - m_new)
        pr = jnp.exp(s - m_new)
        l_scr[...] = alpha * l_scr[...] + jnp.sum(pr, axis=-1, keepdims=True)
        prb = pr.astype(BF16)
        o = jnp.concatenate(
            [jnp.dot(prb[h * qp:(h + 1) * qp, :], v_of(h), preferred_element_type=F32) for h in range(C_HEADS)],
            axis=0)
        acc_scr[...] = alpha * acc_scr[...] + o
        m_scr[...] = m_new

    attend(lambda h: jnp.concatenate([k_refs[g][h] for g in range(gpp)], axis=0).astype(BF16),
           lambda h: jnp.concatenate([v_refs[g][h] for g in range(gpp)], axis=0).astype(BF16),
           head_rows(lambda h: jnp.concatenate([ck_ref[h, g:g + 1, :] for g in range(gpp)], axis=1)), None)

    @pl.when(t == nt - 1)
    def _():
        pad = jnp.zeros((PAGE_SIZE - nq, hd), F32)
        attend(lambda h: jnp.concatenate([kn_ref[h], pad], axis=0).astype(BF16),
               lambda h: jnp.concatenate([vn_ref[h], pad], axis=0).astype(BF16),
               ckn_rows, lane <= qi)
        out = acc_scr[...] / l_scr[...]
        gate = _silu(g_ref[...].astype(F32))
        for h in range(C_HEADS):
            cs = slice(h * hd, (h + 1) * hd)
            o_ref[:, cs] = (out[h * qp:h * qp + nq, :] * gate[:, cs]).astype(o_ref.dtype)


def _fox_decode(page_table, q_aug, cache_k_hm, cache_v_hm, ck_past, ck_new, k_new, v_new, p_all, *, layer, dec_seq):
    n_seq, npg = page_table.shape
    rows = q_aug.shape[0]
    nrow = C_HEADS * DEC_QROWS
    gpp = _pick_tile(npg, 16, SUBLANES)

    def page_spec(g):
        return pl.BlockSpec((None, None, C_HEADS, PAGE_SIZE, C_HEAD_DIM),
                            lambda s, t, pt: (layer, pt[s, t * gpp + g], 0, 0, 0))

    return pl.pallas_call(
        functools.partial(_decode_kernel, gpp=gpp, dec_seq=dec_seq),
        grid_spec=pltpu.PrefetchScalarGridSpec(
            num_scalar_prefetch=1,
            grid=(n_seq, npg // gpp),
            in_specs=[pl.BlockSpec((dec_seq, C_HEADS * 2 * C_HEAD_DIM), lambda s, t, pt: (s, 0))]
            + [page_spec(g) for g in range(gpp)]
            + [page_spec(g) for g in range(gpp)]
            + [
                pl.BlockSpec((None, C_HEADS, gpp, PAGE_SIZE), lambda s, t, pt: (s, 0, t, 0)),
                pl.BlockSpec((None, SUBLANES, LANES), lambda s, t, pt: (s, 0, 0)),
                pl.BlockSpec((None, C_HEADS, dec_seq, C_HEAD_DIM), lambda s, t, pt: (s, 0, 0, 0)),
                pl.BlockSpec((None, C_HEADS, dec_seq, C_HEAD_DIM), lambda s, t, pt: (s, 0, 0, 0)),
                pl.BlockSpec((dec_seq, C_WIDTH), lambda s, t, pt: (s, OFF_CG // C_WIDTH)),
            ],
            out_specs=pl.BlockSpec((dec_seq, C_WIDTH), lambda s, t, pt: (s, 0)),
            scratch_shapes=[
                pltpu.VMEM((nrow, C_HEAD_DIM), BF16),
                pltpu.VMEM((nrow, 1), F32),
                pltpu.VMEM((nrow, 1), F32),
                pltpu.VMEM((nrow, C_HEAD_DIM), F32),
            ],
        ),
        out_shape=jax.ShapeDtypeStruct((rows, C_WIDTH), p_all.dtype),
        compiler_params=_cparams("arbitrary", "arbitrary"),
        name="fox_decode",
    )(page_table, q_aug, *([cache_k_hm] * gpp), *([cache_v_hm] * gpp), ck_past, ck_new, k_new, v_new, p_all)


def _merge_kernel(x_ref, a_ref, b_ref, c_ref, mg0_ref, mg1_ref, mg2_ref,
                  xs_ref, as_ref, bs_ref, cs_ref, mgs0_ref, mgs1_ref, mgs2_ref,
                  bm_ref, wa_ref, wb_ref, wc_ref, wo_ref, y_ref, ys_ref):
    d = D_MODEL

    def gated(o_ref, w_ref, mg_ref, k):
        gate = jax.nn.sigmoid(mg_ref[...].astype(F32) + bm_ref[:, k * d:(k + 1) * d])
        return gate * jnp.dot(o_ref[...].astype(BF16), w_ref[...], preferred_element_type=F32)

    def layer_out(xr, ar, br, cr, m0, m1, m2):
        merged = gated(ar, wa_ref, m0, 0) + gated(br, wb_ref, m1, 1) + gated(cr, wc_ref, m2, 2)
        return xr[...] + jnp.dot(merged.astype(BF16), wo_ref[...], preferred_element_type=F32)

    y_ref[...] = layer_out(x_ref, a_ref, b_ref, c_ref, mg0_ref, mg1_ref, mg2_ref)

    @pl.when(pl.program_id(0) == pl.num_programs(0) - 1)
    def _():
        ys_ref[...] = layer_out(xs_ref, as_ref, bs_ref, cs_ref, mgs0_ref, mgs1_ref, mgs2_ref)


def _merge(x2, a_o, b_o, c_o, p_all, xs2, a_s, b_s, c_s, p_s, b_merge, wa, wb, wc, wo):
    rows = x2.shape[0]
    rows_s = xs2.shape[0]
    tm = _pick_tile(rows, 256)
    d = D_MODEL
    bw = B_HEADS * B_DVP

    def const(shape, col=0):
        return pl.BlockSpec(shape, lambda i: (0, col), pipeline_mode=pl.Buffered(1))

    return pl.pallas_call(
        _merge_kernel,
        grid=(rows // tm,),
        in_specs=[
            pl.BlockSpec((tm, d), lambda i: (i, 0)),
            pl.BlockSpec((tm, A_WIDTH), lambda i: (i, 0)),
            pl.BlockSpec((tm, bw), lambda i: (i, 0)),
            pl.BlockSpec((tm, C_WIDTH), lambda i: (i, 0)),
            pl.BlockSpec((tm, d), lambda i: (i, OFF_MG // d + 0)),
            pl.BlockSpec((tm, d), lambda i: (i, OFF_MG // d + 1)),
            pl.BlockSpec((tm, d), lambda i: (i, OFF_MG // d + 2)),
            const((rows_s, d)),
            const((rows_s, A_WIDTH)),
            const((rows_s, bw)),
            const((rows_s, C_WIDTH)),
            const((rows_s, d), OFF_MG // d + 0),
            const((rows_s, d), OFF_MG // d + 1),
            const((rows_s, d), OFF_MG // d + 2),
            const((1, N_BRANCH * d)),
            const((A_WIDTH, d)),
            const((bw, d)),
            const((C_WIDTH, d)),
            const((d, d)),
        ],
        out_specs=[pl.BlockSpec((tm, d), lambda i: (i, 0)), pl.BlockSpec((rows_s, d), lambda i: (0, 0))],
        out_shape=[jax.ShapeDtypeStruct((rows, d), F32), jax.ShapeDtypeStruct((rows_s, d), F32)],
        compiler_params=_cparams("arbitrary"),
        name="merge_out",
    )(x2, a_o, b_o, c_o, p_all, p_all, p_all, xs2, a_s, b_s, c_s, p_s, p_s, p_s,
      b_merge.reshape(1, N_BRANCH * d), wa, wb, wc, wo)


def _pad_heads(x, n_heads, d, dp, axis):
    shp = x.shape
    x = x.reshape(shp[:axis] + (n_heads, d) + shp[axis + 1:])
    pad = [(0, 0)] * x.ndim
    pad[axis + 1] = (0, dp - d)
    x = jnp.pad(x, pad)
    return x.reshape(shp[:axis] + (n_heads * dp,) + shp[axis + 1:])


def _pack_plan():
    src = {}
    acc = 0
    for name, size in zip(("a_u", "a_g", "b_q", "b_k", "b_v", "b_g", "b_r", "c_q", "c_k", "c_v", "c_g", "c_f", "m_g"),
                          SPLIT_SIZES):
        src[name] = acc
        acc += size
    d_in = acc
    segs = [(OFF_MG, "m_g", 1, 6144, 6144), (OFF_CG, "c_g", 1, C_WIDTH, C_WIDTH), (OFF_CQ, "c_q", 1, C_WIDTH, C_WIDTH),
            (OFF_CK, "c_k", 1, C_WIDTH, C_WIDTH), (OFF_CV, "c_v", 1, C_WIDTH, C_WIDTH),
            (OFF_AU, "a_u", 1, A_WIDTH, A_WIDTH), (OFF_AG, "a_g", 1, A_WIDTH, A_WIDTH),
            (OFF_BQ, "b_q", B_HEADS, B_DK, B_DKP), (OFF_BK, "b_k", B_HEADS, B_DK, B_DKP),
            (OFF_BV, "b_v", B_HEADS, B_DV, B_DVP), (OFF_BG, "b_g", B_HEADS, B_DV, B_DVP),
            (OFF_SM, "b_r", 1, B_GATE_RANK, LANES), (OFF_SM + LANES, "c_f", 1, C_HEADS, LANES)]
    starts = [0] * (NP // LANES)
    nvalid = [0] * (NP // LANES)
    for off, name, heads, dreal, dpad in segs:
        for h in range(heads):
            for b in range(dpad // LANES):
                blk = (off + h * dpad) // LANES + b
                n = max(0, min(LANES, dreal - b * LANES))
                s = src[name] + h * dreal + b * LANES
                assert s + LANES <= d_in
                starts[blk], nvalid[blk] = (s if n else 0), n
    return starts, nvalid


PACK_WINDOWS = 4


def _packw_kernel(st_ref, nv_ref, *refs):
    del st_ref
    w_refs, o_ref = refs[:PACK_WINDOWS], refs[PACK_WINDOWS]
    j = pl.program_id(0)
    row = _iota((LANES, D_MODEL), 0)
    for g in range(PACK_WINDOWS):
        live = row < nv_ref[j * PACK_WINDOWS + g]
        for l in range(o_ref.shape[0]):
            o_ref[l, g * LANES:(g + 1) * LANES, :] = jnp.where(live, w_refs[g][:, l, :], 0.0).astype(BF16)


def _pack_w_in(w_in):
    depth = w_in.shape[0]
    w_t = jnp.transpose(w_in, (2, 0, 1))
    starts, nvalid = _pack_plan()
    nblk = NP // LANES
    assert nblk % PACK_WINDOWS == 0

    def window(g):
        return pl.BlockSpec((pl.Element(LANES), pl.Element(depth), pl.Element(D_MODEL)),
                            lambda j, st, nv: (st[j * PACK_WINDOWS + g], 0, 0))

    return pl.pallas_call(
        _packw_kernel,
        grid_spec=pltpu.PrefetchScalarGridSpec(
            num_scalar_prefetch=2,
            grid=(nblk // PACK_WINDOWS,),
            in_specs=[window(g) for g in range(PACK_WINDOWS)],
            out_specs=pl.BlockSpec((depth, PACK_WINDOWS * LANES, D_MODEL), lambda j, st, nv: (0, j, 0)),
        ),
        out_shape=jax.ShapeDtypeStruct((depth, NP, D_MODEL), BF16),
        compiler_params=_cparams("parallel"),
        name="pack_w_in",
    )(jnp.asarray(starts, jnp.int32), jnp.asarray(nvalid, jnp.int32), *([w_t] * PACK_WINDOWS))


def _layer_params(l, norm_g, pool_w, pool_scale, gla_w_a2, gla_b_a, gla_norm, fox_b_f, fox_q_norm,
                  fox_k_norm, w_branch_a, w_branch_b, w_branch_c, b_merge, w_out):
    wa2 = _pad_heads(gla_w_a2[l], B_HEADS, B_DK, B_DKP, 1)
    return dict(
        norm_g=norm_g[l],
        pool_w=pool_w[l],
        pool_scale=pool_scale[l],
        wa2=jnp.pad(wa2, ((0, LANES - B_GATE_RANK), (0, 0))),
        ba=_pad_heads(gla_b_a[l].reshape(1, -1), B_HEADS, B_DK, B_DKP, 1),
        gn=_pad_heads(gla_norm[l].reshape(1, -1), B_HEADS, B_DV, B_DVP, 1),
        bf=jnp.pad(fox_b_f[l].reshape(1, -1), ((0, 0), (0, LANES - C_HEADS))),
        qn=fox_q_norm[l],
        kn=fox_k_norm[l],
        wa=w_branch_a[l].astype(BF16),
        wb=_pad_heads(w_branch_b[l], B_HEADS, B_DV, B_DVP, 0).astype(BF16),
        wc=w_branch_c[l].astype(BF16),
        b_merge=b_merge[l],
        wo=w_out[l].astype(BF16),
    )


def _state_to_kernel(s):
    st = jnp.swapaxes(s, -1, -2)
    return jnp.pad(st, ((0, 0), (0, 0), (0, B_DVP - B_DV), (0, B_DKP - B_DK)))


def _state_from_kernel(st):
    return jnp.swapaxes(st[:, :, :B_DV, :B_DK], -1, -2)


def kernel(x_prompt, x_sample, cache_k, cache_v, cache_logf, state_gla, state_pool, page_table, norm_g, w_in, pool_w, pool_scale, gla_w_a2, gla_b_a, gla_norm, fox_b_f, fox_q_norm, fox_k_norm, w_branch_a, w_branch_b, w_branch_c, b_merge, w_out):
    depth = w_in.shape[0]
    bp, seq, d = x_prompt.shape
    n_seq, dec_seq, _ = x_sample.shape
    assert d == D_MODEL and dec_seq == SUBLANES and seq % LANES == 0
    yp = x_prompt.reshape(bp * seq, d)
    ys = x_sample.reshape(n_seq * dec_seq, d)
    cache_k_hm = jnp.transpose(cache_k, (0, 1, 3, 2, 4))
    cache_v_hm = jnp.transpose(cache_v, (0, 1, 3, 2, 4))
    logf_hp = jnp.transpose(cache_logf, (0, 3, 1, 2))
    zero_state = jnp.zeros((bp, B_HEADS, B_DVP, B_DKP), F32)
    w_packed = _pack_w_in(w_in)
    outs = {k: [] for k in ("kp", "vp", "lp", "gp", "pp", "ks", "vs", "ls", "gs", "ps")}
    for l in range(depth):
        prm = _layer_params(l, norm_g, pool_w, pool_scale, gla_w_a2, gla_b_a, gla_norm, fox_b_f,
                            fox_q_norm, fox_k_norm, w_branch_a, w_branch_b, w_branch_c, b_merge, w_out)

        p_all, f_all, p_s, f_s = _inproj(yp, ys, prm["norm_g"], w_packed, l)

        a_o = _pool_branch(p_all, prm["pool_w"], prm["pool_scale"], seq_len=seq)
        b_o, s_fin = _gla_branch(p_all, f_all, prm["wa2"], prm["ba"], prm["gn"], zero_state,
                                 n_seq=bp, seq_len=seq, decode=False)
        q_aug, k_aug, k_n, v_n, lf = _fox_prep(p_all, f_all, prm["qn"], prm["kn"], prm["bf"], seq_len=seq)
        c_o = _fox_flash(q_aug, k_aug, p_all, n_seq=bp, seq_len=seq)
        outs["kp"].append(jnp.swapaxes(k_n, 1, 2))
        outs["vp"].append(jnp.swapaxes(v_n, 1, 2))
        outs["lp"].append(lf[:, :C_HEADS].reshape(bp, seq, C_HEADS))
        outs["gp"].append(_state_from_kernel(s_fin))
        a_u = p_all[:, OFF_AU:OFF_AU + A_WIDTH].reshape(bp, seq, A_WIDTH)
        outs["pp"].append(a_u[:, seq - A_BUF:].astype(F32))

        pool_state = jnp.pad(state_pool[l], ((0, 0), (1, 0), (0, 0)))
        a_s = _pool_branch(p_s, prm["pool_w"], prm["pool_scale"], seq_len=dec_seq, state=pool_state)
        b_s, s_fin_s = _gla_branch(p_s, f_s, prm["wa2"], prm["ba"], prm["gn"], _state_to_kernel(state_gla[l]),
                                   n_seq=n_seq, seq_len=dec_seq, decode=True)
        q_aug_s, _, k_ns, v_ns, lf_s = _fox_prep(p_s, f_s, prm["qn"], prm["kn"], prm["bf"], seq_len=dec_seq)
        lfn_t = jnp.swapaxes(lf_s.reshape(n_seq, dec_seq, LANES)[:, :, :SUBLANES], 1, 2)
        lfn_t = jnp.pad(lfn_t, ((0, 0), (0, 0), (0, LANES - dec_seq)))
        ck_past, ck_new = _decode_bias(page_table, logf_hp, lfn_t, layer=l)
        c_s = _fox_decode(page_table, q_aug_s, cache_k_hm, cache_v_hm, ck_past, ck_new, k_ns, v_ns, p_s,
                          layer=l, dec_seq=dec_seq)
        yp, ys = _merge(yp, a_o, b_o, c_o, p_all, ys, a_s, b_s, c_s, p_s,
                        prm["b_merge"], prm["wa"], prm["wb"], prm["wc"], prm["wo"])
        outs["ks"].append(jnp.swapaxes(k_ns, 1, 2))
        outs["vs"].append(jnp.swapaxes(v_ns, 1, 2))
        outs["ls"].append(lf_s[:, :C_HEADS].reshape(n_seq, dec_seq, C_HEADS))
        outs["gs"].append(_state_from_kernel(s_fin_s))
        a_us = p_s[:, OFF_AU:OFF_AU + A_WIDTH].reshape(n_seq, dec_seq, A_WIDTH).astype(F32)
        outs["ps"].append(jnp.concatenate([state_pool[l], a_us], axis=1)[:, -A_BUF:])

    st = lambda k: jnp.stack(outs[k])
    return (yp.reshape(bp, seq, d), ys.reshape(n_seq, dec_seq, d),
            st("kp"), st("vp"), st("lp"), st("gp"), st("pp"),
            st("ks"), st("vs"), st("ls"), st("gs"), st("ps"))
```

```python
import functools
import math

import jax
import jax.numpy as jnp
from jax import lax
from jax.experimental import pallas as pl
from jax.experimental.pallas import tpu as pltpu

F32 = jnp.float32
BF16 = jnp.bfloat16
HIGHEST = lax.Precision.HIGHEST

D_MODEL = 2048
A_WIDTH = 512
A_WINDOWS = (2, 4, 8, 16)
A_GROUPS = 4
A_GROUP_DIM = 128
A_BUF = 15
B_HEADS = 4
B_DK = 96
B_DV = 192
B_GATE_RANK = 16
B_GATE_TEMP = 16.0
B_CHUNK = 32
C_HEADS = 6
C_HEAD_DIM = 128
C_WIDTH = 768
N_BRANCH = 3
EPS = 1e-6
PAGE_SIZE = 128
SPLIT_SIZES = (512, 512, 384, 384, 768, 768, 16, 768, 768, 768, 768, 6, 6144)

LANES = 128
SUBLANES = 8
VMEM_LIMIT_BYTES = 56 * 1024 * 1024

B_DKP = 128
B_DVP = 256
GLA_SPAN = 128

OFF_MG = 0
OFF_CG = 6144
OFF_CQ = OFF_CG + C_WIDTH
OFF_CK = OFF_CQ + C_WIDTH
OFF_CV = OFF_CK + C_WIDTH
OFF_AU = OFF_CV + C_WIDTH
OFF_AG = OFF_AU + A_WIDTH
OFF_BQ = OFF_AG + A_WIDTH
OFF_BK = OFF_BQ + B_HEADS * B_DKP
OFF_BV = OFF_BK + B_HEADS * B_DKP
OFF_BG = OFF_BV + B_HEADS * B_DVP
OFF_SM = OFF_BG + B_HEADS * B_DVP
SM_WIDTH = 2 * LANES
PROJ_TN = 1536
NP = 13824
assert OFF_SM + SM_WIDTH <= NP and NP % PROJ_TN == 0
SM_TILE = OFF_SM // PROJ_TN
SM_LOCAL = OFF_SM - SM_TILE * PROJ_TN
assert SM_LOCAL + SM_WIDTH <= PROJ_TN


def _pick_tile(n, pref, align=SUBLANES):
    if n <= pref:
        return n
    t = (pref // align) * align
    while t > align and n % t:
        t -= align
    assert n % t == 0, (n, pref)
    return t


def _cparams(*sem):
    return pltpu.CompilerParams(dimension_semantics=sem, vmem_limit_bytes=VMEM_LIMIT_BYTES)


def _log_sigmoid(x):
    return jnp.minimum(x, 0.0) - jnp.log1p(jnp.exp(-jnp.abs(x)))


def _silu(x):
    return x * jax.nn.sigmoid(x)


def _iota(shape, dim):
    return lax.broadcasted_iota(jnp.int32, shape, dim)


def _inproj_kernel(x_ref, xs_ref, g_ref, w_ref, p_ref, f_ref, ps_ref, fs_ref, h_scr, hs_scr, *, nj, rchunk):
    i = pl.program_id(0)
    j = pl.program_id(1)
    last_i = pl.num_programs(0) - 1

    def rms(x):
        r = lax.rsqrt(jnp.mean(x * x, axis=-1, keepdims=True) + EPS)
        return ((x * r) * g_ref[...]).astype(BF16)

    @pl.when(j == 0)
    def _():
        def body(c, carry):
            r0 = pl.multiple_of(c * rchunk, rchunk)
            h_scr[pl.ds(r0, rchunk), :] = rms(x_ref[pl.ds(r0, rchunk), :])
            return carry

        lax.fori_loop(0, x_ref.shape[0] // rchunk, body, 0)

        @pl.when(i == last_i)
        def _():
            hs_scr[...] = rms(xs_ref[...])

    dims = (((1,), (1,)), ((), ()))
    acc = lax.dot_general(h_scr[...], w_ref[...], dims, preferred_element_type=F32)
    p_ref[...] = acc.astype(p_ref.dtype)

    @pl.when(j == nj - 1)
    def _():
        f_ref[...] = acc[:, SM_LOCAL:SM_LOCAL + SM_WIDTH]

    @pl.when(i == last_i)
    def _():
        acc_s = lax.dot_general(hs_scr[...], w_ref[...], dims, preferred_element_type=F32)
        ps_ref[...] = acc_s

        @pl.when(j == nj - 1)
        def _():
            fs_ref[...] = acc_s[:, SM_LOCAL:SM_LOCAL + SM_WIDTH]


def _inproj(x2, xs2, norm_g, w_packed, layer):
    rows = x2.shape[0]
    rows_s = xs2.shape[0]
    tm = _pick_tile(rows, 1024)
    ni = rows // tm
    nj = NP // PROJ_TN
    assert SM_TILE == nj - 1
    rchunk = _pick_tile(tm, 128)
    return pl.pallas_call(
        functools.partial(_inproj_kernel, nj=nj, rchunk=rchunk),
        grid=(ni, nj),
        in_specs=[
            pl.BlockSpec((tm, D_MODEL), lambda i, j: (i, 0)),
            pl.BlockSpec((rows_s, D_MODEL), lambda i, j: (0, 0)),
            pl.BlockSpec((1, D_MODEL), lambda i, j: (0, 0)),
            pl.BlockSpec((None, PROJ_TN, D_MODEL), lambda i, j: (layer, j, 0)),
        ],
        out_specs=[
            pl.BlockSpec((tm, PROJ_TN), lambda i, j: (i, j)),
            pl.BlockSpec((tm, SM_WIDTH), lambda i, j: (i, 0)),
            pl.BlockSpec((rows_s, PROJ_TN), lambda i, j: (0, jnp.where(i == ni - 1, j, 0))),
            pl.BlockSpec((rows_s, SM_WIDTH), lambda i, j: (0, 0)),
        ],
        out_shape=[
            jax.ShapeDtypeStruct((rows, NP), BF16),
            jax.ShapeDtypeStruct((rows, SM_WIDTH), F32),
            jax.ShapeDtypeStruct((rows_s, NP), F32),
            jax.ShapeDtypeStruct((rows_s, SM_WIDTH), F32),
        ],
        scratch_shapes=[pltpu.VMEM((tm, D_MODEL), BF16), pltpu.VMEM((rows_s, D_MODEL), BF16)],
        compiler_params=_cparams("arbitrary", "arbitrary"),
        name="inproj",
    )(x2, xs2, norm_g.reshape(1, D_MODEL), w_packed)


def _pool_kernel(u_ref, g_ref, halo_ref, pw_ref, sc_ref, o_ref, *, tiles_per_seq, decode):
    i = pl.program_id(0)
    tr = u_ref.shape[0]
    hr = halo_ref.shape[0]
    mm_dt = F32 if decode else BF16
    u = u_ref[...].astype(F32)
    halo = halo_ref[...].astype(F32)
    if not decode:
        halo = jnp.where(i % tiles_per_seq == 0, 0.0, halo)
    ext = jnp.concatenate([halo, u], axis=0).astype(mm_dt)
    r = _iota((tr, hr + tr), 0)
    c = _iota((tr, hr + tr), 1) - hr
    rowpos = _iota((tr, 1), 0)
    pos = ((i % tiles_per_seq) * tr + rowpos).astype(F32)
    outs = []
    for g, w in enumerate(A_WINDOWS):
        lo, hi = g * A_GROUP_DIM, (g + 1) * A_GROUP_DIM
        band = ((c >= r - (w - 1)) & (c <= r)).astype(mm_dt)
        if decode:
            win_sum = jnp.dot(band, ext[:, lo:hi], preferred_element_type=F32, precision=HIGHEST)
            count = float(w)
        else:
            win_sum = jnp.dot(band, ext[:, lo:hi], preferred_element_type=F32)
            count = jnp.minimum(pos + 1.0, float(w))
        pooled = win_sum / count - u[:, lo:hi]
        outs.append(jnp.dot(pooled.astype(mm_dt), pw_ref[g].astype(mm_dt), preferred_element_type=F32))
    mixed = jnp.concatenate(outs, axis=1) * sc_ref[...]
    o_ref[...] = (mixed * _silu(g_ref[...].astype(F32))).astype(o_ref.dtype)


def _pool_branch(p_all, pool_w, pool_scale, *, seq_len, state=None):
    rows = p_all.shape[0]
    decode = state is not None
    wdt = F32 if decode else BF16
    if decode:
        tr, tps = seq_len, 1
        halo_spec = pl.BlockSpec((None, 16, A_WIDTH), lambda i: (i, 0, 0))
        halo_arr = state
    else:
        tr = _pick_tile(seq_len, 256)
        tps = seq_len // tr
        hb = LANES
        assert tr % hb == 0
        halo_spec = pl.BlockSpec((hb, A_WIDTH), lambda i: (jnp.maximum(i * (tr // hb) - 1, 0), OFF_AU // A_WIDTH))
        halo_arr = p_all
    return pl.pallas_call(
        functools.partial(_pool_kernel, tiles_per_seq=tps, decode=decode),
        grid=(rows // tr,),
        in_specs=[
            pl.BlockSpec((tr, A_WIDTH), lambda i: (i, OFF_AU // A_WIDTH)),
            pl.BlockSpec((tr, A_WIDTH), lambda i: (i, OFF_AG // A_WIDTH)),
            halo_spec,
            pl.BlockSpec((A_GROUPS, A_GROUP_DIM, A_GROUP_DIM), lambda i: (0, 0, 0)),
            pl.BlockSpec((1, A_WIDTH), lambda i: (0, 0)),
        ],
        out_specs=pl.BlockSpec((tr, A_WIDTH), lambda i: (i, 0)),
        out_shape=jax.ShapeDtypeStruct((rows, A_WIDTH), p_all.dtype),
        compiler_params=_cparams("arbitrary"),
        name="pool_decode" if decode else "pool_prompt",
    )(p_all, p_all, halo_arr, pool_w.astype(wdt), pool_scale.reshape(1, A_WIDTH))


def _gla_kernel(q_ref, k_ref, v_ref, g_ref, r_ref, wa_ref, ba_ref, gn_ref, s0_ref,
                o_ref, so_ref, s_scr, *, chunk, span, decode):
    tb = pl.program_id(1)
    nb = pl.num_programs(1)
    tbk = q_ref.shape[0]
    mm_dt = F32 if decode else BF16

    @pl.when(tb == 0)
    def _():
        s_scr[...] = s0_ref[...]

    logit = jnp.dot(r_ref[...], wa_ref[...], preferred_element_type=F32, precision=HIGHEST) + ba_ref[...]
    log_a = _log_sigmoid(logit) / B_GATE_TEMP
    nsub = span // chunk
    tri = (_iota((span, span), 0) >= _iota((span, span), 1))
    scale = B_DK ** -0.5
    kw = log_a.shape[1]
    nt_dims = (((1,), (1,)), ((), ()))
    if not decode:
        la_hi = log_a.astype(BF16)
        la_r1 = log_a - la_hi.astype(F32)
        la_mid = la_r1.astype(BF16)
        la_lo = (la_r1 - la_mid.astype(F32)).astype(BF16)
        la_terms = jnp.concatenate([la_hi, la_mid, la_lo], axis=1)
        tri_b = tri.astype(BF16)
    zero_blk = jnp.zeros((chunk, B_DKP), mm_dt)
    for c in range(tbk // span):
        rs = slice(c * span, (c + 1) * span)
        if decode:
            b = jnp.dot(tri.astype(F32), log_a[rs], preferred_element_type=F32, precision=HIGHEST)
        else:
            b3 = jnp.dot(tri_b, la_terms[rs], preferred_element_type=F32)
            b = (b3[:, :kw] + b3[:, kw:2 * kw]) + b3[:, 2 * kw:]
        ends = [b[(i + 1) * chunk - 1:(i + 1) * chunk, :] for i in range(nsub)]
        starts = [jnp.zeros_like(ends[0])] + ends[:-1]
        b_last = ends[-1]
        rows_of = lambda vecs: jnp.concatenate([jnp.broadcast_to(v, (chunk, kw)) for v in vecs], axis=0)
        r_rows = rows_of(starts)
        q_in = (q_ref[rs, :].astype(F32) * scale) * jnp.exp(b - r_rows)
        k_in = k_ref[rs, :].astype(F32) * jnp.exp(r_rows - b)
        q_t = q_in * rows_of([jnp.exp(v) for v in starts])
        k_end = k_in * rows_of([jnp.exp(b_last - v) for v in starts])
        decay = jnp.exp(b_last)
        for h in range(B_HEADS):
            ks = slice(h * B_DKP, (h + 1) * B_DKP)
            vs = slice(h * B_DVP, (h + 1) * B_DVP)
            q_blocks, k_blocks = [], []
            for i in range(nsub):
                qi = q_in[i * chunk:(i + 1) * chunk, ks]
                kj = k_in[i * chunk:(i + 1) * chunk, ks].astype(mm_dt)
                q_blocks.append(jnp.concatenate(
                    [(qi * jnp.exp(starts[i][:, ks] - starts[j][:, ks])).astype(mm_dt) for j in range(i)]
                    + [qi.astype(mm_dt)] + [zero_blk] * (nsub - 1 - i), axis=1))
                k_blocks.append(jnp.concatenate([kj if j == i else zero_blk for j in range(nsub)], axis=1))
            att = lax.dot_general(jnp.concatenate(q_blocks, axis=0), jnp.concatenate(k_blocks, axis=0), nt_dims,
                                  preferred_element_type=F32)
            att = jnp.where(tri, att, 0.0)
            vh = v_ref[rs, vs].astype(mm_dt)
            st = s_scr[h]
            o = jnp.dot(att.astype(mm_dt), vh, preferred_element_type=F32)
            o = o + lax.dot_general(q_t[:, ks].astype(mm_dt), st.astype(mm_dt), nt_dims,
                                    preferred_element_type=F32)
            kv_t = lax.dot_general(vh, k_end[:, ks].astype(mm_dt), (((0,), (0,)), ((), ())),
                                   preferred_element_type=F32)
            s_scr[h] = st * decay[:, ks] + kv_t
            r = lax.rsqrt(jnp.sum(o * o, axis=-1, keepdims=True) * (1.0 / B_DV) + EPS)
            on = (o * r) * gn_ref[:, vs]
            o_ref[rs, vs] = (on * _silu(g_ref[rs, vs].astype(F32))).astype(o_ref.dtype)

    @pl.when(tb == nb - 1)
    def _():
        so_ref[...] = s_scr[...]


def _gla_branch(p_all, f_all, wa_p, ba_p, gn_p, s0_t, *, n_seq, seq_len, decode):
    rows = p_all.shape[0]
    chunk = math.gcd(seq_len, B_CHUNK)
    tbk = _pick_tile(seq_len, 256)
    nb = seq_len // tbk
    span = GLA_SPAN if (not decode and tbk % GLA_SPAN == 0 and GLA_SPAN % chunk == 0) else chunk
    kw = B_HEADS * B_DKP
    vw = B_HEADS * B_DVP
    return pl.pallas_call(
        functools.partial(_gla_kernel, chunk=chunk, span=span, decode=decode),
        grid=(n_seq, nb),
        in_specs=[
            pl.BlockSpec((tbk, kw), lambda s, t: (s * nb + t, OFF_BQ // kw)),
            pl.BlockSpec((tbk, kw), lambda s, t: (s * nb + t, OFF_BK // kw)),
            pl.BlockSpec((tbk, vw), lambda s, t: (s * nb + t, OFF_BV // vw)),
            pl.BlockSpec((tbk, vw), lambda s, t: (s * nb + t, OFF_BG // vw)),
            pl.BlockSpec((tbk, LANES), lambda s, t: (s * nb + t, 0)),
            pl.BlockSpec((LANES, kw), lambda s, t: (0, 0)),
            pl.BlockSpec((1, kw), lambda s, t: (0, 0)),
            pl.BlockSpec((1, vw), lambda s, t: (0, 0)),
            pl.BlockSpec((None, B_HEADS, B_DVP, B_DKP), lambda s, t: (s, 0, 0, 0)),
        ],
        out_specs=[
            pl.BlockSpec((tbk, vw), lambda s, t: (s * nb + t, 0)),
            pl.BlockSpec((None, B_HEADS, B_DVP, B_DKP), lambda s, t: (s, 0, 0, 0)),
        ],
        out_shape=[
            jax.ShapeDtypeStruct((rows, vw), p_all.dtype),
            jax.ShapeDtypeStruct((n_seq, B_HEADS, B_DVP, B_DKP), F32),
        ],
        scratch_shapes=[pltpu.VMEM((B_HEADS, B_DVP, B_DKP), F32)],
        compiler_params=_cparams("arbitrary", "arbitrary"),
        name="gla_decode" if decode else "gla_prompt",
    )(p_all, p_all, p_all, p_all, f_all, wa_p, ba_p, gn_p, s0_t)


def _foxprep_kernel(cq_ref, ck_ref, cv_ref, cf_ref, qn_ref, kn_ref, bf_ref,
                    qa_ref, ka_ref, ko_ref, vo_ref, lf_ref, carry_scr, *, tiles_per_seq):
    i = pl.program_id(0)
    tr = cq_ref.shape[0]
    hd = C_HEAD_DIM
    lane = _iota((tr, LANES), 1)

    lf = jnp.where(lane < C_HEADS, _log_sigmoid(cf_ref[...] + bf_ref[...]), 0.0)
    lf_ref[...] = lf

    @pl.when(i % tiles_per_seq == 0)
    def _():
        carry_scr[...] = jnp.zeros_like(carry_scr)

    tri = (_iota((tr, tr), 0) >= _iota((tr, tr), 1)).astype(F32)
    cum = jnp.dot(tri, lf, preferred_element_type=F32, precision=HIGHEST) + carry_scr[...]
    carry_scr[...] = cum[tr - 1:tr, :]

    def qk_norm(x, g_ref):
        r = lax.rsqrt(jnp.mean(x * x, axis=-1, keepdims=True) + EPS)
        return (x * r) * g_ref[...]

    one = jnp.ones((tr, LANES), F32)
    zero = jnp.zeros((tr, LANES), F32)
    for h in range(C_HEADS):
        cs = slice(h * hd, (h + 1) * hd)
        ch = jnp.sum(jnp.where(lane == h, cum, 0.0), axis=-1, keepdims=True)
        hi = ch.astype(BF16).astype(F32)
        r1 = ch - hi
        mid = r1.astype(BF16).astype(F32)
        lo = r1 - mid
        qn = qk_norm(cq_ref[:, cs].astype(F32), qn_ref)
        kn = qk_norm(ck_ref[:, cs].astype(F32), kn_ref)
        aux_q = jnp.where(lane == 0, hi, jnp.where(lane == 1, mid, jnp.where(lane == 2, lo,
                          jnp.where(lane < 6, one, zero))))
        aux_k = jnp.where(lane < 3, one, jnp.where(lane == 3, -hi, jnp.where(lane == 4, -mid,
                          jnp.where(lane == 5, -lo, zero))))
        a0 = 2 * h * hd
        qa_ref[:, a0:a0 + hd] = (qn * (hd ** -0.5)).astype(qa_ref.dtype)
        qa_ref[:, a0 + hd:a0 + 2 * hd] = aux_q.astype(qa_ref.dtype)
        ka_ref[:, a0:a0 + hd] = kn.astype(ka_ref.dtype)
        ka_ref[:, a0 + hd:a0 + 2 * hd] = aux_k.astype(ka_ref.dtype)
        ko_ref[h] = kn
        vo_ref[h] = cv_ref[:, cs].astype(F32)


def _fox_prep(p_all, f_all, fox_q_norm, fox_k_norm, bf_p, *, seq_len):
    rows = p_all.shape[0]
    tr = _pick_tile(seq_len, 256)
    tps = seq_len // tr
    hd = C_HEAD_DIM
    cw = C_WIDTH
    aw = C_HEADS * 2 * hd
    return pl.pallas_call(
        functools.partial(_foxprep_kernel, tiles_per_seq=tps),
        grid=(rows // tr,),
        in_specs=[
            pl.BlockSpec((tr, cw), lambda i: (i, OFF_CQ // cw)),
            pl.BlockSpec((tr, cw), lambda i: (i, OFF_CK // cw)),
            pl.BlockSpec((tr, cw), lambda i: (i, OFF_CV // cw)),
            pl.BlockSpec((tr, LANES), lambda i: (i, 1)),
            pl.BlockSpec((1, hd), lambda i: (0, 0)),
            pl.BlockSpec((1, hd), lambda i: (0, 0)),
            pl.BlockSpec((1, LANES), lambda i: (0, 0)),
        ],
        out_specs=[
            pl.BlockSpec((tr, aw), lambda i: (i, 0)),
            pl.BlockSpec((tr, aw), lambda i: (i, 0)),
            pl.BlockSpec((None, C_HEADS, tr, hd), lambda i: (i // tps, 0, i % tps, 0)),
            pl.BlockSpec((None, C_HEADS, tr, hd), lambda i: (i // tps, 0, i % tps, 0)),
            pl.BlockSpec((tr, LANES), lambda i: (i, 0)),
        ],
        out_shape=[
            jax.ShapeDtypeStruct((rows, aw), p_all.dtype),
            jax.ShapeDtypeStruct((rows, aw), p_all.dtype),
            jax.ShapeDtypeStruct((rows // seq_len, C_HEADS, seq_len, hd), F32),
            jax.ShapeDtypeStruct((rows // seq_len, C_HEADS, seq_len, hd), F32),
            jax.ShapeDtypeStruct((rows, LANES), F32),
        ],
        scratch_shapes=[pltpu.VMEM((1, LANES), F32)],
        compiler_params=_cparams("arbitrary"),
        name="fox_prep",
    )(p_all, p_all, p_all, f_all, fox_q_norm.reshape(1, hd), fox_k_norm.reshape(1, hd), bf_p)


def _flash_kernel(q_ref, k_ref, v_ref, g_ref, o_ref, m_scr, l_scr, acc_scr, *, rs):
    i = pl.program_id(2)
    t = q_ref.shape[0]
    hd = C_HEAD_DIM
    m_scr[...] = jnp.full_like(m_scr, -jnp.inf)
    l_scr[...] = jnp.zeros_like(l_scr)
    acc_scr[...] = jnp.zeros_like(acc_scr)
    ones = jnp.ones((t, LANES), BF16)

    def step(j, diagonal):
        k0 = pl.multiple_of(j * t, t)
        kb = k_ref[pl.ds(k0, t), :]
        vb = jnp.concatenate([v_ref[pl.ds(k0, t), :], ones], axis=1)
        for r in range(t // rs):
            rows = slice(r * rs, (r + 1) * rs)
            nk = (r + 1) * rs if diagonal else t
            s = lax.dot_general(q_ref[rows, :], kb[:nk], (((1,), (1,)), ((), ())), preferred_element_type=F32)
            if diagonal:
                s = jnp.where(_iota((rs, nk), 0) + r * rs >= _iota((rs, nk), 1), s, -jnp.inf)
            m_prev = m_scr[rows, :]
            m_new = jnp.maximum(m_prev, jnp.max(s, axis=-1, keepdims=True))
            alpha = jnp.exp(m_prev - m_new)
            p = jnp.exp(s - jnp.concatenate([m_new] * (nk // LANES), axis=1))
            pv = jnp.dot(p.astype(BF16), vb[:nk], preferred_element_type=F32)
            l_scr[rows, :] = alpha * l_scr[rows, :] + pv[:, hd:]
            acc_scr[rows, :] = alpha * acc_scr[rows, :] + pv[:, :hd]
            m_scr[rows, :] = m_new

    def pair(jj, carry):
        step(2 * jj, False)
        step(2 * jj + 1, False)
        return carry

    lax.fori_loop(0, i // 2, pair, 0)

    @pl.when(i % 2 == 1)
    def _():
        step(i - 1, False)

    step(i, True)
    out = acc_scr[...] / l_scr[...]
    o_ref[...] = (out * _silu(g_ref[...].astype(F32))).astype(BF16)


def _fox_flash(q_aug, k_aug, p_all, *, n_seq, seq_len):
    rows = q_aug.shape[0]
    t = _pick_tile(seq_len, 1024, LANES)
    rs = _pick_tile(t, 256, LANES)
    nt = seq_len // t
    hd = C_HEAD_DIM
    aw = 2 * hd
    return pl.pallas_call(
        functools.partial(_flash_kernel, rs=rs),
        grid=(n_seq, C_HEADS, nt),
        in_specs=[
            pl.BlockSpec((t, aw), lambda b, h, i: (b * nt + i, h)),
            pl.BlockSpec((seq_len, aw), lambda b, h, i: (b, h)),
            pl.BlockSpec((seq_len, hd), lambda b, h, i: (b, OFF_CV // hd + h)),
            pl.BlockSpec((t, hd), lambda b, h, i: (b * nt + i, OFF_CG // hd + h)),
        ],
        out_specs=pl.BlockSpec((t, hd), lambda b, h, i: (b * nt + i, h)),
        out_shape=jax.ShapeDtypeStruct((rows, C_WIDTH), BF16),
        scratch_shapes=[pltpu.VMEM((t, LANES), F32), pltpu.VMEM((t, LANES), F32), pltpu.VMEM((t, hd), F32)],
        compiler_params=_cparams("parallel", "parallel", "arbitrary"),
        name="fox_flash",
    )(q_aug, k_aug, p_all, p_all)


def _decbias_kernel(pt_ref, lf_ref, lfn_ref, ck_ref, ckn_ref, g_scr):
    s = pl.program_id(0)
    npg = g_scr.shape[1]

    def gather(p, carry):
        page = pt_ref[s, p]
        for h in range(C_HEADS):
            g_scr[h, pl.ds(p, 1), :] = lf_ref[h, pl.ds(page, 1), :]
        return carry

    lax.fori_loop(0, npg, gather, 0)
    upper = (_iota((LANES, LANES), 0) <= _iota((LANES, LANES), 1)).astype(F32)
    earlier = (_iota((npg, npg), 1) < _iota((npg, npg), 0)).astype(F32)
    lfn_cum = jnp.dot(lfn_ref[...], upper, preferred_element_type=F32, precision=HIGHEST)
    ckn_ref[...] = jnp.zeros_like(ckn_ref)
    for h in range(C_HEADS):
        cum = jnp.dot(g_scr[h], upper, preferred_element_type=F32, precision=HIGHEST)
        tot = jnp.broadcast_to(cum[:, LANES - 1:LANES], cum.shape)
        before = jnp.dot(earlier, tot, preferred_element_type=F32, precision=HIGHEST)
        ck_ref[h] = cum + before
        ckn_ref[h:h + 1, :] = lfn_cum[h:h + 1, :] + (before[npg - 1:npg, :] + tot[npg - 1:npg, :])


def _decode_bias(page_table, logf_hp, lfn_t, *, layer):
    n_seq, npg = page_table.shape
    n_pool = logf_hp.shape[2]
    return pl.pallas_call(
        _decbias_kernel,
        grid_spec=pltpu.PrefetchScalarGridSpec(
            num_scalar_prefetch=1,
            grid=(n_seq,),
            in_specs=[
                pl.BlockSpec((None, C_HEADS, n_pool, PAGE_SIZE), lambda s, pt: (layer, 0, 0, 0)),
                pl.BlockSpec((None, SUBLANES, LANES), lambda s, pt: (s, 0, 0)),
            ],
            out_specs=[
                pl.BlockSpec((None, C_HEADS, npg, PAGE_SIZE), lambda s, pt: (s, 0, 0, 0)),
                pl.BlockSpec((None, SUBLANES, LANES), lambda s, pt: (s, 0, 0)),
            ],
            scratch_shapes=[pltpu.VMEM((C_HEADS, npg, PAGE_SIZE), F32)],
        ),
        out_shape=[
            jax.ShapeDtypeStruct((n_seq, C_HEADS, npg, PAGE_SIZE), F32),
            jax.ShapeDtypeStruct((n_seq, SUBLANES, LANES), F32),
        ],
        compiler_params=_cparams("arbitrary"),
        name="decode_bias",
    )(page_table, logf_hp, lfn_t)


DEC_QROWS = 16


def _decode_kernel(pt_ref, qa_ref, *refs, gpp, dec_seq):
    del pt_ref
    k_refs = refs[:gpp]
    v_refs = refs[gpp:2 * gpp]
    ck_ref, ckn_ref, kn_ref, vn_ref, g_ref, o_ref, q_scr, m_scr, l_scr, acc_scr = refs[2 * gpp:]
    t = pl.program_id(1)
    nt = pl.num_programs(1)
    nq = dec_seq
    qp = DEC_QROWS
    nrow = C_HEADS * qp
    hd = C_HEAD_DIM
    row = _iota((nrow, LANES), 0)
    lane = _iota((nrow, LANES), 1)
    qi = jnp.bitwise_and(row, qp - 1)

    @pl.when(t == 0)
    def _():
        qa = qa_ref[...].astype(F32)
        zero = jnp.zeros((qp - nq, hd), F32)
        q_scr[...] = jnp.concatenate(
            [jnp.concatenate([qa[:, h * 2 * hd:h * 2 * hd + hd], zero], axis=0) for h in range(C_HEADS)],
            axis=0).astype(BF16)
        m_scr[...] = jnp.full_like(m_scr, -jnp.inf)
        l_scr[...] = jnp.zeros_like(l_scr)
        acc_scr[...] = jnp.zeros_like(acc_scr)

    def head_rows(get_row):
        return jnp.concatenate(
            [jnp.broadcast_to(get_row(h), (qp, get_row(h).shape[1])) for h in range(C_HEADS)], axis=0)

    ckn_rows = head_rows(lambda h: ckn_ref[h:h + 1, :])
    c_q = jnp.sum(jnp.where(lane == qi, ckn_rows, 0.0), axis=-1, keepdims=True)

    def attend(k_of, v_of, ck_rows, mask):
        s = jnp.concatenate(
            [lax.dot_general(q_scr[h * qp:(h + 1) * qp, :], k_of(h), (((1,), (1,)), ((), ())),
                             preferred_element_type=F32) for h in range(C_HEADS)], axis=0)
        s = s + (c_q - ck_rows)
        if mask is not None:
            s = jnp.where(mask, s, -jnp.inf)
        m_prev = m_scr[...]
        m_new = jnp.maximum(m_prev, jnp.max(s, axis=-1, keepdims=True))
        alpha = jnp.exp(m_prev - m_new)
        pr = jnp.exp(s - m_new)
        l_scr[...] = alpha * l_scr[...] + jnp.sum(pr, axis=-1, keepdims=True)
        prb = pr.astype(BF16)
        o = jnp.concatenate(
            [jnp.dot(prb[h * qp:(h + 1) * qp, :], v_of(h), preferred_element_type=F32) for h in range(C_HEADS)],
            axis=0)
        acc_scr[...] = alpha * acc_scr[...] + o
        m_scr[...] = m_new

    attend(lambda h: jnp.concatenate([k_refs[g][h] for g in range(gpp)], axis=0).astype(BF16),
           lambda h: jnp.concatenate([v_refs[g][h] for g in range(gpp)], axis=0).astype(BF16),
           head_rows(lambda h: jnp.concatenate([ck_ref[h, g:g + 1, :] for g in range(gpp)], axis=1)), None)

    @pl.when(t == nt - 1)
    def _():
        pad = jnp.zeros((PAGE_SIZE - nq, hd), F32)
        attend(lambda h: jnp.concatenate([kn_ref[h], pad], axis=0).astype(BF16),
               lambda h: jnp.concatenate([vn_ref[h], pad], axis=0).astype(BF16),
               ckn_rows, lane <= qi)
        out = acc_scr[...] / l_scr[...]
        gate = _silu(g_ref[...].astype(F32))
        for h in range(C_HEADS):
            cs = slice(h * hd, (h + 1) * hd)
            o_ref[:, cs] = (out[h * qp:h * qp + nq, :] * gate[:, cs]).astype(o_ref.dtype)


def _fox_decode(page_table, q_aug, cache_k_hm, cache_v_hm, ck_past, ck_new, k_new, v_new, p_all, *, layer, dec_seq):
    n_seq, npg = page_table.shape
    rows = q_aug.shape[0]
    nrow = C_HEADS * DEC_QROWS
    gpp = _pick_tile(npg, 16, SUBLANES)

    def page_spec(g):
        return pl.BlockSpec((None, None, C_HEADS, PAGE_SIZE, C_HEAD_DIM),
                            lambda s, t, pt: (layer, pt[s, t * gpp + g], 0, 0, 0))

    return pl.pallas_call(
        functools.partial(_decode_kernel, gpp=gpp, dec_seq=dec_seq),
        grid_spec=pltpu.PrefetchScalarGridSpec(
            num_scalar_prefetch=1,
            grid=(n_seq, npg // gpp),
            in_specs=[pl.BlockSpec((dec_seq, C_HEADS * 2 * C_HEAD_DIM), lambda s, t, pt: (s, 0))]
            + [page_spec(g) for g in range(gpp)]
            + [page_spec(g) for g in range(gpp)]
            + [
                pl.BlockSpec((None, C_HEADS, gpp, PAGE_SIZE), lambda s, t, pt: (s, 0, t, 0)),
                pl.BlockSpec((None, SUBLANES, LANES), lambda s, t, pt: (s, 0, 0)),
                pl.BlockSpec((None, C_HEADS, dec_seq, C_HEAD_DIM), lambda s, t, pt: (s, 0, 0, 0)),
                pl.BlockSpec((None, C_HEADS, dec_seq, C_HEAD_DIM), lambda s, t, pt: (s, 0, 0, 0)),
                pl.BlockSpec((dec_seq, C_WIDTH), lambda s, t, pt: (s, OFF_CG // C_WIDTH)),
            ],
            out_specs=pl.BlockSpec((dec_seq, C_WIDTH), lambda s, t, pt: (s, 0)),
            scratch_shapes=[
                pltpu.VMEM((nrow, C_HEAD_DIM), BF16),
                pltpu.VMEM((nrow, 1), F32),
                pltpu.VMEM((nrow, 1), F32),
                pltpu.VMEM((nrow, C_HEAD_DIM), F32),
            ],
        ),
        out_shape=jax.ShapeDtypeStruct((rows, C_WIDTH), p_all.dtype),
        compiler_params=_cparams("arbitrary", "arbitrary"),
        name="fox_decode",
    )(page_table, q_aug, *([cache_k_hm] * gpp), *([cache_v_hm] * gpp), ck_past, ck_new, k_new, v_new, p_all)


def _merge_kernel(x_ref, a_ref, b_ref, c_ref, mg0_ref, mg1_ref, mg2_ref,
                  xs_ref, as_ref, bs_ref, cs_ref, mgs0_ref, mgs1_ref, mgs2_ref,
                  bm_ref, wa_ref, wb_ref, wc_ref, wo_ref, y_ref, ys_ref):
    d = D_MODEL

    def gated(o_ref, w_ref, mg_ref, k):
        gate = jax.nn.sigmoid(mg_ref[...].astype(F32) + bm_ref[:, k * d:(k + 1) * d])
        return gate * jnp.dot(o_ref[...].astype(BF16), w_ref[...], preferred_element_type=F32)

    def layer_out(xr, ar, br, cr, m0, m1, m2):
        merged = gated(ar, wa_ref, m0, 0) + gated(br, wb_ref, m1, 1) + gated(cr, wc_ref, m2, 2)
        return xr[...] + jnp.dot(merged.astype(BF16), wo_ref[...], preferred_element_type=F32)

    y_ref[...] = layer_out(x_ref, a_ref, b_ref, c_ref, mg0_ref, mg1_ref, mg2_ref)

    @pl.when(pl.program_id(0) == pl.num_programs(0) - 1)
    def _():
        ys_ref[...] = layer_out(xs_ref, as_ref, bs_ref, cs_ref, mgs0_ref, mgs1_ref, mgs2_ref)


def _merge(x2, a_o, b_o, c_o, p_all, xs2, a_s, b_s, c_s, p_s, b_merge, wa, wb, wc, wo):
    rows = x2.shape[0]
    rows_s = xs2.shape[0]
    tm = _pick_tile(rows, 256)
    d = D_MODEL
    bw = B_HEADS * B_DVP

    def const(shape, col=0):
        return pl.BlockSpec(shape, lambda i: (0, col), pipeline_mode=pl.Buffered(1))

    return pl.pallas_call(
        _merge_kernel,
        grid=(rows // tm,),
        in_specs=[
            pl.BlockSpec((tm, d), lambda i: (i, 0)),
            pl.BlockSpec((tm, A_WIDTH), lambda i: (i, 0)),
            pl.BlockSpec((tm, bw), lambda i: (i, 0)),
            pl.BlockSpec((tm, C_WIDTH), lambda i: (i, 0)),
            pl.BlockSpec((tm, d), lambda i: (i, OFF_MG // d + 0)),
            pl.BlockSpec((tm, d), lambda i: (i, OFF_MG // d + 1)),
            pl.BlockSpec((tm, d), lambda i: (i, OFF_MG // d + 2)),
            const((rows_s, d)),
            const((rows_s, A_WIDTH)),
            const((rows_s, bw)),
            const((rows_s, C_WIDTH)),
            const((rows_s, d), OFF_MG // d + 0),
            const((rows_s, d), OFF_MG // d + 1),
            const((rows_s, d), OFF_MG // d + 2),
            const((1, N_BRANCH * d)),
            const((A_WIDTH, d)),
            const((bw, d)),
            const((C_WIDTH, d)),
            const((d, d)),
        ],
        out_specs=[pl.BlockSpec((tm, d), lambda i: (i, 0)), pl.BlockSpec((rows_s, d), lambda i: (0, 0))],
        out_shape=[jax.ShapeDtypeStruct((rows, d), F32), jax.ShapeDtypeStruct((rows_s, d), F32)],
        compiler_params=_cparams("arbitrary"),
        name="merge_out",
    )(x2, a_o, b_o, c_o, p_all, p_all, p_all, xs2, a_s, b_s, c_s, p_s, p_s, p_s,
      b_merge.reshape(1, N_BRANCH * d), wa, wb, wc, wo)


def _pad_heads(x, n_heads, d, dp, axis):
    shp = x.shape
    x = x.reshape(shp[:axis] + (n_heads, d) + shp[axis + 1:])
    pad = [(0, 0)] * x.ndim
    pad[axis + 1] = (0, dp - d)
    x = jnp.pad(x, pad)
    return x.reshape(shp[:axis] + (n_heads * dp,) + shp[axis + 1:])


def _pack_plan():
    src = {}
    acc = 0
    for name, size in zip(("a_u", "a_g", "b_q", "b_k", "b_v", "b_g", "b_r", "c_q", "c_k", "c_v", "c_g", "c_f", "m_g"),
                          SPLIT_SIZES):
        src[name] = acc
        acc += size
    d_in = acc
    segs = [(OFF_MG, "m_g", 1, 6144, 6144), (OFF_CG, "c_g", 1, C_WIDTH, C_WIDTH), (OFF_CQ, "c_q", 1, C_WIDTH, C_WIDTH),
            (OFF_CK, "c_k", 1, C_WIDTH, C_WIDTH), (OFF_CV, "c_v", 1, C_WIDTH, C_WIDTH),
            (OFF_AU, "a_u", 1, A_WIDTH, A_WIDTH), (OFF_AG, "a_g", 1, A_WIDTH, A_WIDTH),
            (OFF_BQ, "b_q", B_HEADS, B_DK, B_DKP), (OFF_BK, "b_k", B_HEADS, B_DK, B_DKP),
            (OFF_BV, "b_v", B_HEADS, B_DV, B_DVP), (OFF_BG, "b_g", B_HEADS, B_DV, B_DVP),
            (OFF_SM, "b_r", 1, B_GATE_RANK, LANES), (OFF_SM + LANES, "c_f", 1, C_HEADS, LANES)]
    starts = [0] * (NP // LANES)
    nvalid = [0] * (NP // LANES)
    for off, name, heads, dreal, dpad in segs:
        for h in range(heads):
            for b in range(dpad // LANES):
                blk = (off + h * dpad) // LANES + b
                n = max(0, min(LANES, dreal - b * LANES))
                s = src[name] + h * dreal + b * LANES
                assert s + LANES <= d_in
                starts[blk], nvalid[blk] = (s if n else 0), n
    return starts, nvalid


PACK_WINDOWS = 4


def _packw_kernel(st_ref, nv_ref, *refs):
    del st_ref
    w_refs, o_ref = refs[:PACK_WINDOWS], refs[PACK_WINDOWS]
    j = pl.program_id(0)
    row = _iota((LANES, D_MODEL), 0)
    for g in range(PACK_WINDOWS):
        live = row < nv_ref[j * PACK_WINDOWS + g]
        for l in range(o_ref.shape[0]):
            o_ref[l, g * LANES:(g + 1) * LANES, :] = jnp.where(live, w_refs[g][:, l, :], 0.0).astype(BF16)


def _pack_w_in(w_in):
    depth = w_in.shape[0]
    w_t = jnp.transpose(w_in, (2, 0, 1))
    starts, nvalid = _pack_plan()
    nblk = NP // LANES
    assert nblk % PACK_WINDOWS == 0

    def window(g):
        return pl.BlockSpec((pl.Element(LANES), pl.Element(depth), pl.Element(D_MODEL)),
                            lambda j, st, nv: (st[j * PACK_WINDOWS + g], 0, 0))

    return pl.pallas_call(
        _packw_kernel,
        grid_spec=pltpu.PrefetchScalarGridSpec(
            num_scalar_prefetch=2,
            grid=(nblk // PACK_WINDOWS,),
            in_specs=[window(g) for g in range(PACK_WINDOWS)],
            out_specs=pl.BlockSpec((depth, PACK_WINDOWS * LANES, D_MODEL), lambda j, st, nv: (0, j, 0)),
        ),
        out_shape=jax.ShapeDtypeStruct((depth, NP, D_MODEL), BF16),
        compiler_params=_cparams("parallel"),
        name="pack_w_in",
    )(jnp.asarray(starts, jnp.int32), jnp.asarray(nvalid, jnp.int32), *([w_t] * PACK_WINDOWS))


def _layer_params(l, norm_g, pool_w, pool_scale, gla_w_a2, gla_b_a, gla_norm, fox_b_f, fox_q_norm,
                  fox_k_norm, w_branch_a, w_branch_b, w_branch_c, b_merge, w_out):
    wa2 = _pad_heads(gla_w_a2[l], B_HEADS, B_DK, B_DKP, 1)
    return dict(
        norm_g=norm_g[l],
        pool_w=pool_w[l],
        pool_scale=pool_scale[l],
        wa2=jnp.pad(wa2, ((0, LANES - B_GATE_RANK), (0, 0))),
        ba=_pad_heads(gla_b_a[l].reshape(1, -1), B_HEADS, B_DK, B_DKP, 1),
        gn=_pad_heads(gla_norm[l].reshape(1, -1), B_HEADS, B_DV, B_DVP, 1),
        bf=jnp.pad(fox_b_f[l].reshape(1, -1), ((0, 0), (0, LANES - C_HEADS))),
        qn=fox_q_norm[l],
        kn=fox_k_norm[l],
        wa=w_branch_a[l].astype(BF16),
        wb=_pad_heads(w_branch_b[l], B_HEADS, B_DV, B_DVP, 0).astype(BF16),
        wc=w_branch_c[l].astype(BF16),
        b_merge=b_merge[l],
        wo=w_out[l].astype(BF16),
    )


def _state_to_kernel(s):
    st = jnp.swapaxes(s, -1, -2)
    return jnp.pad(st, ((0, 0), (0, 0), (0, B_DVP - B_DV), (0, B_DKP - B_DK)))


def _state_from_kernel(st):
    return jnp.swapaxes(st[:, :, :B_DV, :B_DK], -1, -2)


def kernel(x_prompt, x_sample, cache_k, cache_v, cache_logf, state_gla, state_pool, page_table, norm_g, w_in, pool_w, pool_scale, gla_w_a2, gla_b_a, gla_norm, fox_b_f, fox_q_norm, fox_k_norm, w_branch_a, w_branch_b, w_branch_c, b_merge, w_out):
    depth = w_in.shape[0]
    bp, seq, d = x_prompt.shape
    n_seq, dec_seq, _ = x_sample.shape
    assert d == D_MODEL and dec_seq == SUBLANES and seq % LANES == 0
    yp = x_prompt.reshape(bp * seq, d)
    ys = x_sample.reshape(n_seq * dec_seq, d)
    cache_k_hm = jnp.transpose(cache_k, (0, 1, 3, 2, 4))
    cache_v_hm = jnp.transpose(cache_v, (0, 1, 3, 2, 4))
    logf_hp = jnp.transpose(cache_logf, (0, 3, 1, 2))
    zero_state = jnp.zeros((bp, B_HEADS, B_DVP, B_DKP), F32)
    w_packed = _pack_w_in(w_in)
    outs = {k: [] for k in ("kp", "vp", "lp", "gp", "pp", "ks", "vs", "ls", "gs", "ps")}
    for l in range(depth):
        prm = _layer_params(l, norm_g, pool_w, pool_scale, gla_w_a2, gla_b_a, gla_norm, fox_b_f,
                            fox_q_norm, fox_k_norm, w_branch_a, w_branch_b, w_branch_c, b_merge, w_out)

        p_all, f_all, p_s, f_s = _inproj(yp, ys, prm["norm_g"], w_packed, l)

        a_o = _pool_branch(p_all, prm["pool_w"], prm["pool_scale"], seq_len=seq)
        b_o, s_fin = _gla_branch(p_all, f_all, prm["wa2"], prm["ba"], prm["gn"], zero_state,
                                 n_seq=bp, seq_len=seq, decode=False)
        q_aug, k_aug, k_n, v_n, lf = _fox_prep(p_all, f_all, prm["qn"], prm["kn"], prm["bf"], seq_len=seq)
        c_o = _fox_flash(q_aug, k_aug, p_all, n_seq=bp, seq_len=seq)
        outs["kp"].append(jnp.swapaxes(k_n, 1, 2))
        outs["vp"].append(jnp.swapaxes(v_n, 1, 2))
        outs["lp"].append(lf[:, :C_HEADS].reshape(bp, seq, C_HEADS))
        outs["gp"].append(_state_from_kernel(s_fin))
        a_u = p_all[:, OFF_AU:OFF_AU + A_WIDTH].reshape(bp, seq, A_WIDTH)
        outs["pp"].append(a_u[:, seq - A_BUF:].astype(F32))

        pool_state = jnp.pad(state_pool[l], ((0, 0), (1, 0), (0, 0)))
        a_s = _pool_branch(p_s, prm["pool_w"], prm["pool_scale"], seq_len=dec_seq, state=pool_state)
        b_s, s_fin_s = _gla_branch(p_s, f_s, prm["wa2"], prm["ba"], prm["gn"], _state_to_kernel(state_gla[l]),
                                   n_seq=n_seq, seq_len=dec_seq, decode=True)
        q_aug_s, _, k_ns, v_ns, lf_s = _fox_prep(p_s, f_s, prm["qn"], prm["kn"], prm["bf"], seq_len=dec_seq)
        lfn_t = jnp.swapaxes(lf_s.reshape(n_seq, dec_seq, LANES)[:, :, :SUBLANES], 1, 2)
        lfn_t = jnp.pad(lfn_t, ((0, 0), (0, 0), (0, LANES - dec_seq)))
        ck_past, ck_new = _decode_bias(page_table, logf_hp, lfn_t, layer=l)
        c_s = _fox_decode(page_table, q_aug_s, cache_k_hm, cache_v_hm, ck_past, ck_new, k_ns, v_ns, p_s,
                          layer=l, dec_seq=dec_seq)
        yp, ys = _merge(yp, a_o, b_o, c_o, p_all, ys, a_s, b_s, c_s, p_s,
                        prm["b_merge"], prm["wa"], prm["wb"], prm["wc"], prm["wo"])
        outs["ks"].append(jnp.swapaxes(k_ns, 1, 2))
        outs["vs"].append(jnp.swapaxes(v_ns, 1, 2))
        outs["ls"].append(lf_s[:, :C_HEADS].reshape(n_seq, dec_seq, C_HEADS))
        outs["gs"].append(_state_from_kernel(s_fin_s))
        a_us = p_s[:, OFF_AU:OFF_AU + A_WIDTH].reshape(n_seq, dec_seq, A_WIDTH).astype(F32)
        outs["ps"].append(jnp.concatenate([state_pool[l], a_us], axis=1)[:, -A_BUF:])

    st = lambda k: jnp.stack(outs[k])
    return (yp.reshape(bp, seq, d), ys.reshape(n_seq, dec_seq, d),
            st("kp"), st("vp"), st("lp"), st("gp"), st("pp"),
            st("ks"), st("vs"), st("ls"), st("gs"), st("ps"))
```

```python
import functools
import math

import jax
import jax.numpy as jnp
from jax import lax
from jax.experimental import pallas as pl
from jax.experimental.pallas import tpu as pltpu

F32 = jnp.float32
BF16 = jnp.bfloat16
HIGHEST = lax.Precision.HIGHEST

D_MODEL = 2048
A_WIDTH = 512
A_WINDOWS = (2, 4, 8, 16)
A_GROUPS = 4
A_GROUP_DIM = 128
A_BUF = 15
B_HEADS = 4
B_DK = 96
B_DV = 192
B_GATE_RANK = 16
B_GATE_TEMP = 16.0
B_CHUNK = 32
C_HEADS = 6
C_HEAD_DIM = 128
C_WIDTH = 768
N_BRANCH = 3
EPS = 1e-6
PAGE_SIZE = 128
SPLIT_SIZES = (512, 512, 384, 384, 768, 768, 16, 768, 768, 768, 768, 6, 6144)

LANES = 128
SUBLANES = 8
VMEM_LIMIT_BYTES = 56 * 1024 * 1024

B_DKP = 128
B_DVP = 256
GLA_SPAN = 128

OFF_MG = 0
OFF_CG = 6144
OFF_CQ = OFF_CG + C_WIDTH
OFF_CK = OFF_CQ + C_WIDTH
OFF_CV = OFF_CK + C_WIDTH
OFF_AU = OFF_CV + C_WIDTH
OFF_AG = OFF_AU + A_WIDTH
OFF_BQ = OFF_AG + A_WIDTH
OFF_BK = OFF_BQ + B_HEADS * B_DKP
OFF_BV = OFF_BK + B_HEADS * B_DKP
OFF_BG = OFF_BV + B_HEADS * B_DVP
OFF_SM = OFF_BG + B_HEADS * B_DVP
SM_WIDTH = 2 * LANES
PROJ_TN = 1536
NP = 13824
assert OFF_SM + SM_WIDTH <= NP and NP % PROJ_TN == 0
SM_TILE = OFF_SM // PROJ_TN
SM_LOCAL = OFF_SM - SM_TILE * PROJ_TN
assert SM_LOCAL + SM_WIDTH <= PROJ_TN


def _pick_tile(n, pref, align=SUBLANES):
    if n <= pref:
        return n
    t = (pref // align) * align
    while t > align and n % t:
        t -= align
    assert n % t == 0, (n, pref)
    return t


def _cparams(*sem):
    return pltpu.CompilerParams(dimension_semantics=sem, vmem_limit_bytes=VMEM_LIMIT_BYTES)


def _log_sigmoid(x):
    return jnp.minimum(x, 0.0) - jnp.log1p(jnp.exp(-jnp.abs(x)))


def _silu(x):
    return x * jax.nn.sigmoid(x)


def _iota(shape, dim):
    return lax.broadcasted_iota(jnp.int32, shape, dim)


def _inproj_kernel(x_ref, xs_ref, g_ref, w_ref, p_ref, f_ref, ps_ref, fs_ref, h_scr, hs_scr, *, nj, rchunk):
    i = pl.program_id(0)
    j = pl.program_id(1)
    last_i = pl.num_programs(0) - 1

    def rms(x):
        r = lax.rsqrt(jnp.mean(x * x, axis=-1, keepdims=True) + EPS)
        return ((x * r) * g_ref[...]).astype(BF16)

    @pl.when(j == 0)
    def _():
        def body(c, carry):
            r0 = pl.multiple_of(c * rchunk, rchunk)
            h_scr[pl.ds(r0, rchunk), :] = rms(x_ref[pl.ds(r0, rchunk), :])
            return carry

        lax.fori_loop(0, x_ref.shape[0] // rchunk, body, 0)

        @pl.when(i == last_i)
        def _():
            hs_scr[...] = rms(xs_ref[...])

    dims = (((1,), (1,)), ((), ()))
    acc = lax.dot_general(h_scr[...], w_ref[...], dims, preferred_element_type=F32)
    p_ref[...] = acc.astype(p_ref.dtype)

    @pl.when(j == nj - 1)
    def _():
        f_ref[...] = acc[:, SM_LOCAL:SM_LOCAL + SM_WIDTH]

    @pl.when(i == last_i)
    def _():
        acc_s = lax.dot_general(hs_scr[...], w_ref[...], dims, preferred_element_type=F32)
        ps_ref[...] = acc_s

        @pl.when(j == nj - 1)
        def _():
            fs_ref[...] = acc_s[:, SM_LOCAL:SM_LOCAL + SM_WIDTH]


def _inproj(x2, xs2, norm_g, w_packed, layer):
    rows = x2.shape[0]
    rows_s = xs2.shape[0]
    tm = _pick_tile(rows, 1024)
    ni = rows // tm
    nj = NP // PROJ_TN
    assert SM_TILE == nj - 1
    rchunk = _pick_tile(tm, 128)
    return pl.pallas_call(
        functools.partial(_inproj_kernel, nj=nj, rchunk=rchunk),
        grid=(ni, nj),
        in_specs=[
            pl.BlockSpec((tm, D_MODEL), lambda i, j: (i, 0)),
            pl.BlockSpec((rows_s, D_MODEL), lambda i, j: (0, 0)),
            pl.BlockSpec((1, D_MODEL), lambda i, j: (0, 0)),
            pl.BlockSpec((None, PROJ_TN, D_MODEL), lambda i, j: (layer, j, 0)),
        ],
        out_specs=[
            pl.BlockSpec((tm, PROJ_TN), lambda i, j: (i, j)),
            pl.BlockSpec((tm, SM_WIDTH), lambda i, j: (i, 0)),
            pl.BlockSpec((rows_s, PROJ_TN), lambda i, j: (0, jnp.where(i == ni - 1, j, 0))),
            pl.BlockSpec((rows_s, SM_WIDTH), lambda i, j: (0, 0)),
        ],
        out_shape=[
            jax.ShapeDtypeStruct((rows, NP), BF16),
            jax.ShapeDtypeStruct((rows, SM_WIDTH), F32),
            jax.ShapeDtypeStruct((rows_s, NP), F32),
            jax.ShapeDtypeStruct((rows_s, SM_WIDTH), F32),
        ],
        scratch_shapes=[pltpu.VMEM((tm, D_MODEL), BF16), pltpu.VMEM((rows_s, D_MODEL), BF16)],
        compiler_params=_cparams("arbitrary", "arbitrary"),
        name="inproj",
    )(x2, xs2, norm_g.reshape(1, D_MODEL), w_packed)


def _pool_kernel(u_ref, g_ref, halo_ref, pw_ref, sc_ref, o_ref, band_scr, *, tiles_per_seq, decode):
    i = pl.program_id(0)
    tr = u_ref.shape[0]
    hr = halo_ref.shape[0]
    mm_dt = F32 if decode else BF16
    u = u_ref[...].astype(F32)
    halo = halo_ref[...].astype(F32)
    if not decode:
        halo = jnp.where(i % tiles_per_seq == 0, 0.0, halo)
    ext = jnp.concatenate([halo, u], axis=0).astype(mm_dt)

    @pl.when(i == 0)
    def _():
        r = _iota((tr, hr + tr), 0)
        c = _iota((tr, hr + tr), 1) - hr
        for g, w in enumerate(A_WINDOWS):
            band_scr[g] = ((c >= r - (w - 1)) & (c <= r)).astype(mm_dt)

    rowpos = _iota((tr, 1), 0)
    pos = ((i % tiles_per_seq) * tr + rowpos).astype(F32)
    outs = []
    for g, w in enumerate(A_WINDOWS):
        lo, hi = g * A_GROUP_DIM, (g + 1) * A_GROUP_DIM
        band = band_scr[g]
        if decode:
            win_sum = jnp.dot(band, ext[:, lo:hi], preferred_element_type=F32, precision=HIGHEST)
            count = float(w)
        else:
            win_sum = jnp.dot(band, ext[:, lo:hi], preferred_element_type=F32)
            count = jnp.minimum(pos + 1.0, float(w))
        pooled = win_sum / count - u[:, lo:hi]
        outs.append(jnp.dot(pooled.astype(mm_dt), pw_ref[g].astype(mm_dt), preferred_element_type=F32))
    mixed = jnp.concatenate(outs, axis=1) * sc_ref[...]
    o_ref[...] = (mixed * _silu(g_ref[...].astype(F32))).astype(o_ref.dtype)


def _pool_branch(p_all, pool_w, pool_scale, *, seq_len, state=None):
    rows = p_all.shape[0]
    decode = state is not None
    wdt = F32 if decode else BF16
    if decode:
        tr, tps = seq_len, 1
        hb = 16
        halo_spec = pl.BlockSpec((None, hb, A_WIDTH), lambda i: (i, 0, 0))
        halo_arr = state
    else:
        tr = _pick_tile(seq_len, 512, LANES)
        tps = seq_len // tr
        hb = LANES
        assert tr % hb == 0
        halo_spec = pl.BlockSpec((hb, A_WIDTH), lambda i: (jnp.maximum(i * (tr // hb) - 1, 0), OFF_AU // A_WIDTH))
        halo_arr = p_all
    return pl.pallas_call(
        functools.partial(_pool_kernel, tiles_per_seq=tps, decode=decode),
        grid=(rows // tr,),
        in_specs=[
            pl.BlockSpec((tr, A_WIDTH), lambda i: (i, OFF_AU // A_WIDTH)),
            pl.BlockSpec((tr, A_WIDTH), lambda i: (i, OFF_AG // A_WIDTH)),
            halo_spec,
            pl.BlockSpec((A_GROUPS, A_GROUP_DIM, A_GROUP_DIM), lambda i: (0, 0, 0)),
            pl.BlockSpec((1, A_WIDTH), lambda i: (0, 0)),
        ],
        out_specs=pl.BlockSpec((tr, A_WIDTH), lambda i: (i, 0)),
        out_shape=jax.ShapeDtypeStruct((rows, A_WIDTH), p_all.dtype),
        scratch_shapes=[pltpu.VMEM((A_GROUPS, tr, hb + tr), wdt)],
        compiler_params=_cparams("arbitrary"),
        name="pool_decode" if decode else "pool_prompt",
    )(p_all, p_all, halo_arr, pool_w.astype(wdt), pool_scale.reshape(1, A_WIDTH))


def _gla_kernel(q_ref, k_ref, v_ref, g_ref, r_ref, wa_ref, ba_ref, gn_ref, s0_ref,
                o_ref, so_ref, s_scr, *, chunk, span, decode):
    tb = pl.program_id(1)
    nb = pl.num_programs(1)
    tbk = q_ref.shape[0]
    mm_dt = F32 if decode else BF16

    @pl.when(tb == 0)
    def _():
        s_scr[...] = s0_ref[...]

    logit = jnp.dot(r_ref[...], wa_ref[...], preferred_element_type=F32, precision=HIGHEST) + ba_ref[...]
    log_a = _log_sigmoid(logit) / B_GATE_TEMP
    nsub = span // chunk
    tri = (_iota((span, span), 0) >= _iota((span, span), 1))
    scale = B_DK ** -0.5
    kw = log_a.shape[1]
    nt_dims = (((1,), (1,)), ((), ()))
    if not decode:
        la_hi = log_a.astype(BF16)
        la_r1 = log_a - la_hi.astype(F32)
        la_mid = la_r1.astype(BF16)
        la_lo = (la_r1 - la_mid.astype(F32)).astype(BF16)
        la_terms = jnp.concatenate([la_hi, la_mid, la_lo], axis=1)
        tri_b = tri.astype(BF16)
    zero_blk = jnp.zeros((chunk, B_DKP), mm_dt)
    for c in range(tbk // span):
        rs = slice(c * span, (c + 1) * span)
        if decode:
            b = jnp.dot(tri.astype(F32), log_a[rs], preferred_element_type=F32, precision=HIGHEST)
        else:
            b3 = jnp.dot(tri_b, la_terms[rs], preferred_element_type=F32)
            b = (b3[:, :kw] + b3[:, kw:2 * kw]) + b3[:, 2 * kw:]
        ends = [b[(i + 1) * chunk - 1:(i + 1) * chunk, :] for i in range(nsub)]
        starts = [jnp.zeros_like(ends[0])] + ends[:-1]
        b_last = ends[-1]
        rows_of = lambda vecs: jnp.concatenate([jnp.broadcast_to(v, (chunk, kw)) for v in vecs], axis=0)
        r_rows = rows_of(starts)
        q_in = (q_ref[rs, :].astype(F32) * scale) * jnp.exp(b - r_rows)
        k_in = k_ref[rs, :].astype(F32) * jnp.exp(r_rows - b)
        q_t = q_in * rows_of([jnp.exp(v) for v in starts])
        k_end = k_in * rows_of([jnp.exp(b_last - v) for v in starts])
        decay = jnp.exp(b_last)
        for h in range(B_HEADS):
            ks = slice(h * B_DKP, (h + 1) * B_DKP)
            vs = slice(h * B_DVP, (h + 1) * B_DVP)
            q_blocks, k_blocks = [], []
            for i in range(nsub):
                qi = q_in[i * chunk:(i + 1) * chunk, ks]
                kj = k_in[i * chunk:(i + 1) * chunk, ks].astype(mm_dt)
                q_blocks.append(jnp.concatenate(
                    [(qi * jnp.exp(starts[i][:, ks] - starts[j][:, ks])).astype(mm_dt) for j in range(i)]
                    + [qi.astype(mm_dt)] + [zero_blk] * (nsub - 1 - i), axis=1))
                k_blocks.append(jnp.concatenate([kj if j == i else zero_blk for j in range(nsub)], axis=1))
            att = lax.dot_general(jnp.concatenate(q_blocks, axis=0), jnp.concatenate(k_blocks, axis=0), nt_dims,
                                  preferred_element_type=F32)
            att = jnp.where(tri, att, 0.0)
            vh = v_ref[rs, vs].astype(mm_dt)
            st = s_scr[h]
            o = jnp.dot(att.astype(mm_dt), vh, preferred_element_type=F32)
            o = o + lax.dot_general(q_t[:, ks].astype(mm_dt), st.astype(mm_dt), nt_dims,
                                    preferred_element_type=F32)
            kv_t = lax.dot_general(vh, k_end[:, ks].astype(mm_dt), (((0,), (0,)), ((), ())),
                                   preferred_element_type=F32)
            s_scr[h] = st * decay[:, ks] + kv_t
            r = lax.rsqrt(jnp.sum(o * o, axis=-1, keepdims=True) * (1.0 / B_DV) + EPS)
            on = (o * r) * gn_ref[:, vs]
            o_ref[rs, vs] = (on * _silu(g_ref[rs, vs].astype(F32))).astype(o_ref.dtype)

    @pl.when(tb == nb - 1)
    def _():
        so_ref[...] = s_scr[...]


def _gla_branch(p_all, f_all, wa_p, ba_p, gn_p, s0_t, *, n_seq, seq_len, decode):
    rows = p_all.shape[0]
    chunk = math.gcd(seq_len, B_CHUNK)
    tbk = _pick_tile(seq_len, 256)
    nb = seq_len // tbk
    span = GLA_SPAN if (not decode and tbk % GLA_SPAN == 0 and GLA_SPAN % chunk == 0) else chunk
    kw = B_HEADS * B_DKP
    vw = B_HEADS * B_DVP
    return pl.pallas_call(
        functools.partial(_gla_kernel, chunk=chunk, span=span, decode=decode),
        grid=(n_seq, nb),
        in_specs=[
            pl.BlockSpec((tbk, kw), lambda s, t: (s * nb + t, OFF_BQ // kw)),
            pl.BlockSpec((tbk, kw), lambda s, t: (s * nb + t, OFF_BK // kw)),
            pl.BlockSpec((tbk, vw), lambda s, t: (s * nb + t, OFF_BV // vw)),
            pl.BlockSpec((tbk, vw), lambda s, t: (s * nb + t, OFF_BG // vw)),
            pl.BlockSpec((tbk, LANES), lambda s, t: (s * nb + t, 0)),
            pl.BlockSpec((LANES, kw), lambda s, t: (0, 0)),
            pl.BlockSpec((1, kw), lambda s, t: (0, 0)),
            pl.BlockSpec((1, vw), lambda s, t: (0, 0)),
            pl.BlockSpec((None, B_HEADS, B_DVP, B_DKP), lambda s, t: (s, 0, 0, 0)),
        ],
        out_specs=[
            pl.BlockSpec((tbk, vw), lambda s, t: (s * nb + t, 0)),
            pl.BlockSpec((None, B_HEADS, B_DVP, B_DKP), lambda s, t: (s, 0, 0, 0)),
        ],
        out_shape=[
            jax.ShapeDtypeStruct((rows, vw), p_all.dtype),
            jax.ShapeDtypeStruct((n_seq, B_HEADS, B_DVP, B_DKP), F32),
        ],
        scratch_shapes=[pltpu.VMEM((B_HEADS, B_DVP, B_DKP), F32)],
        compiler_params=_cparams("arbitrary", "arbitrary"),
        name="gla_decode" if decode else "gla_prompt",
    )(p_all, p_all, p_all, p_all, f_all, wa_p, ba_p, gn_p, s0_t)


def _foxprep_kernel(cq_ref, ck_ref, cv_ref, cf_ref, qn_ref, kn_ref, bf_ref, kstack_ref, vstack_ref,
                    qa_ref, ka_ref, ko_ref, vo_ref, lf_ref, carry_scr, *, tiles_per_seq):
    del kstack_ref, vstack_ref
    i = pl.program_id(0)
    tr = cq_ref.shape[0]
    hd = C_HEAD_DIM
    lane = _iota((tr, LANES), 1)

    lf = jnp.where(lane < C_HEADS, _log_sigmoid(cf_ref[...] + bf_ref[...]), 0.0)
    lf_ref[...] = lf

    @pl.when(i % tiles_per_seq == 0)
    def _():
        carry_scr[...] = jnp.zeros_like(carry_scr)

    tri = _iota((tr, tr), 0) >= _iota((tr, tr), 1)
    if tr % LANES == 0:
        hi = lf.astype(BF16)
        r1 = lf - hi.astype(F32)
        mid = r1.astype(BF16)
        lo = (r1 - mid.astype(F32)).astype(BF16)
        c3 = jnp.dot(tri.astype(BF16), jnp.concatenate([hi, mid, lo], axis=1), preferred_element_type=F32)
        cum = (c3[:, :LANES] + c3[:, LANES:2 * LANES]) + c3[:, 2 * LANES:]
    else:
        cum = jnp.dot(tri.astype(F32), lf, preferred_element_type=F32, precision=HIGHEST)
    cum = cum + carry_scr[...]
    carry_scr[...] = cum[tr - 1:tr, :]

    def qk_norm(x, g_ref):
        r = lax.rsqrt(jnp.mean(x * x, axis=-1, keepdims=True) + EPS)
        return (x * r) * g_ref[...]

    one = jnp.ones((tr, LANES), F32)
    zero = jnp.zeros((tr, LANES), F32)
    for h in range(C_HEADS):
        cs = slice(h * hd, (h + 1) * hd)
        ch = jnp.sum(jnp.where(lane == h, cum, 0.0), axis=-1, keepdims=True)
        hi = ch.astype(BF16).astype(F32)
        r1 = ch - hi
        mid = r1.astype(BF16).astype(F32)
        lo = r1 - mid
        qn = qk_norm(cq_ref[:, cs].astype(F32), qn_ref)
        kn = qk_norm(ck_ref[:, cs].astype(F32), kn_ref)
        aux_q = jnp.where(lane == 0, hi, jnp.where(lane == 1, mid, jnp.where(lane == 2, lo,
                          jnp.where(lane < 6, one, zero))))
        aux_k = jnp.where(lane < 3, one, jnp.where(lane == 3, -hi, jnp.where(lane == 4, -mid,
                          jnp.where(lane == 5, -lo, zero))))
        a0 = 2 * h * hd
        qa_ref[:, a0:a0 + hd] = (qn * (hd ** -0.5)).astype(qa_ref.dtype)
        qa_ref[:, a0 + hd:a0 + 2 * hd] = aux_q.astype(qa_ref.dtype)
        ka_ref[:, a0:a0 + hd] = kn.astype(ka_ref.dtype)
        ka_ref[:, a0 + hd:a0 + 2 * hd] = aux_k.astype(ka_ref.dtype)
        ko_ref[h] = kn
        vo_ref[h] = cv_ref[:, cs].astype(F32)


def _fox_prep(p_all, f_all, fox_q_norm, fox_k_norm, bf_p, k_stack, v_stack, *, seq_len, layer):
    rows = p_all.shape[0]
    tr = _pick_tile(seq_len, 512)
    tps = seq_len // tr
    hd = C_HEAD_DIM
    cw = C_WIDTH
    aw = C_HEADS * 2 * hd
    slab = pl.BlockSpec((None, None, C_HEADS, tr, hd), lambda i: (layer, i // tps, 0, i % tps, 0))
    in_place = pl.BlockSpec(memory_space=pl.ANY)
    return pl.pallas_call(
        functools.partial(_foxprep_kernel, tiles_per_seq=tps),
        grid=(rows // tr,),
        in_specs=[
            pl.BlockSpec((tr, cw), lambda i: (i, OFF_CQ // cw)),
            pl.BlockSpec((tr, cw), lambda i: (i, OFF_CK // cw)),
            pl.BlockSpec((tr, cw), lambda i: (i, OFF_CV // cw)),
            pl.BlockSpec((tr, LANES), lambda i: (i, 1)),
            pl.BlockSpec((1, hd), lambda i: (0, 0)),
            pl.BlockSpec((1, hd), lambda i: (0, 0)),
            pl.BlockSpec((1, LANES), lambda i: (0, 0)),
            in_place,
            in_place,
        ],
        out_specs=[
            pl.BlockSpec((tr, aw), lambda i: (i, 0)),
            pl.BlockSpec((tr, aw), lambda i: (i, 0)),
            slab,
            slab,
            pl.BlockSpec((tr, LANES), lambda i: (i, 0)),
        ],
        out_shape=[
            jax.ShapeDtypeStruct((rows, aw), p_all.dtype),
            jax.ShapeDtypeStruct((rows, aw), p_all.dtype),
            jax.ShapeDtypeStruct(k_stack.shape, F32),
            jax.ShapeDtypeStruct(v_stack.shape, F32),
            jax.ShapeDtypeStruct((rows, LANES), F32),
        ],
        input_output_aliases={7: 2, 8: 3},
        scratch_shapes=[pltpu.VMEM((1, LANES), F32)],
        compiler_params=_cparams("arbitrary"),
        name="fox_prep",
    )(p_all, p_all, p_all, f_all, fox_q_norm.reshape(1, hd), fox_k_norm.reshape(1, hd), bf_p, k_stack, v_stack)


def _flash_kernel(q_ref, k_ref, v_ref, g_ref, o_ref, m_scr, l_scr, acc_scr, *, rs):
    i = pl.program_id(2)
    t = q_ref.shape[0]
    hd = C_HEAD_DIM
    m_scr[...] = jnp.full_like(m_scr, -jnp.inf)
    l_scr[...] = jnp.zeros_like(l_scr)
    acc_scr[...] = jnp.zeros_like(acc_scr)
    ones = jnp.ones((t, LANES), BF16)

    def step(j, diagonal):
        k0 = pl.multiple_of(j * t, t)
        kb = k_ref[pl.ds(k0, t), :]
        vb = jnp.concatenate([v_ref[pl.ds(k0, t), :], ones], axis=1)
        for r in range(t // rs):
            rows = slice(r * rs, (r + 1) * rs)
            nk = (r + 1) * rs if diagonal else t
            s = lax.dot_general(q_ref[rows, :], kb[:nk], (((1,), (1,)), ((), ())), preferred_element_type=F32)
            if diagonal:
                s = jnp.where(_iota((rs, nk), 0) + r * rs >= _iota((rs, nk), 1), s, -jnp.inf)
            m_prev = m_scr[rows, :]
            m_new = jnp.maximum(m_prev, jnp.max(s, axis=-1, keepdims=True))
            alpha = jnp.exp(m_prev - m_new)
            p = jnp.exp(s - jnp.concatenate([m_new] * (nk // LANES), axis=1))
            pv = jnp.dot(p.astype(BF16), vb[:nk], preferred_element_type=F32)
            l_scr[rows, :] = alpha * l_scr[rows, :] + pv[:, hd:]
            acc_scr[rows, :] = alpha * acc_scr[rows, :] + pv[:, :hd]
            m_scr[rows, :] = m_new

    def pair(jj, carry):
        step(2 * jj, False)
        step(2 * jj + 1, False)
        return carry

    lax.fori_loop(0, i // 2, pair, 0)

    @pl.when(i % 2 == 1)
    def _():
        step(i - 1, False)

    step(i, True)
    out = acc_scr[...] / l_scr[...]
    o_ref[...] = (out * _silu(g_ref[...].astype(F32))).astype(BF16)


def _fox_flash(q_aug, k_aug, p_all, *, n_seq, seq_len):
    rows = q_aug.shape[0]
    t = _pick_tile(seq_len, 1024, LANES)
    rs = _pick_tile(t, 256, LANES)
    nt = seq_len // t
    hd = C_HEAD_DIM
    aw = 2 * hd
    return pl.pallas_call(
        functools.partial(_flash_kernel, rs=rs),
        grid=(n_seq, C_HEADS, nt),
        in_specs=[
            pl.BlockSpec((t, aw), lambda b, h, i: (b * nt + i, h)),
            pl.BlockSpec((seq_len, aw), lambda b, h, i: (b, h)),
            pl.BlockSpec((seq_len, hd), lambda b, h, i: (b, OFF_CV // hd + h)),
            pl.BlockSpec((t, hd), lambda b, h, i: (b * nt + i, OFF_CG // hd + h)),
        ],
        out_specs=pl.BlockSpec((t, hd), lambda b, h, i: (b * nt + i, h)),
        out_shape=jax.ShapeDtypeStruct((rows, C_WIDTH), BF16),
        scratch_shapes=[pltpu.VMEM((t, LANES), F32), pltpu.VMEM((t, LANES), F32), pltpu.VMEM((t, hd), F32)],
        compiler_params=_cparams("parallel", "parallel", "arbitrary"),
        name="fox_flash",
    )(q_aug, k_aug, p_all, p_all)


def _decbias_kernel(pt_ref, lf_ref, lfn_ref, ck_ref, ckn_ref, g_scr):
    s = pl.program_id(0)
    npg = g_scr.shape[1]

    def gather(p, carry):
        page = pt_ref[s, p]
        for h in range(C_HEADS):
            g_scr[h, pl.ds(p, 1), :] = lf_ref[h, pl.ds(page, 1), :]
        return carry

    lax.fori_loop(0, npg, gather, 0)
    upper = (_iota((LANES, LANES), 0) <= _iota((LANES, LANES), 1)).astype(F32)
    earlier = (_iota((npg, npg), 1) < _iota((npg, npg), 0)).astype(F32)
    lfn_cum = jnp.dot(lfn_ref[...], upper, preferred_element_type=F32, precision=HIGHEST)
    ckn_ref[...] = jnp.zeros_like(ckn_ref)
    for h in range(C_HEADS):
        cum = jnp.dot(g_scr[h], upper, preferred_element_type=F32, precision=HIGHEST)
        tot = jnp.broadcast_to(cum[:, LANES - 1:LANES], cum.shape)
        before = jnp.dot(earlier, tot, preferred_element_type=F32, precision=HIGHEST)
        ck_ref[h] = cum + before
        ckn_ref[h:h + 1, :] = lfn_cum[h:h + 1, :] + (before[npg - 1:npg, :] + tot[npg - 1:npg, :])


def _decode_bias(page_table, logf_hp, lfn_t, *, layer):
    n_seq, npg = page_table.shape
    n_pool = logf_hp.shape[2]
    return pl.pallas_call(
        _decbias_kernel,
        grid_spec=pltpu.PrefetchScalarGridSpec(
            num_scalar_prefetch=1,
            grid=(n_seq,),
            in_specs=[
                pl.BlockSpec((None, C_HEADS, n_pool, PAGE_SIZE), lambda s, pt: (layer, 0, 0, 0)),
                pl.BlockSpec((None, SUBLANES, LANES), lambda s, pt: (s, 0, 0)),
            ],
            out_specs=[
                pl.BlockSpec((None, C_HEADS, npg, PAGE_SIZE), lambda s, pt: (s, 0, 0, 0)),
                pl.BlockSpec((None, SUBLANES, LANES), lambda s, pt: (s, 0, 0)),
            ],
            scratch_shapes=[pltpu.VMEM((C_HEADS, npg, PAGE_SIZE), F32)],
        ),
        out_shape=[
            jax.ShapeDtypeStruct((n_seq, C_HEADS, npg, PAGE_SIZE), F32),
            jax.ShapeDtypeStruct((n_seq, SUBLANES, LANES), F32),
        ],
        compiler_params=_cparams("arbitrary"),
        name="decode_bias",
    )(page_table, logf_hp, lfn_t)


DEC_QROWS = 16


def _decode_kernel(pt_ref, qa_ref, *refs, gpp, dec_seq):
    del pt_ref
    k_refs = refs[:gpp]
    v_refs = refs[gpp:2 * gpp]
    ck_ref, ckn_ref, kn_ref, vn_ref, g_ref, o_ref, q_scr, m_scr, l_scr, acc_scr = refs[2 * gpp:]
    t = pl.program_id(1)
    nt = pl.num_programs(1)
    nq = dec_seq
    qp = DEC_QROWS
    nrow = C_HEADS * qp
    hd = C_HEAD_DIM
    row = _iota((nrow, LANES), 0)
    lane = _iota((nrow, LANES), 1)
    qi = jnp.bitwise_and(row, qp - 1)

    @pl.when(t == 0)
    def _():
        qa = qa_ref[...].astype(F32)
        zero = jnp.zeros((qp - nq, hd), F32)
        q_scr[...] = jnp.concatenate(
            [jnp.concatenate([qa[:, h * 2 * hd:h * 2 * hd + hd], zero], axis=0) for h in range(C_HEADS)],
            axis=0).astype(BF16)
        m_scr[...] = jnp.full_like(m_scr, -jnp.inf)
        l_scr[...] = jnp.zeros_like(l_scr)
        acc_scr[...] = jnp.zeros_like(acc_scr)

    def head_rows(get_row):
        return jnp.concatenate(
            [jnp.broadcast_to(get_row(h), (qp, get_row(h).shape[1])) for h in range(C_HEADS)], axis=0)

    ckn_rows = head_rows(lambda h: ckn_ref[h:h + 1, :])
    c_q = jnp.sum(jnp.where(lane == qi, ckn_rows, 0.0), axis=-1, keepdims=True)

    def attend(k_of, v_of, ck_rows, mask):
        s = jnp.concatenate(
            [lax.dot_general(q_scr[h * qp:(h + 1) * qp, :], k_of(h), (((1,), (1,)), ((), ())),
                             preferred_element_type=F32) for h in range(C_HEADS)], axis=0)
        s = s + (c_q - ck_rows)
        if mask is not None:
            s = jnp.where(mask, s, -jnp.inf)
        m_prev = m_scr[...]
        m_new = jnp.maximum(m_prev, jnp.max(s, axis=-1, keepdims=True))
        alpha = jnp.exp(m_prev - m_new)
        pr = jnp.exp(s - m_new)
        l_scr[...] = alpha * l_scr[...] + jnp.sum(pr, axis=-1, keepdims=True)
        prb = pr.astype(BF16)
        o = jnp.concatenate(
            [jnp.dot(prb[h * qp:(h + 1) * qp, :], v_of(h), preferred_element_type=F32) for h in range(C_HEADS)],
            axis=0)
        acc_scr[...] = alpha * acc_scr[...] + o
        m_scr[...] = m_new

    attend(lambda h: jnp.concatenate([k_refs[g][h] for g in range(gpp)], axis=0).astype(BF16),
           lambda h: jnp.concatenate([v_refs[g][h] for g in range(gpp)], axis=0).astype(BF16),
           head_rows(lambda h: jnp.concatenate([ck_ref[h, g:g + 1, :] for g in range(gpp)], axis=1)), None)

    @pl.when(t == nt - 1)
    def _():
        pad = jnp.zeros((PAGE_SIZE - nq, hd), F32)
        attend(lambda h: jnp.concatenate([kn_ref[h], pad], axis=0).astype(BF16),
               lambda h: jnp.concatenate([vn_ref[h], pad], axis=0).astype(BF16),
               ckn_rows, lane <= qi)
        out = acc_scr[...] / l_scr[...]
        gate = _silu(g_ref[...].astype(F32))
        for h in range(C_HEADS):
            cs = slice(h * hd, (h + 1) * hd)
            o_ref[:, cs] = (out[h * qp:h * qp + nq, :] * gate[:, cs]).astype(o_ref.dtype)


def _fox_decode(page_table, q_aug, cache_k_hm, cache_v_hm, ck_past, ck_new, k_new, v_new, p_all, *, layer, dec_seq):
    n_seq, npg = page_table.shape
    rows = q_aug.shape[0]
    nrow = C_HEADS * DEC_QROWS
    gpp = _pick_tile(npg, 16, SUBLANES)

    def page_spec(g):
        return pl.BlockSpec((None, None, C_HEADS, PAGE_SIZE, C_HEAD_DIM),
                            lambda s, t, pt: (layer, pt[s, t * gpp + g], 0, 0, 0))

    return pl.pallas_call(
        functools.partial(_decode_kernel, gpp=gpp, dec_seq=dec_seq),
        grid_spec=pltpu.PrefetchScalarGridSpec(
            num_scalar_prefetch=1,
            grid=(n_seq, npg // gpp),
            in_specs=[pl.BlockSpec((dec_seq, C_HEADS * 2 * C_HEAD_DIM), lambda s, t, pt: (s, 0))]
            + [page_spec(g) for g in range(gpp)]
            + [page_spec(g) for g in range(gpp)]
            + [
                pl.BlockSpec((None, C_HEADS, gpp, PAGE_SIZE), lambda s, t, pt: (s, 0, t, 0)),
                pl.BlockSpec((None, SUBLANES, LANES), lambda s, t, pt: (s, 0, 0)),
                pl.BlockSpec((None, C_HEADS, dec_seq, C_HEAD_DIM), lambda s, t, pt: (s, 0, 0, 0)),
                pl.BlockSpec((None, C_HEADS, dec_seq, C_HEAD_DIM), lambda s, t, pt: (s, 0, 0, 0)),
                pl.BlockSpec((dec_seq, C_WIDTH), lambda s, t, pt: (s, OFF_CG // C_WIDTH)),
            ],
            out_specs=pl.BlockSpec((dec_seq, C_WIDTH), lambda s, t, pt: (s, 0)),
            scratch_shapes=[
                pltpu.VMEM((nrow, C_HEAD_DIM), BF16),
                pltpu.VMEM((nrow, 1), F32),
                pltpu.VMEM((nrow, 1), F32),
                pltpu.VMEM((nrow, C_HEAD_DIM), F32),
            ],
        ),
        out_shape=jax.ShapeDtypeStruct((rows, C_WIDTH), p_all.dtype),
        compiler_params=_cparams("arbitrary", "arbitrary"),
        name="fox_decode",
    )(page_table, q_aug, *([cache_k_hm] * gpp), *([cache_v_hm] * gpp), ck_past, ck_new, k_new, v_new, p_all)


def _merge_kernel(x_ref, a_ref, b_ref, c_ref, mg0_ref, mg1_ref, mg2_ref,
                  xs_ref, as_ref, bs_ref, cs_ref, mgs0_ref, mgs1_ref, mgs2_ref,
                  bm_ref, wa_ref, wb_ref, wc_ref, wo_ref, y_ref, ys_ref):
    d = D_MODEL

    def gated(o_ref, w_ref, mg_ref, k):
        gate = jax.nn.sigmoid(mg_ref[...].astype(F32) + bm_ref[:, k * d:(k + 1) * d])
        return gate * jnp.dot(o_ref[...].astype(BF16), w_ref[...], preferred_element_type=F32)

    def layer_out(xr, ar, br, cr, m0, m1, m2):
        merged = gated(ar, wa_ref, m0, 0) + gated(br, wb_ref, m1, 1) + gated(cr, wc_ref, m2, 2)
        return xr[...] + jnp.dot(merged.astype(BF16), wo_ref[...], preferred_element_type=F32)

    y_ref[...] = layer_out(x_ref, a_ref, b_ref, c_ref, mg0_ref, mg1_ref, mg2_ref)

    @pl.when(pl.program_id(0) == pl.num_programs(0) - 1)
    def _():
        ys_ref[...] = layer_out(xs_ref, as_ref, bs_ref, cs_ref, mgs0_ref, mgs1_ref, mgs2_ref)


def _merge(x2, a_o, b_o, c_o, p_all, xs2, a_s, b_s, c_s, p_s, b_merge, wa, wb, wc, wo):
    rows = x2.shape[0]
    rows_s = xs2.shape[0]
    tm = _pick_tile(rows, 256)
    d = D_MODEL
    bw = B_HEADS * B_DVP

    def const(shape, col=0):
        return pl.BlockSpec(shape, lambda i: (0, col), pipeline_mode=pl.Buffered(1))

    return pl.pallas_call(
        _merge_kernel,
        grid=(rows // tm,),
        in_specs=[
            pl.BlockSpec((tm, d), lambda i: (i, 0)),
            pl.BlockSpec((tm, A_WIDTH), lambda i: (i, 0)),
            pl.BlockSpec((tm, bw), lambda i: (i, 0)),
            pl.BlockSpec((tm, C_WIDTH), lambda i: (i, 0)),
            pl.BlockSpec((tm, d), lambda i: (i, OFF_MG // d + 0)),
            pl.BlockSpec((tm, d), lambda i: (i, OFF_MG // d + 1)),
            pl.BlockSpec((tm, d), lambda i: (i, OFF_MG // d + 2)),
            const((rows_s, d)),
            const((rows_s, A_WIDTH)),
            const((rows_s, bw)),
            const((rows_s, C_WIDTH)),
            const((rows_s, d), OFF_MG // d + 0),
            const((rows_s, d), OFF_MG // d + 1),
            const((rows_s, d), OFF_MG // d + 2),
            const((1, N_BRANCH * d)),
            const((A_WIDTH, d)),
            const((bw, d)),
            const((C_WIDTH, d)),
            const((d, d)),
        ],
        out_specs=[pl.BlockSpec((tm, d), lambda i: (i, 0)), pl.BlockSpec((rows_s, d), lambda i: (0, 0))],
        out_shape=[jax.ShapeDtypeStruct((rows, d), F32), jax.ShapeDtypeStruct((rows_s, d), F32)],
        compiler_params=_cparams("arbitrary"),
        name="merge_out",
    )(x2, a_o, b_o, c_o, p_all, p_all, p_all, xs2, a_s, b_s, c_s, p_s, p_s, p_s,
      b_merge.reshape(1, N_BRANCH * d), wa, wb, wc, wo)


def _pad_heads(x, n_heads, d, dp, axis):
    shp = x.shape
    x = x.reshape(shp[:axis] + (n_heads, d) + shp[axis + 1:])
    pad = [(0, 0)] * x.ndim
    pad[axis + 1] = (0, dp - d)
    x = jnp.pad(x, pad)
    return x.reshape(shp[:axis] + (n_heads * dp,) + shp[axis + 1:])


def _pack_plan():
    src = {}
    acc = 0
    for name, size in zip(("a_u", "a_g", "b_q", "b_k", "b_v", "b_g", "b_r", "c_q", "c_k", "c_v", "c_g", "c_f", "m_g"),
                          SPLIT_SIZES):
        src[name] = acc
        acc += size
    d_in = acc
    segs = [(OFF_MG, "m_g", 1, 6144, 6144), (OFF_CG, "c_g", 1, C_WIDTH, C_WIDTH), (OFF_CQ, "c_q", 1, C_WIDTH, C_WIDTH),
            (OFF_CK, "c_k", 1, C_WIDTH, C_WIDTH), (OFF_CV, "c_v", 1, C_WIDTH, C_WIDTH),
            (OFF_AU, "a_u", 1, A_WIDTH, A_WIDTH), (OFF_AG, "a_g", 1, A_WIDTH, A_WIDTH),
            (OFF_BQ, "b_q", B_HEADS, B_DK, B_DKP), (OFF_BK, "b_k", B_HEADS, B_DK, B_DKP),
            (OFF_BV, "b_v", B_HEADS, B_DV, B_DVP), (OFF_BG, "b_g", B_HEADS, B_DV, B_DVP),
            (OFF_SM, "b_r", 1, B_GATE_RANK, LANES), (OFF_SM + LANES, "c_f", 1, C_HEADS, LANES)]
    starts = [0] * (NP // LANES)
    nvalid = [0] * (NP // LANES)
    for off, name, heads, dreal, dpad in segs:
        for h in range(heads):
            for b in range(dpad // LANES):
                blk = (off + h * dpad) // LANES + b
                n = max(0, min(LANES, dreal - b * LANES))
                s = src[name] + h * dreal + b * LANES
                assert s + LANES <= d_in
                starts[blk], nvalid[blk] = (s if n else 0), n
    return starts, nvalid


PACK_WINDOWS = 4


def _packw_kernel(st_ref, nv_ref, *refs):
    del st_ref
    w_refs, o_ref = refs[:PACK_WINDOWS], refs[PACK_WINDOWS]
    j = pl.program_id(0)
    row = _iota((LANES, D_MODEL), 0)
    for g in range(PACK_WINDOWS):
        live = row < nv_ref[j * PACK_WINDOWS + g]
        for l in range(o_ref.shape[0]):
            o_ref[l, g * LANES:(g + 1) * LANES, :] = jnp.where(live, w_refs[g][:, l, :], 0.0).astype(BF16)


def _pack_w_in(w_in):
    depth = w_in.shape[0]
    w_t = jnp.transpose(w_in, (2, 0, 1))
    starts, nvalid = _pack_plan()
    nblk = NP // LANES
    assert nblk % PACK_WINDOWS == 0

    def window(g):
        return pl.BlockSpec((pl.Element(LANES), pl.Element(depth), pl.Element(D_MODEL)),
                            lambda j, st, nv: (st[j * PACK_WINDOWS + g], 0, 0))

    return pl.pallas_call(
        _packw_kernel,
        grid_spec=pltpu.PrefetchScalarGridSpec(
            num_scalar_prefetch=2,
            grid=(nblk // PACK_WINDOWS,),
            in_specs=[window(g) for g in range(PACK_WINDOWS)],
            out_specs=pl.BlockSpec((depth, PACK_WINDOWS * LANES, D_MODEL), lambda j, st, nv: (0, j, 0)),
        ),
        out_shape=jax.ShapeDtypeStruct((depth, NP, D_MODEL), BF16),
        compiler_params=_cparams("parallel"),
        name="pack_w_in",
    )(jnp.asarray(starts, jnp.int32), jnp.asarray(nvalid, jnp.int32), *([w_t] * PACK_WINDOWS))


def _layer_params(l, norm_g, pool_w, pool_scale, gla_w_a2, gla_b_a, gla_norm, fox_b_f, fox_q_norm,
                  fox_k_norm, w_branch_a, w_branch_b, w_branch_c, b_merge, w_out):
    wa2 = _pad_heads(gla_w_a2[l], B_HEADS, B_DK, B_DKP, 1)
    return dict(
        norm_g=norm_g[l],
        pool_w=pool_w[l],
        pool_scale=pool_scale[l],
        wa2=jnp.pad(wa2, ((0, LANES - B_GATE_RANK), (0, 0))),
        ba=_pad_heads(gla_b_a[l].reshape(1, -1), B_HEADS, B_DK, B_DKP, 1),
        gn=_pad_heads(gla_norm[l].reshape(1, -1), B_HEADS, B_DV, B_DVP, 1),
        bf=jnp.pad(fox_b_f[l].reshape(1, -1), ((0, 0), (0, LANES - C_HEADS))),
        qn=fox_q_norm[l],
        kn=fox_k_norm[l],
        wa=w_branch_a[l].astype(BF16),
        wb=_pad_heads(w_branch_b[l], B_HEADS, B_DV, B_DVP, 0).astype(BF16),
        wc=w_branch_c[l].astype(BF16),
        b_merge=b_merge[l],
        wo=w_out[l].astype(BF16),
    )


def _state_to_kernel(s):
    st = jnp.swapaxes(s, -1, -2)
    return jnp.pad(st, ((0, 0), (0, 0), (0, B_DVP - B_DV), (0, B_DKP - B_DK)))


def _state_from_kernel(st):
    return jnp.swapaxes(st[:, :, :B_DV, :B_DK], -1, -2)


def kernel(x_prompt, x_sample, cache_k, cache_v, cache_logf, state_gla, state_pool, page_table, norm_g, w_in, pool_w, pool_scale, gla_w_a2, gla_b_a, gla_norm, fox_b_f, fox_q_norm, fox_k_norm, w_branch_a, w_branch_b, w_branch_c, b_merge, w_out):
    depth = w_in.shape[0]
    bp, seq, d = x_prompt.shape
    n_seq, dec_seq, _ = x_sample.shape
    assert d == D_MODEL and dec_seq == SUBLANES and seq % LANES == 0
    yp = x_prompt.reshape(bp * seq, d)
    ys = x_sample.reshape(n_seq * dec_seq, d)
    cache_k_hm = jnp.transpose(cache_k, (0, 1, 3, 2, 4))
    cache_v_hm = jnp.transpose(cache_v, (0, 1, 3, 2, 4))
    logf_hp = jnp.transpose(cache_logf, (0, 3, 1, 2))
    zero_state = jnp.zeros((bp, B_HEADS, B_DVP, B_DKP), F32)
    w_packed = _pack_w_in(w_in)
    outs = {k: [] for k in ("lp", "gp", "pp", "ls", "gs", "ps")}
    kp_stack = jnp.zeros((depth, bp, C_HEADS, seq, C_HEAD_DIM), F32)
    vp_stack = jnp.zeros((depth, bp, C_HEADS, seq, C_HEAD_DIM), F32)
    ks_stack = jnp.zeros((depth, n_seq, C_HEADS, dec_seq, C_HEAD_DIM), F32)
    vs_stack = jnp.zeros((depth, n_seq, C_HEADS, dec_seq, C_HEAD_DIM), F32)
    for l in range(depth):
        prm = _layer_params(l, norm_g, pool_w, pool_scale, gla_w_a2, gla_b_a, gla_norm, fox_b_f,
                            fox_q_norm, fox_k_norm, w_branch_a, w_branch_b, w_branch_c, b_merge, w_out)

        p_all, f_all, p_s, f_s = _inproj(yp, ys, prm["norm_g"], w_packed, l)

        a_o = _pool_branch(p_all, prm["pool_w"], prm["pool_scale"], seq_len=seq)
        b_o, s_fin = _gla_branch(p_all, f_all, prm["wa2"], prm["ba"], prm["gn"], zero_state,
                                 n_seq=bp, seq_len=seq, decode=False)
        q_aug, k_aug, kp_stack, vp_stack, lf = _fox_prep(p_all, f_all, prm["qn"], prm["kn"], prm["bf"],
                                                         kp_stack, vp_stack, seq_len=seq, layer=l)
        c_o = _fox_flash(q_aug, k_aug, p_all, n_seq=bp, seq_len=seq)
        outs["lp"].append(lf[:, :C_HEADS].reshape(bp, seq, C_HEADS))
        outs["gp"].append(_state_from_kernel(s_fin))
        a_u = p_all[:, OFF_AU:OFF_AU + A_WIDTH].reshape(bp, seq, A_WIDTH)
        outs["pp"].append(a_u[:, seq - A_BUF:].astype(F32))

        pool_state = jnp.pad(state_pool[l], ((0, 0), (1, 0), (0, 0)))
        a_s = _pool_branch(p_s, prm["pool_w"], prm["pool_scale"], seq_len=dec_seq, state=pool_state)
        b_s, s_fin_s = _gla_branch(p_s, f_s, prm["wa2"], prm["ba"], prm["gn"], _state_to_kernel(state_gla[l]),
                                   n_seq=n_seq, seq_len=dec_seq, decode=True)
        q_aug_s, _, ks_stack, vs_stack, lf_s = _fox_prep(p_s, f_s, prm["qn"], prm["kn"], prm["bf"],
                                                         ks_stack, vs_stack, seq_len=dec_seq, layer=l)
        k_ns, v_ns = ks_stack[l], vs_stack[l]
        lfn_t = jnp.swapaxes(lf_s.reshape(n_seq, dec_seq, LANES)[:, :, :SUBLANES], 1, 2)
        lfn_t = jnp.pad(lfn_t, ((0, 0), (0, 0), (0, LANES - dec_seq)))
        ck_past, ck_new = _decode_bias(page_table, logf_hp, lfn_t, layer=l)
        c_s = _fox_decode(page_table, q_aug_s, cache_k_hm, cache_v_hm, ck_past, ck_new, k_ns, v_ns, p_s,
                          layer=l, dec_seq=dec_seq)
        yp, ys = _merge(yp, a_o, b_o, c_o, p_all, ys, a_s, b_s, c_s, p_s,
                        prm["b_merge"], prm["wa"], prm["wb"], prm["wc"], prm["wo"])
        outs["ls"].append(lf_s[:, :C_HEADS].reshape(n_seq, dec_seq, C_HEADS))
        outs["gs"].append(_state_from_kernel(s_fin_s))
        a_us = p_s[:, OFF_AU:OFF_AU + A_WIDTH].reshape(n_seq, dec_seq, A_WIDTH).astype(F32)
        outs["ps"].append(jnp.concatenate([state_pool[l], a_us], axis=1)[:, -A_BUF:])

    st = lambda k: jnp.stack(outs[k])
    hm = lambda a: jnp.swapaxes(a, 2, 3)
    return (yp.reshape(bp, seq, d), ys.reshape(n_seq, dec_seq, d),
            hm(kp_stack), hm(vp_stack), st("lp"), st("gp"), st("pp"),
            hm(ks_stack), hm(vs_stack), st("ls"), st("gs"), st("ps"))
```

```python
import functools
import math

import jax
import jax.numpy as jnp
from jax import lax
from jax.experimental import pallas as pl
from jax.experimental.pallas import tpu as pltpu

F32 = jnp.float32
BF16 = jnp.bfloat16
HIGHEST = lax.Precision.HIGHEST

D_MODEL = 2048
A_WIDTH = 512
A_WINDOWS = (2, 4, 8, 16)
A_GROUPS = 4
A_GROUP_DIM = 128
A_BUF = 15
B_HEADS = 4
B_DK = 96
B_DV = 192
B_GATE_RANK = 16
B_GATE_TEMP = 16.0
B_CHUNK = 32
C_HEADS = 6
C_HEAD_DIM = 128
C_WIDTH = 768
N_BRANCH = 3
EPS = 1e-6
PAGE_SIZE = 128
SPLIT_SIZES = (512, 512, 384, 384, 768, 768, 16, 768, 768, 768, 768, 6, 6144)

LANES = 128
SUBLANES = 8
VMEM_LIMIT_BYTES = 56 * 1024 * 1024

B_DKP = 128
B_DVP = 256
GLA_SPAN = 128

OFF_MG = 0
OFF_CG = 6144
OFF_CQ = OFF_CG + C_WIDTH
OFF_CK = OFF_CQ + C_WIDTH
OFF_CV = OFF_CK + C_WIDTH
OFF_AU = OFF_CV + C_WIDTH
OFF_AG = OFF_AU + A_WIDTH
OFF_BQ = OFF_AG + A_WIDTH
OFF_BK = OFF_BQ + B_HEADS * B_DKP
OFF_BV = OFF_BK + B_HEADS * B_DKP
OFF_BG = OFF_BV + B_HEADS * B_DVP
OFF_SM = OFF_BG + B_HEADS * B_DVP
SM_WIDTH = 2 * LANES
PROJ_TN = 1536
NP = 13824
assert OFF_SM + SM_WIDTH <= NP and NP % PROJ_TN == 0
SM_TILE = OFF_SM // PROJ_TN
SM_LOCAL = OFF_SM - SM_TILE * PROJ_TN
assert SM_LOCAL + SM_WIDTH <= PROJ_TN


def _pick_tile(n, pref, align=SUBLANES):
    if n <= pref:
        return n
    t = (pref // align) * align
    while t > align and n % t:
        t -= align
    assert n % t == 0, (n, pref)
    return t


def _cparams(*sem):
    return pltpu.CompilerParams(dimension_semantics=sem, vmem_limit_bytes=VMEM_LIMIT_BYTES)


def _log_sigmoid(x):
    return jnp.minimum(x, 0.0) - jnp.log1p(jnp.exp(-jnp.abs(x)))


def _silu(x):
    return x * jax.nn.sigmoid(x)


def _iota(shape, dim):
    return lax.broadcasted_iota(jnp.int32, shape, dim)


def _inproj_kernel(x_ref, xs_ref, g_ref, w_ref, p_ref, f_ref, ps_ref, fs_ref, h_scr, hs_scr, *, nj, rchunk):
    i = pl.program_id(0)
    j = pl.program_id(1)
    last_i = pl.num_programs(0) - 1

    def rms(x):
        r = lax.rsqrt(jnp.mean(x * x, axis=-1, keepdims=True) + EPS)
        return ((x * r) * g_ref[...]).astype(BF16)

    @pl.when(j == 0)
    def _():
        def body(c, carry):
            r0 = pl.multiple_of(c * rchunk, rchunk)
            h_scr[pl.ds(r0, rchunk), :] = rms(x_ref[pl.ds(r0, rchunk), :])
            return carry

        lax.fori_loop(0, x_ref.shape[0] // rchunk, body, 0)

        @pl.when(i == last_i)
        def _():
            hs_scr[...] = rms(xs_ref[...])

    dims = (((1,), (1,)), ((), ()))
    acc = lax.dot_general(h_scr[...], w_ref[...], dims, preferred_element_type=F32)
    p_ref[...] = acc.astype(p_ref.dtype)

    @pl.when(j == nj - 1)
    def _():
        f_ref[...] = acc[:, SM_LOCAL:SM_LOCAL + SM_WIDTH]

    @pl.when(i == last_i)
    def _():
        acc_s = lax.dot_general(hs_scr[...], w_ref[...], dims, preferred_element_type=F32)
        ps_ref[...] = acc_s

        @pl.when(j == nj - 1)
        def _():
            fs_ref[...] = acc_s[:, SM_LOCAL:SM_LOCAL + SM_WIDTH]


def _inproj(x2, xs2, norm_g, w_packed, layer):
    rows = x2.shape[0]
    rows_s = xs2.shape[0]
    tm = _pick_tile(rows, 1024)
    ni = rows // tm
    nj = NP // PROJ_TN
    assert SM_TILE == nj - 1
    rchunk = _pick_tile(tm, 128)
    return pl.pallas_call(
        functools.partial(_inproj_kernel, nj=nj, rchunk=rchunk),
        grid=(ni, nj),
        in_specs=[
            pl.BlockSpec((tm, D_MODEL), lambda i, j: (i, 0)),
            pl.BlockSpec((rows_s, D_MODEL), lambda i, j: (0, 0)),
            pl.BlockSpec((1, D_MODEL), lambda i, j: (0, 0)),
            pl.BlockSpec((None, PROJ_TN, D_MODEL), lambda i, j: (layer, j, 0)),
        ],
        out_specs=[
            pl.BlockSpec((tm, PROJ_TN), lambda i, j: (i, j)),
            pl.BlockSpec((tm, SM_WIDTH), lambda i, j: (i, 0)),
            pl.BlockSpec((rows_s, PROJ_TN), lambda i, j: (0, jnp.where(i == ni - 1, j, 0))),
            pl.BlockSpec((rows_s, SM_WIDTH), lambda i, j: (0, 0)),
        ],
        out_shape=[
            jax.ShapeDtypeStruct((rows, NP), BF16),
            jax.ShapeDtypeStruct((rows, SM_WIDTH), F32),
            jax.ShapeDtypeStruct((rows_s, NP), F32),
            jax.ShapeDtypeStruct((rows_s, SM_WIDTH), F32),
        ],
        scratch_shapes=[pltpu.VMEM((tm, D_MODEL), BF16), pltpu.VMEM((rows_s, D_MODEL), BF16)],
        compiler_params=_cparams("arbitrary", "arbitrary"),
        name="inproj",
    )(x2, xs2, norm_g.reshape(1, D_MODEL), w_packed)


def _pool_kernel(u_ref, g_ref, halo_ref, pw_ref, sc_ref, o_ref, band_scr, *, tiles_per_seq, decode):
    i = pl.program_id(0)
    tr = u_ref.shape[0]
    hr = halo_ref.shape[0]
    mm_dt = F32 if decode else BF16
    u = u_ref[...].astype(F32)
    halo = halo_ref[...].astype(F32)
    if not decode:
        halo = jnp.where(i % tiles_per_seq == 0, 0.0, halo)
    ext = jnp.concatenate([halo, u], axis=0).astype(mm_dt)

    @pl.when(i == 0)
    def _():
        r = _iota((tr, hr + tr), 0)
        c = _iota((tr, hr + tr), 1) - hr
        for g, w in enumerate(A_WINDOWS):
            band_scr[g] = ((c >= r - (w - 1)) & (c <= r)).astype(mm_dt)

    rowpos = _iota((tr, 1), 0)
    pos = ((i % tiles_per_seq) * tr + rowpos).astype(F32)
    outs = []
    for g, w in enumerate(A_WINDOWS):
        lo, hi = g * A_GROUP_DIM, (g + 1) * A_GROUP_DIM
        band = band_scr[g]
        if decode:
            win_sum = jnp.dot(band, ext[:, lo:hi], preferred_element_type=F32, precision=HIGHEST)
            count = float(w)
        else:
            win_sum = jnp.dot(band, ext[:, lo:hi], preferred_element_type=F32)
            count = jnp.minimum(pos + 1.0, float(w))
        pooled = win_sum / count - u[:, lo:hi]
        outs.append(jnp.dot(pooled.astype(mm_dt), pw_ref[g].astype(mm_dt), preferred_element_type=F32))
    mixed = jnp.concatenate(outs, axis=1) * sc_ref[...]
    o_ref[...] = (mixed * _silu(g_ref[...].astype(F32))).astype(o_ref.dtype)


def _pool_branch(p_all, pool_w, pool_scale, *, seq_len, state=None):
    rows = p_all.shape[0]
    decode = state is not None
    wdt = F32 if decode else BF16
    if decode:
        tr, tps = seq_len, 1
        hb = 16
        halo_spec = pl.BlockSpec((None, hb, A_WIDTH), lambda i: (i, 0, 0))
        halo_arr = state
    else:
        tr = _pick_tile(seq_len, 512, LANES)
        tps = seq_len // tr
        hb = LANES
        assert tr % hb == 0
        halo_spec = pl.BlockSpec((hb, A_WIDTH), lambda i: (jnp.maximum(i * (tr // hb) - 1, 0), OFF_AU // A_WIDTH))
        halo_arr = p_all
    return pl.pallas_call(
        functools.partial(_pool_kernel, tiles_per_seq=tps, decode=decode),
        grid=(rows // tr,),
        in_specs=[
            pl.BlockSpec((tr, A_WIDTH), lambda i: (i, OFF_AU // A_WIDTH)),
            pl.BlockSpec((tr, A_WIDTH), lambda i: (i, OFF_AG // A_WIDTH)),
            halo_spec,
            pl.BlockSpec((A_GROUPS, A_GROUP_DIM, A_GROUP_DIM), lambda i: (0, 0, 0)),
            pl.BlockSpec((1, A_WIDTH), lambda i: (0, 0)),
        ],
        out_specs=pl.BlockSpec((tr, A_WIDTH), lambda i: (i, 0)),
        out_shape=jax.ShapeDtypeStruct((rows, A_WIDTH), p_all.dtype),
        scratch_shapes=[pltpu.VMEM((A_GROUPS, tr, hb + tr), wdt)],
        compiler_params=_cparams("arbitrary"),
        name="pool_decode" if decode else "pool_prompt",
    )(p_all, p_all, halo_arr, pool_w.astype(wdt), pool_scale.reshape(1, A_WIDTH))


def _gla_kernel(q_ref, k_ref, v_ref, g_ref, r_ref, wa_ref, ba_ref, gn_ref, s0_ref,
                o_ref, so_ref, s_scr, *, chunk, span, decode):
    tb = pl.program_id(1)
    nb = pl.num_programs(1)
    tbk = q_ref.shape[0]
    mm_dt = F32 if decode else BF16

    @pl.when(tb == 0)
    def _():
        s_scr[...] = s0_ref[...]

    logit = jnp.dot(r_ref[...], wa_ref[...], preferred_element_type=F32, precision=HIGHEST) + ba_ref[...]
    log_a = _log_sigmoid(logit) / B_GATE_TEMP
    nsub = span // chunk
    tri = (_iota((span, span), 0) >= _iota((span, span), 1))
    scale = B_DK ** -0.5
    kw = log_a.shape[1]
    nt_dims = (((1,), (1,)), ((), ()))
    if not decode:
        la_hi = log_a.astype(BF16)
        la_r1 = log_a - la_hi.astype(F32)
        la_mid = la_r1.astype(BF16)
        la_lo = (la_r1 - la_mid.astype(F32)).astype(BF16)
        la_terms = jnp.concatenate([la_hi, la_mid, la_lo], axis=1)
        tri_b = tri.astype(BF16)
    zero_blk = jnp.zeros((chunk, B_DKP), mm_dt)
    for c in range(tbk // span):
        rs = slice(c * span, (c + 1) * span)
        if decode:
            b = jnp.dot(tri.astype(F32), log_a[rs], preferred_element_type=F32, precision=HIGHEST)
        else:
            b3 = jnp.dot(tri_b, la_terms[rs], preferred_element_type=F32)
            b = (b3[:, :kw] + b3[:, kw:2 * kw]) + b3[:, 2 * kw:]
        ends = [b[(i + 1) * chunk - 1:(i + 1) * chunk, :] for i in range(nsub)]
        starts = [jnp.zeros_like(ends[0])] + ends[:-1]
        b_last = ends[-1]
        rows_of = lambda vecs: jnp.concatenate([jnp.broadcast_to(v, (chunk, kw)) for v in vecs], axis=0)
        r_rows = rows_of(starts)
        q_in = (q_ref[rs, :].astype(F32) * scale) * jnp.exp(b - r_rows)
        k_in = k_ref[rs, :].astype(F32) * jnp.exp(r_rows - b)
        q_t = q_in * rows_of([jnp.exp(v) for v in starts])
        k_end = k_in * rows_of([jnp.exp(b_last - v) for v in starts])
        decay = jnp.exp(b_last)
        for h in range(B_HEADS):
            ks = slice(h * B_DKP, (h + 1) * B_DKP)
            vs = slice(h * B_DVP, (h + 1) * B_DVP)
            q_blocks, k_blocks = [], []
            for i in range(nsub):
                qi = q_in[i * chunk:(i + 1) * chunk, ks]
                kj = k_in[i * chunk:(i + 1) * chunk, ks].astype(mm_dt)
                q_blocks.append(jnp.concatenate(
                    [(qi * jnp.exp(starts[i][:, ks] - starts[j][:, ks])).astype(mm_dt) for j in range(i)]
                    + [qi.astype(mm_dt)] + [zero_blk] * (nsub - 1 - i), axis=1))
                k_blocks.append(jnp.concatenate([kj if j == i else zero_blk for j in range(nsub)], axis=1))
            att = lax.dot_general(jnp.concatenate(q_blocks, axis=0), jnp.concatenate(k_blocks, axis=0), nt_dims,
                                  preferred_element_type=F32)
            att = jnp.where(tri, att, 0.0)
            vh = v_ref[rs, vs].astype(mm_dt)
            st = s_scr[h]
            o = jnp.dot(att.astype(mm_dt), vh, preferred_element_type=F32)
            o = o + lax.dot_general(q_t[:, ks].astype(mm_dt), st.astype(mm_dt), nt_dims,
                                    preferred_element_type=F32)
            kv_t = lax.dot_general(vh, k_end[:, ks].astype(mm_dt), (((0,), (0,)), ((), ())),
                                   preferred_element_type=F32)
            s_scr[h] = st * decay[:, ks] + kv_t
            r = lax.rsqrt(jnp.sum(o * o, axis=-1, keepdims=True) * (1.0 / B_DV) + EPS)
            on = (o * r) * gn_ref[:, vs]
            o_ref[rs, vs] = (on * _silu(g_ref[rs, vs].astype(F32))).astype(o_ref.dtype)

    @pl.when(tb == nb - 1)
    def _():
        so_ref[...] = s_scr[...]


def _gla_branch(p_all, f_all, wa_p, ba_p, gn_p, s0_t, *, n_seq, seq_len, decode):
    rows = p_all.shape[0]
    chunk = math.gcd(seq_len, B_CHUNK)
    tbk = _pick_tile(seq_len, 512)
    nb = seq_len // tbk
    span = GLA_SPAN if (not decode and tbk % GLA_SPAN == 0 and GLA_SPAN % chunk == 0) else chunk
    kw = B_HEADS * B_DKP
    vw = B_HEADS * B_DVP
    return pl.pallas_call(
        functools.partial(_gla_kernel, chunk=chunk, span=span, decode=decode),
        grid=(n_seq, nb),
        in_specs=[
            pl.BlockSpec((tbk, kw), lambda s, t: (s * nb + t, OFF_BQ // kw)),
            pl.BlockSpec((tbk, kw), lambda s, t: (s * nb + t, OFF_BK // kw)),
            pl.BlockSpec((tbk, vw), lambda s, t: (s * nb + t, OFF_BV // vw)),
            pl.BlockSpec((tbk, vw), lambda s, t: (s * nb + t, OFF_BG // vw)),
            pl.BlockSpec((tbk, LANES), lambda s, t: (s * nb + t, 0)),
            pl.BlockSpec((LANES, kw), lambda s, t: (0, 0)),
            pl.BlockSpec((1, kw), lambda s, t: (0, 0)),
            pl.BlockSpec((1, vw), lambda s, t: (0, 0)),
            pl.BlockSpec((None, B_HEADS, B_DVP, B_DKP), lambda s, t: (s, 0, 0, 0)),
        ],
        out_specs=[
            pl.BlockSpec((tbk, vw), lambda s, t: (s * nb + t, 0)),
            pl.BlockSpec((None, B_HEADS, B_DVP, B_DKP), lambda s, t: (s, 0, 0, 0)),
        ],
        out_shape=[
            jax.ShapeDtypeStruct((rows, vw), p_all.dtype),
            jax.ShapeDtypeStruct((n_seq, B_HEADS, B_DVP, B_DKP), F32),
        ],
        scratch_shapes=[pltpu.VMEM((B_HEADS, B_DVP, B_DKP), F32)],
        compiler_params=_cparams("arbitrary", "arbitrary"),
        name="gla_decode" if decode else "gla_prompt",
    )(p_all, p_all, p_all, p_all, f_all, wa_p, ba_p, gn_p, s0_t)


def _foxprep_kernel(cq_ref, ck_ref, cv_ref, cf_ref, qn_ref, kn_ref, bf_ref, *rest, tiles_per_seq, first_layer):
    if first_layer:
        qa_ref, ka_ref, ko_ref, vo_ref, lf_ref, carry_scr = rest
    else:
        qa_ref, ka_ref, ko_ref, vo_ref, lf_ref, carry_scr = rest[2:]
    i = pl.program_id(0)
    tr = cq_ref.shape[0]
    hd = C_HEAD_DIM
    lane = _iota((tr, LANES), 1)

    lf = jnp.where(lane < C_HEADS, _log_sigmoid(cf_ref[...] + bf_ref[...]), 0.0)
    lf_ref[...] = lf

    @pl.when(i % tiles_per_seq == 0)
    def _():
        carry_scr[...] = jnp.zeros_like(carry_scr)

    tri = _iota((tr, tr), 0) >= _iota((tr, tr), 1)
    if tr % LANES == 0:
        hi = lf.astype(BF16)
        r1 = lf - hi.astype(F32)
        mid = r1.astype(BF16)
        lo = (r1 - mid.astype(F32)).astype(BF16)
        c3 = jnp.dot(tri.astype(BF16), jnp.concatenate([hi, mid, lo], axis=1), preferred_element_type=F32)
        cum = (c3[:, :LANES] + c3[:, LANES:2 * LANES]) + c3[:, 2 * LANES:]
    else:
        cum = jnp.dot(tri.astype(F32), lf, preferred_element_type=F32, precision=HIGHEST)
    cum = cum + carry_scr[...]
    carry_scr[...] = cum[tr - 1:tr, :]

    def qk_norm(x, g_ref):
        r = lax.rsqrt(jnp.mean(x * x, axis=-1, keepdims=True) + EPS)
        return (x * r) * g_ref[...]

    one = jnp.ones((tr, LANES), F32)
    zero = jnp.zeros((tr, LANES), F32)
    for h in range(C_HEADS):
        cs = slice(h * hd, (h + 1) * hd)
        ch = jnp.sum(jnp.where(lane == h, cum, 0.0), axis=-1, keepdims=True)
        hi = ch.astype(BF16).astype(F32)
        r1 = ch - hi
        mid = r1.astype(BF16).astype(F32)
        lo = r1 - mid
        qn = qk_norm(cq_ref[:, cs].astype(F32), qn_ref)
        kn = qk_norm(ck_ref[:, cs].astype(F32), kn_ref)
        aux_q = jnp.where(lane == 0, hi, jnp.where(lane == 1, mid, jnp.where(lane == 2, lo,
                          jnp.where(lane < 6, one, zero))))
        aux_k = jnp.where(lane < 3, one, jnp.where(lane == 3, -hi, jnp.where(lane == 4, -mid,
                          jnp.where(lane == 5, -lo, zero))))
        a0 = 2 * h * hd
        qa_ref[:, a0:a0 + hd] = (qn * (hd ** -0.5)).astype(qa_ref.dtype)
        qa_ref[:, a0 + hd:a0 + 2 * hd] = aux_q.astype(qa_ref.dtype)
        ka_ref[:, a0:a0 + hd] = kn.astype(ka_ref.dtype)
        ka_ref[:, a0 + hd:a0 + 2 * hd] = aux_k.astype(ka_ref.dtype)
        vn = cv_ref[:, cs].astype(F32)
        if first_layer:
            for l2 in range(ko_ref.shape[0]):
                ko_ref[l2, h] = kn
                vo_ref[l2, h] = vn
        else:
            ko_ref[h] = kn
            vo_ref[h] = vn


def _fox_prep(p_all, f_all, fox_q_norm, fox_k_norm, bf_p, k_stack, v_stack, *, seq_len, layer, depth):
    rows = p_all.shape[0]
    n_seq = rows // seq_len
    tr = _pick_tile(seq_len, 512)
    tps = seq_len // tr
    hd = C_HEAD_DIM
    cw = C_WIDTH
    aw = C_HEADS * 2 * hd
    first = k_stack is None
    stack_shape = (depth, n_seq, C_HEADS, seq_len, hd)
    if first:
        kv_spec = pl.BlockSpec((depth, None, C_HEADS, tr, hd), lambda i: (0, i // tps, 0, i % tps, 0))
        extra_specs, extra_args, aliases = [], [], {}
    else:
        kv_spec = pl.BlockSpec((None, None, C_HEADS, tr, hd), lambda i: (layer, i // tps, 0, i % tps, 0))
        extra_specs = [pl.BlockSpec(memory_space=pl.ANY)] * 2
        extra_args, aliases = [k_stack, v_stack], {7: 2, 8: 3}
    return pl.pallas_call(
        functools.partial(_foxprep_kernel, tiles_per_seq=tps, first_layer=first),
        grid=(rows // tr,),
        in_specs=[
            pl.BlockSpec((tr, cw), lambda i: (i, OFF_CQ // cw)),
            pl.BlockSpec((tr, cw), lambda i: (i, OFF_CK // cw)),
            pl.BlockSpec((tr, cw), lambda i: (i, OFF_CV // cw)),
            pl.BlockSpec((tr, LANES), lambda i: (i, 1)),
            pl.BlockSpec((1, hd), lambda i: (0, 0)),
            pl.BlockSpec((1, hd), lambda i: (0, 0)),
            pl.BlockSpec((1, LANES), lambda i: (0, 0)),
        ] + extra_specs,
        out_specs=[
            pl.BlockSpec((tr, aw), lambda i: (i, 0)),
            pl.BlockSpec((tr, aw), lambda i: (i, 0)),
            kv_spec,
            kv_spec,
            pl.BlockSpec((tr, LANES), lambda i: (i, 0)),
        ],
        out_shape=[
            jax.ShapeDtypeStruct((rows, aw), p_all.dtype),
            jax.ShapeDtypeStruct((rows, aw), p_all.dtype),
            jax.ShapeDtypeStruct(stack_shape, F32),
            jax.ShapeDtypeStruct(stack_shape, F32),
            jax.ShapeDtypeStruct((rows, LANES), F32),
        ],
        input_output_aliases=aliases,
        scratch_shapes=[pltpu.VMEM((1, LANES), F32)],
        compiler_params=_cparams("arbitrary"),
        name="fox_prep",
    )(p_all, p_all, p_all, f_all, fox_q_norm.reshape(1, hd), fox_k_norm.reshape(1, hd), bf_p, *extra_args)


def _flash_kernel(q_ref, k_ref, v_ref, g_ref, o_ref, m_scr, l_scr, acc_scr, *, rs):
    i = pl.program_id(2)
    t = q_ref.shape[0]
    hd = C_HEAD_DIM
    m_scr[...] = jnp.full_like(m_scr, -jnp.inf)
    l_scr[...] = jnp.zeros_like(l_scr)
    acc_scr[...] = jnp.zeros_like(acc_scr)
    ones = jnp.ones((t, LANES), BF16)

    def step(j, diagonal):
        k0 = pl.multiple_of(j * t, t)
        kb = k_ref[pl.ds(k0, t), :]
        vb = jnp.concatenate([v_ref[pl.ds(k0, t), :], ones], axis=1)
        for r in range(t // rs):
            rows = slice(r * rs, (r + 1) * rs)
            nk = (r + 1) * rs if diagonal else t
            s = lax.dot_general(q_ref[rows, :], kb[:nk], (((1,), (1,)), ((), ())), preferred_element_type=F32)
            if diagonal:
                s = jnp.where(_iota((rs, nk), 0) + r * rs >= _iota((rs, nk), 1), s, -jnp.inf)
            m_prev = m_scr[rows, :]
            m_new = jnp.maximum(m_prev, jnp.max(s, axis=-1, keepdims=True))
            alpha = jnp.exp(m_prev - m_new)
            p = jnp.exp(s - jnp.concatenate([m_new] * (nk // LANES), axis=1))
            pv = jnp.dot(p.astype(BF16), vb[:nk], preferred_element_type=F32)
            l_scr[rows, :] = alpha * l_scr[rows, :] + pv[:, hd:]
            acc_scr[rows, :] = alpha * acc_scr[rows, :] + pv[:, :hd]
            m_scr[rows, :] = m_new

    def pair(jj, carry):
        step(2 * jj, False)
        step(2 * jj + 1, False)
        return carry

    lax.fori_loop(0, i // 2, pair, 0)

    @pl.when(i % 2 == 1)
    def _():
        step(i - 1, False)

    step(i, True)
    out = acc_scr[...] / l_scr[...]
    o_ref[...] = (out * _silu(g_ref[...].astype(F32))).astype(BF16)


def _fox_flash(q_aug, k_aug, p_all, *, n_seq, seq_len):
    rows = q_aug.shape[0]
    t = _pick_tile(seq_len, 1024, LANES)
    rs = _pick_tile(t, 256, LANES)
    nt = seq_len // t
    hd = C_HEAD_DIM
    aw = 2 * hd
    return pl.pallas_call(
        functools.partial(_flash_kernel, rs=rs),
        grid=(n_seq, C_HEADS, nt),
        in_specs=[
            pl.BlockSpec((t, aw), lambda b, h, i: (b * nt + i, h)),
            pl.BlockSpec((seq_len, aw), lambda b, h, i: (b, h)),
            pl.BlockSpec((seq_len, hd), lambda b, h, i: (b, OFF_CV // hd + h)),
            pl.BlockSpec((t, hd), lambda b, h, i: (b * nt + i, OFF_CG // hd + h)),
        ],
        out_specs=pl.BlockSpec((t, hd), lambda b, h, i: (b * nt + i, h)),
        out_shape=jax.ShapeDtypeStruct((rows, C_WIDTH), BF16),
        scratch_shapes=[pltpu.VMEM((t, LANES), F32), pltpu.VMEM((t, LANES), F32), pltpu.VMEM((t, hd), F32)],
        compiler_params=_cparams("parallel", "parallel", "arbitrary"),
        name="fox_flash",
    )(q_aug, k_aug, p_all, p_all)


def _decbias_kernel(pt_ref, lf_ref, lfn_ref, ck_ref, ckn_ref, g_scr):
    s = pl.program_id(0)
    npg = g_scr.shape[1]

    def gather(p, carry):
        page = pt_ref[s, p]
        for h in range(C_HEADS):
            g_scr[h, pl.ds(p, 1), :] = lf_ref[h, pl.ds(page, 1), :]
        return carry

    lax.fori_loop(0, npg, gather, 0)
    upper = (_iota((LANES, LANES), 0) <= _iota((LANES, LANES), 1)).astype(F32)
    earlier = (_iota((npg, npg), 1) < _iota((npg, npg), 0)).astype(F32)
    lfn_cum = jnp.dot(lfn_ref[...], upper, preferred_element_type=F32, precision=HIGHEST)
    ckn_ref[...] = jnp.zeros_like(ckn_ref)
    for h in range(C_HEADS):
        cum = jnp.dot(g_scr[h], upper, preferred_element_type=F32, precision=HIGHEST)
        tot = jnp.broadcast_to(cum[:, LANES - 1:LANES], cum.shape)
        before = jnp.dot(earlier, tot, preferred_element_type=F32, precision=HIGHEST)
        ck_ref[h] = cum + before
        ckn_ref[h:h + 1, :] = lfn_cum[h:h + 1, :] + (before[npg - 1:npg, :] + tot[npg - 1:npg, :])


def _decode_bias(page_table, logf_hp, lfn_t, *, layer):
    n_seq, npg = page_table.shape
    n_pool = logf_hp.shape[2]
    return pl.pallas_call(
        _decbias_kernel,
        grid_spec=pltpu.PrefetchScalarGridSpec(
            num_scalar_prefetch=1,
            grid=(n_seq,),
            in_specs=[
                pl.BlockSpec((None, C_HEADS, n_pool, PAGE_SIZE), lambda s, pt: (layer, 0, 0, 0)),
                pl.BlockSpec((None, SUBLANES, LANES), lambda s, pt: (s, 0, 0)),
            ],
            out_specs=[
                pl.BlockSpec((None, C_HEADS, npg, PAGE_SIZE), lambda s, pt: (s, 0, 0, 0)),
                pl.BlockSpec((None, SUBLANES, LANES), lambda s, pt: (s, 0, 0)),
            ],
            scratch_shapes=[pltpu.VMEM((C_HEADS, npg, PAGE_SIZE), F32)],
        ),
        out_shape=[
            jax.ShapeDtypeStruct((n_seq, C_HEADS, npg, PAGE_SIZE), F32),
            jax.ShapeDtypeStruct((n_seq, SUBLANES, LANES), F32),
        ],
        compiler_params=_cparams("arbitrary"),
        name="decode_bias",
    )(page_table, logf_hp, lfn_t)


DEC_QROWS = 16


def _decode_kernel(pt_ref, qa_ref, *refs, gpp, dec_seq):
    del pt_ref
    k_refs = refs[:gpp]
    v_refs = refs[gpp:2 * gpp]
    ck_ref, ckn_ref, kn_ref, vn_ref, g_ref, o_ref, q_scr, m_scr, l_scr, acc_scr = refs[2 * gpp:]
    t = pl.program_id(1)
    nt = pl.num_programs(1)
    nq = dec_seq
    qp = DEC_QROWS
    nrow = C_HEADS * qp
    hd = C_HEAD_DIM
    row = _iota((nrow, LANES), 0)
    lane = _iota((nrow, LANES), 1)
    qi = jnp.bitwise_and(row, qp - 1)

    @pl.when(t == 0)
    def _():
        qa = qa_ref[...].astype(F32)
        zero = jnp.zeros((qp - nq, hd), F32)
        q_scr[...] = jnp.concatenate(
            [jnp.concatenate([qa[:, h * 2 * hd:h * 2 * hd + hd], zero], axis=0) for h in range(C_HEADS)],
            axis=0).astype(BF16)
        m_scr[...] = jnp.full_like(m_scr, -jnp.inf)
        l_scr[...] = jnp.zeros_like(l_scr)
        acc_scr[...] = jnp.zeros_like(acc_scr)

    def head_rows(get_row):
        return jnp.concatenate(
            [jnp.broadcast_to(get_row(h), (qp, get_row(h).shape[1])) for h in range(C_HEADS)], axis=0)

    ckn_rows = head_rows(lambda h: ckn_ref[h:h + 1, :])
    c_q = jnp.sum(jnp.where(lane == qi, ckn_rows, 0.0), axis=-1, keepdims=True)

    def attend(k_of, v_of, ck_rows, mask):
        s = jnp.concatenate(
            [lax.dot_general(q_scr[h * qp:(h + 1) * qp, :], k_of(h), (((1,), (1,)), ((), ())),
                             preferred_element_type=F32) for h in range(C_HEADS)], axis=0)
        s = s + (c_q - ck_rows)
        if mask is not None:
            s = jnp.where(mask, s, -jnp.inf)
        m_prev = m_scr[...]
        m_new = jnp.maximum(m_prev, jnp.max(s, axis=-1, keepdims=True))
        alpha = jnp.exp(m_prev - m_new)
        pr = jnp.exp(s - m_new)
        l_scr[...] = alpha * l_scr[...] + jnp.sum(pr, axis=-1, keepdims=True)
        prb = pr.astype(BF16)
        o = jnp.concatenate(
            [jnp.dot(prb[h * qp:(h + 1) * qp, :], v_of(h), preferred_element_type=F32) for h in range(C_HEADS)],
            axis=0)
        acc_scr[...] = alpha * acc_scr[...] + o
        m_scr[...] = m_new

    attend(lambda h: jnp.concatenate([k_refs[g][h] for g in range(gpp)], axis=0).astype(BF16),
           lambda h: jnp.concatenate([v_refs[g][h] for g in range(gpp)], axis=0).astype(BF16),
           head_rows(lambda h: jnp.concatenate([ck_ref[h, g:g + 1, :] for g in range(gpp)], axis=1)), None)

    @pl.when(t == nt - 1)
    def _():
        pad = jnp.zeros((PAGE_SIZE - nq, hd), F32)
        attend(lambda h: jnp.concatenate([kn_ref[h], pad], axis=0).astype(BF16),
               lambda h: jnp.concatenate([vn_ref[h], pad], axis=0).astype(BF16),
               ckn_rows, lane <= qi)
        out = acc_scr[...] / l_scr[...]
        gate = _silu(g_ref[...].astype(F32))
        for h in range(C_HEADS):
            cs = slice(h * hd, (h + 1) * hd)
            o_ref[:, cs] = (out[h * qp:h * qp + nq, :] * gate[:, cs]).astype(o_ref.dtype)


def _fox_decode(page_table, q_aug, cache_k_hm, cache_v_hm, ck_past, ck_new, k_new, v_new, p_all, *, layer, dec_seq):
    n_seq, npg = page_table.shape
    rows = q_aug.shape[0]
    nrow = C_HEADS * DEC_QROWS
    gpp = _pick_tile(npg, 16, SUBLANES)

    def page_spec(g):
        return pl.BlockSpec((None, None, C_HEADS, PAGE_SIZE, C_HEAD_DIM),
                            lambda s, t, pt: (layer, pt[s, t * gpp + g], 0, 0, 0))

    return pl.pallas_call(
        functools.partial(_decode_kernel, gpp=gpp, dec_seq=dec_seq),
        grid_spec=pltpu.PrefetchScalarGridSpec(
            num_scalar_prefetch=1,
            grid=(n_seq, npg // gpp),
            in_specs=[pl.BlockSpec((dec_seq, C_HEADS * 2 * C_HEAD_DIM), lambda s, t, pt: (s, 0))]
            + [page_spec(g) for g in range(gpp)]
            + [page_spec(g) for g in range(gpp)]
            + [
                pl.BlockSpec((None, C_HEADS, gpp, PAGE_SIZE), lambda s, t, pt: (s, 0, t, 0)),
                pl.BlockSpec((None, SUBLANES, LANES), lambda s, t, pt: (s, 0, 0)),
                pl.BlockSpec((None, C_HEADS, dec_seq, C_HEAD_DIM), lambda s, t, pt: (s, 0, 0, 0)),
                pl.BlockSpec((None, C_HEADS, dec_seq, C_HEAD_DIM), lambda s, t, pt: (s, 0, 0, 0)),
                pl.BlockSpec((dec_seq, C_WIDTH), lambda s, t, pt: (s, OFF_CG // C_WIDTH)),
            ],
            out_specs=pl.BlockSpec((dec_seq, C_WIDTH), lambda s, t, pt: (s, 0)),
            scratch_shapes=[
                pltpu.VMEM((nrow, C_HEAD_DIM), BF16),
                pltpu.VMEM((nrow, 1), F32),
                pltpu.VMEM((nrow, 1), F32),
                pltpu.VMEM((nrow, C_HEAD_DIM), F32),
            ],
        ),
        out_shape=jax.ShapeDtypeStruct((rows, C_WIDTH), p_all.dtype),
        compiler_params=_cparams("arbitrary", "arbitrary"),
        name="fox_decode",
    )(page_table, q_aug, *([cache_k_hm] * gpp), *([cache_v_hm] * gpp), ck_past, ck_new, k_new, v_new, p_all)


def _merge_kernel(x_ref, a_ref, b_ref, c_ref, mg0_ref, mg1_ref, mg2_ref,
                  xs_ref, as_ref, bs_ref, cs_ref, mgs0_ref, mgs1_ref, mgs2_ref,
                  bm_ref, wa_ref, wb_ref, wc_ref, wo_ref, y_ref, ys_ref):
    d = D_MODEL

    def gated(o_ref, w_ref, mg_ref, k):
        gate = jax.nn.sigmoid(mg_ref[...].astype(F32) + bm_ref[:, k * d:(k + 1) * d])
        return gate * jnp.dot(o_ref[...].astype(BF16), w_ref[...], preferred_element_type=F32)

    def layer_out(xr, ar, br, cr, m0, m1, m2):
        merged = gated(ar, wa_ref, m0, 0) + gated(br, wb_ref, m1, 1) + gated(cr, wc_ref, m2, 2)
        return xr[...] + jnp.dot(merged.astype(BF16), wo_ref[...], preferred_element_type=F32)

    y_ref[...] = layer_out(x_ref, a_ref, b_ref, c_ref, mg0_ref, mg1_ref, mg2_ref)

    @pl.when(pl.program_id(0) == pl.num_programs(0) - 1)
    def _():
        ys_ref[...] = layer_out(xs_ref, as_ref, bs_ref, cs_ref, mgs0_ref, mgs1_ref, mgs2_ref)


def _merge(x2, a_o, b_o, c_o, p_all, xs2, a_s, b_s, c_s, p_s, b_merge, wa, wb, wc, wo):
    rows = x2.shape[0]
    rows_s = xs2.shape[0]
    tm = _pick_tile(rows, 256)
    d = D_MODEL
    bw = B_HEADS * B_DVP

    def const(shape, col=0):
        return pl.BlockSpec(shape, lambda i: (0, col), pipeline_mode=pl.Buffered(1))

    return pl.pallas_call(
        _merge_kernel,
        grid=(rows // tm,),
        in_specs=[
            pl.BlockSpec((tm, d), lambda i: (i, 0)),
            pl.BlockSpec((tm, A_WIDTH), lambda i: (i, 0)),
            pl.BlockSpec((tm, bw), lambda i: (i, 0)),
            pl.BlockSpec((tm, C_WIDTH), lambda i: (i, 0)),
            pl.BlockSpec((tm, d), lambda i: (i, OFF_MG // d + 0)),
            pl.BlockSpec((tm, d), lambda i: (i, OFF_MG // d + 1)),
            pl.BlockSpec((tm, d), lambda i: (i, OFF_MG // d + 2)),
            const((rows_s, d)),
            const((rows_s, A_WIDTH)),
            const((rows_s, bw)),
            const((rows_s, C_WIDTH)),
            const((rows_s, d), OFF_MG // d + 0),
            const((rows_s, d), OFF_MG // d + 1),
            const((rows_s, d), OFF_MG // d + 2),
            const((1, N_BRANCH * d)),
            const((A_WIDTH, d)),
            const((bw, d)),
            const((C_WIDTH, d)),
            const((d, d)),
        ],
        out_specs=[pl.BlockSpec((tm, d), lambda i: (i, 0)), pl.BlockSpec((rows_s, d), lambda i: (0, 0))],
        out_shape=[jax.ShapeDtypeStruct((rows, d), F32), jax.ShapeDtypeStruct((rows_s, d), F32)],
        compiler_params=_cparams("arbitrary"),
        name="merge_out",
    )(x2, a_o, b_o, c_o, p_all, p_all, p_all, xs2, a_s, b_s, c_s, p_s, p_s, p_s,
      b_merge.reshape(1, N_BRANCH * d), wa, wb, wc, wo)


def _pad_heads(x, n_heads, d, dp, axis):
    shp = x.shape
    x = x.reshape(shp[:axis] + (n_heads, d) + shp[axis + 1:])
    pad = [(0, 0)] * x.ndim
    pad[axis + 1] = (0, dp - d)
    x = jnp.pad(x, pad)
    return x.reshape(shp[:axis] + (n_heads * dp,) + shp[axis + 1:])


def _pack_plan():
    src = {}
    acc = 0
    for name, size in zip(("a_u", "a_g", "b_q", "b_k", "b_v", "b_g", "b_r", "c_q", "c_k", "c_v", "c_g", "c_f", "m_g"),
                          SPLIT_SIZES):
        src[name] = acc
        acc += size
    d_in = acc
    segs = [(OFF_MG, "m_g", 1, 6144, 6144), (OFF_CG, "c_g", 1, C_WIDTH, C_WIDTH), (OFF_CQ, "c_q", 1, C_WIDTH, C_WIDTH),
            (OFF_CK, "c_k", 1, C_WIDTH, C_WIDTH), (OFF_CV, "c_v", 1, C_WIDTH, C_WIDTH),
            (OFF_AU, "a_u", 1, A_WIDTH, A_WIDTH), (OFF_AG, "a_g", 1, A_WIDTH, A_WIDTH),
            (OFF_BQ, "b_q", B_HEADS, B_DK, B_DKP), (OFF_BK, "b_k", B_HEADS, B_DK, B_DKP),
            (OFF_BV, "b_v", B_HEADS, B_DV, B_DVP), (OFF_BG, "b_g", B_HEADS, B_DV, B_DVP),
            (OFF_SM, "b_r", 1, B_GATE_RANK, LANES), (OFF_SM + LANES, "c_f", 1, C_HEADS, LANES)]
    starts = [0] * (NP // LANES)
    nvalid = [0] * (NP // LANES)
    for off, name, heads, dreal, dpad in segs:
        for h in range(heads):
            for b in range(dpad // LANES):
                blk = (off + h * dpad) // LANES + b
                n = max(0, min(LANES, dreal - b * LANES))
                s = src[name] + h * dreal + b * LANES
                assert s + LANES <= d_in
                starts[blk], nvalid[blk] = (s if n else 0), n
    return starts, nvalid


PACK_WINDOWS = 4


def _packw_kernel(st_ref, nv_ref, *refs):
    del st_ref
    w_refs, o_ref = refs[:PACK_WINDOWS], refs[PACK_WINDOWS]
    j = pl.program_id(0)
    row = _iota((LANES, D_MODEL), 0)
    for g in range(PACK_WINDOWS):
        live = row < nv_ref[j * PACK_WINDOWS + g]
        for l in range(o_ref.shape[0]):
            o_ref[l, g * LANES:(g + 1) * LANES, :] = jnp.where(live, w_refs[g][:, l, :], 0.0).astype(BF16)


def _pack_w_in(w_in):
    depth = w_in.shape[0]
    w_t = jnp.transpose(w_in, (2, 0, 1))
    starts, nvalid = _pack_plan()
    nblk = NP // LANES
    assert nblk % PACK_WINDOWS == 0

    def window(g):
        return pl.BlockSpec((pl.Element(LANES), pl.Element(depth), pl.Element(D_MODEL)),
                            lambda j, st, nv: (st[j * PACK_WINDOWS + g], 0, 0))

    return pl.pallas_call(
        _packw_kernel,
        grid_spec=pltpu.PrefetchScalarGridSpec(
            num_scalar_prefetch=2,
            grid=(nblk // PACK_WINDOWS,),
            in_specs=[window(g) for g in range(PACK_WINDOWS)],
            out_specs=pl.BlockSpec((depth, PACK_WINDOWS * LANES, D_MODEL), lambda j, st, nv: (0, j, 0)),
        ),
        out_shape=jax.ShapeDtypeStruct((depth, NP, D_MODEL), BF16),
        compiler_params=_cparams("parallel"),
        name="pack_w_in",
    )(jnp.asarray(starts, jnp.int32), jnp.asarray(nvalid, jnp.int32), *([w_t] * PACK_WINDOWS))


def _layer_params(l, norm_g, pool_w, pool_scale, gla_w_a2, gla_b_a, gla_norm, fox_b_f, fox_q_norm,
                  fox_k_norm, w_branch_a, w_branch_b, w_branch_c, b_merge, w_out):
    wa2 = _pad_heads(gla_w_a2[l], B_HEADS, B_DK, B_DKP, 1)
    return dict(
        norm_g=norm_g[l],
        pool_w=pool_w[l],
        pool_scale=pool_scale[l],
        wa2=jnp.pad(wa2, ((0, LANES - B_GATE_RANK), (0, 0))),
        ba=_pad_heads(gla_b_a[l].reshape(1, -1), B_HEADS, B_DK, B_DKP, 1),
        gn=_pad_heads(gla_norm[l].reshape(1, -1), B_HEADS, B_DV, B_DVP, 1),
        bf=jnp.pad(fox_b_f[l].reshape(1, -1), ((0, 0), (0, LANES - C_HEADS))),
        qn=fox_q_norm[l],
        kn=fox_k_norm[l],
        wa=w_branch_a[l].astype(BF16),
        wb=_pad_heads(w_branch_b[l], B_HEADS, B_DV, B_DVP, 0).astype(BF16),
        wc=w_branch_c[l].astype(BF16),
        b_merge=b_merge[l],
        wo=w_out[l].astype(BF16),
    )


def _state_to_kernel(s):
    st = jnp.swapaxes(s, -1, -2)
    return jnp.pad(st, ((0, 0), (0, 0), (0, B_DVP - B_DV), (0, B_DKP - B_DK)))


def _state_from_kernel(st):
    return jnp.swapaxes(st[:, :, :B_DV, :B_DK], -1, -2)


def kernel(x_prompt, x_sample, cache_k, cache_v, cache_logf, state_gla, state_pool, page_table, norm_g, w_in, pool_w, pool_scale, gla_w_a2, gla_b_a, gla_norm, fox_b_f, fox_q_norm, fox_k_norm, w_branch_a, w_branch_b, w_branch_c, b_merge, w_out):
    depth = w_in.shape[0]
    bp, seq, d = x_prompt.shape
    n_seq, dec_seq, _ = x_sample.shape
    assert d == D_MODEL and dec_seq == SUBLANES and seq % LANES == 0
    yp = x_prompt.reshape(bp * seq, d)
    ys = x_sample.reshape(n_seq * dec_seq, d)
    cache_k_hm = jnp.transpose(cache_k, (0, 1, 3, 2, 4))
    cache_v_hm = jnp.transpose(cache_v, (0, 1, 3, 2, 4))
    logf_hp = jnp.transpose(cache_logf, (0, 3, 1, 2))
    zero_state = jnp.zeros((bp, B_HEADS, B_DVP, B_DKP), F32)
    w_packed = _pack_w_in(w_in)
    outs = {k: [] for k in ("lp", "gp", "pp", "ls", "gs", "ps")}
    kp_stack = vp_stack = ks_stack = vs_stack = None
    for l in range(depth):
        prm = _layer_params(l, norm_g, pool_w, pool_scale, gla_w_a2, gla_b_a, gla_norm, fox_b_f,
                            fox_q_norm, fox_k_norm, w_branch_a, w_branch_b, w_branch_c, b_merge, w_out)

        p_all, f_all, p_s, f_s = _inproj(yp, ys, prm["norm_g"], w_packed, l)

        a_o = _pool_branch(p_all, prm["pool_w"], prm["pool_scale"], seq_len=seq)
        b_o, s_fin = _gla_branch(p_all, f_all, prm["wa2"], prm["ba"], prm["gn"], zero_state,
                                 n_seq=bp, seq_len=seq, decode=False)
        q_aug, k_aug, kp_stack, vp_stack, lf = _fox_prep(p_all, f_all, prm["qn"], prm["kn"], prm["bf"],
                                                         kp_stack, vp_stack, seq_len=seq, layer=l, depth=depth)
        c_o = _fox_flash(q_aug, k_aug, p_all, n_seq=bp, seq_len=seq)
        outs["lp"].append(lf[:, :C_HEADS].reshape(bp, seq, C_HEADS))
        outs["gp"].append(_state_from_kernel(s_fin))
        a_u = p_all[:, OFF_AU:OFF_AU + A_WIDTH].reshape(bp, seq, A_WIDTH)
        outs["pp"].append(a_u[:, seq - A_BUF:].astype(F32))

        pool_state = jnp.pad(state_pool[l], ((0, 0), (1, 0), (0, 0)))
        a_s = _pool_branch(p_s, prm["pool_w"], prm["pool_scale"], seq_len=dec_seq, state=pool_state)
        b_s, s_fin_s = _gla_branch(p_s, f_s, prm["wa2"], prm["ba"], prm["gn"], _state_to_kernel(state_gla[l]),
                                   n_seq=n_seq, seq_len=dec_seq, decode=True)
        q_aug_s, _, ks_stack, vs_stack, lf_s = _fox_prep(p_s, f_s, prm["qn"], prm["kn"], prm["bf"],
                                                         ks_stack, vs_stack, seq_len=dec_seq, layer=l, depth=depth)
        k_ns, v_ns = ks_stack[l], vs_stack[l]
        lfn_t = jnp.swapaxes(lf_s.reshape(n_seq, dec_seq, LANES)[:, :, :SUBLANES], 1, 2)
        lfn_t = jnp.pad(lfn_t, ((0, 0), (0, 0), (0, LANES - dec_seq)))
        ck_past, ck_new = _decode_bias(page_table, logf_hp, lfn_t, layer=l)
        c_s = _fox_decode(page_table, q_aug_s, cache_k_hm, cache_v_hm, ck_past, ck_new, k_ns, v_ns, p_s,
                          layer=l, dec_seq=dec_seq)
        yp, ys = _merge(yp, a_o, b_o, c_o, p_all, ys, a_s, b_s, c_s, p_s,
                        prm["b_merge"], prm["wa"], prm["wb"], prm["wc"], prm["wo"])
        outs["ls"].append(lf_s[:, :C_HEADS].reshape(n_seq, dec_seq, C_HEADS))
        outs["gs"].append(_state_from_kernel(s_fin_s))
        a_us = p_s[:, OFF_AU:OFF_AU + A_WIDTH].reshape(n_seq, dec_seq, A_WIDTH).astype(F32)
        outs["ps"].append(jnp.concatenate([state_pool[l], a_us], axis=1)[:, -A_BUF:])

    st = lambda k: jnp.stack(outs[k])
    hm = lambda a: jnp.swapaxes(a, 2, 3)
    return (yp.reshape(bp, seq, d), ys.reshape(n_seq, dec_seq, d),
            hm(kp_stack), hm(vp_stack), st("lp"), st("gp"), st("pp"),
            hm(ks_stack), hm(vs_stack), st("ls"), st("gs"), st("ps"))
```

```python
import functools
import math

import jax
import jax.numpy as jnp
from jax import lax
from jax.experimental import pallas as pl
from jax.experimental.pallas import tpu as pltpu

F32 = jnp.float32
BF16 = jnp.bfloat16
HIGHEST = lax.Precision.HIGHEST

D_MODEL = 2048
A_WIDTH = 512
A_WINDOWS = (2, 4, 8, 16)
A_GROUPS = 4
A_GROUP_DIM = 128
A_BUF = 15
B_HEADS = 4
B_DK = 96
B_DV = 192
B_GATE_RANK = 16
B_GATE_TEMP = 16.0
B_CHUNK = 32
C_HEADS = 6
C_HEAD_DIM = 128
C_WIDTH = 768
N_BRANCH = 3
EPS = 1e-6
PAGE_SIZE = 128
SPLIT_SIZES = (512, 512, 384, 384, 768, 768, 16, 768, 768, 768, 768, 6, 6144)

LANES = 128
SUBLANES = 8
VMEM_LIMIT_BYTES = 56 * 1024 * 1024

B_DKP = 128
B_DVP = 256
GLA_SPAN = 128

OFF_MG = 0
OFF_CG = 6144
OFF_CQ = OFF_CG + C_WIDTH
OFF_CK = OFF_CQ + C_WIDTH
OFF_CV = OFF_CK + C_WIDTH
OFF_AU = OFF_CV + C_WIDTH
OFF_AG = OFF_AU + A_WIDTH
OFF_BQ = OFF_AG + A_WIDTH
OFF_BK = OFF_BQ + B_HEADS * B_DKP
OFF_BV = OFF_BK + B_HEADS * B_DKP
OFF_BG = OFF_BV + B_HEADS * B_DVP
OFF_SM = OFF_BG + B_HEADS * B_DVP
SM_WIDTH = 2 * LANES
PROJ_TN = 1536
NP = 13824
assert OFF_SM + SM_WIDTH <= NP and NP % PROJ_TN == 0
SM_TILE = OFF_SM // PROJ_TN
SM_LOCAL = OFF_SM - SM_TILE * PROJ_TN
assert SM_LOCAL + SM_WIDTH <= PROJ_TN


def _pick_tile(n, pref, align=SUBLANES):
    if n <= pref:
        return n
    t = (pref // align) * align
    while t > align and n % t:
        t -= align
    assert n % t == 0, (n, pref)
    return t


def _cparams(*sem):
    return pltpu.CompilerParams(dimension_semantics=sem, vmem_limit_bytes=VMEM_LIMIT_BYTES)


def _log_sigmoid(x):
    return jnp.minimum(x, 0.0) - jnp.log1p(jnp.exp(-jnp.abs(x)))


def _silu(x):
    return x * jax.nn.sigmoid(x)


def _iota(shape, dim):
    return lax.broadcasted_iota(jnp.int32, shape, dim)


def _inproj_kernel(x_ref, xs_ref, g_ref, w_ref, p_ref, f_ref, ps_ref, fs_ref, h_scr, hs_scr, *, nj, rchunk):
    i = pl.program_id(0)
    j = pl.program_id(1)
    last_i = pl.num_programs(0) - 1

    def rms(x):
        r = lax.rsqrt(jnp.mean(x * x, axis=-1, keepdims=True) + EPS)
        return ((x * r) * g_ref[...]).astype(BF16)

    @pl.when(j == 0)
    def _():
        def body(c, carry):
            r0 = pl.multiple_of(c * rchunk, rchunk)
            h_scr[pl.ds(r0, rchunk), :] = rms(x_ref[pl.ds(r0, rchunk), :])
            return carry

        lax.fori_loop(0, x_ref.shape[0] // rchunk, body, 0)

        @pl.when(i == last_i)
        def _():
            hs_scr[...] = rms(xs_ref[...])

    dims = (((1,), (1,)), ((), ()))
    acc = lax.dot_general(h_scr[...], w_ref[...], dims, preferred_element_type=F32)
    p_ref[...] = acc.astype(p_ref.dtype)

    @pl.when(j == nj - 1)
    def _():
        f_ref[...] = acc[:, SM_LOCAL:SM_LOCAL + SM_WIDTH]

    @pl.when(i == last_i)
    def _():
        acc_s = lax.dot_general(hs_scr[...], w_ref[...], dims, preferred_element_type=F32)
        ps_ref[...] = acc_s

        @pl.when(j == nj - 1)
        def _():
            fs_ref[...] = acc_s[:, SM_LOCAL:SM_LOCAL + SM_WIDTH]


def _inproj(x2, xs2, norm_g, w_packed, layer):
    rows = x2.shape[0]
    rows_s = xs2.shape[0]
    tm = _pick_tile(rows, 1024)
    ni = rows // tm
    nj = NP // PROJ_TN
    assert SM_TILE == nj - 1
    rchunk = _pick_tile(tm, 128)
    return pl.pallas_call(
        functools.partial(_inproj_kernel, nj=nj, rchunk=rchunk),
        grid=(ni, nj),
        in_specs=[
            pl.BlockSpec((tm, D_MODEL), lambda i, j: (i, 0)),
            pl.BlockSpec((rows_s, D_MODEL), lambda i, j: (0, 0)),
            pl.BlockSpec((1, D_MODEL), lambda i, j: (0, 0)),
            pl.BlockSpec((None, PROJ_TN, D_MODEL), lambda i, j: (layer, j, 0)),
        ],
        out_specs=[
            pl.BlockSpec((tm, PROJ_TN), lambda i, j: (i, j)),
            pl.BlockSpec((tm, SM_WIDTH), lambda i, j: (i, 0)),
            pl.BlockSpec((rows_s, PROJ_TN), lambda i, j: (0, jnp.where(i == ni - 1, j, 0))),
            pl.BlockSpec((rows_s, SM_WIDTH), lambda i, j: (0, 0)),
        ],
        out_shape=[
            jax.ShapeDtypeStruct((rows, NP), BF16),
            jax.ShapeDtypeStruct((rows, SM_WIDTH), F32),
            jax.ShapeDtypeStruct((rows_s, NP), F32),
            jax.ShapeDtypeStruct((rows_s, SM_WIDTH), F32),
        ],
        scratch_shapes=[pltpu.VMEM((tm, D_MODEL), BF16), pltpu.VMEM((rows_s, D_MODEL), BF16)],
        compiler_params=_cparams("arbitrary", "arbitrary"),
        name="inproj",
    )(x2, xs2, norm_g.reshape(1, D_MODEL), w_packed)


def _pool_kernel(u_ref, g_ref, halo_ref, pw_ref, sc_ref, o_ref, band_scr, *, tiles_per_seq, decode):
    i = pl.program_id(0)
    tr = u_ref.shape[0]
    hr = halo_ref.shape[0]
    mm_dt = F32 if decode else BF16
    u = u_ref[...].astype(F32)
    halo = halo_ref[...].astype(F32)
    if not decode:
        halo = jnp.where(i % tiles_per_seq == 0, 0.0, halo)
    ext = jnp.concatenate([halo, u], axis=0).astype(mm_dt)

    @pl.when(i == 0)
    def _():
        r = _iota((tr, hr + tr), 0)
        c = _iota((tr, hr + tr), 1) - hr
        for g, w in enumerate(A_WINDOWS):
            band_scr[g] = ((c >= r - (w - 1)) & (c <= r)).astype(mm_dt)

    rowpos = _iota((tr, 1), 0)
    pos = ((i % tiles_per_seq) * tr + rowpos).astype(F32)
    outs = []
    for g, w in enumerate(A_WINDOWS):
        lo, hi = g * A_GROUP_DIM, (g + 1) * A_GROUP_DIM
        band = band_scr[g]
        if decode:
            win_sum = jnp.dot(band, ext[:, lo:hi], preferred_element_type=F32, precision=HIGHEST)
            count = float(w)
        else:
            win_sum = jnp.dot(band, ext[:, lo:hi], preferred_element_type=F32)
            count = jnp.minimum(pos + 1.0, float(w))
        pooled = win_sum / count - u[:, lo:hi]
        outs.append(jnp.dot(pooled.astype(mm_dt), pw_ref[g].astype(mm_dt), preferred_element_type=F32))
    mixed = jnp.concatenate(outs, axis=1) * sc_ref[...]
    o_ref[...] = (mixed * _silu(g_ref[...].astype(F32))).astype(o_ref.dtype)


def _pool_branch(p_all, pool_w, pool_scale, *, seq_len, state=None):
    rows = p_all.shape[0]
    decode = state is not None
    wdt = F32 if decode else BF16
    if decode:
        tr, tps = seq_len, 1
        hb = 16
        halo_spec = pl.BlockSpec((None, hb, A_WIDTH), lambda i: (i, 0, 0))
        halo_arr = state
    else:
        tr = _pick_tile(seq_len, 512, LANES)
        tps = seq_len // tr
        hb = LANES
        assert tr % hb == 0
        halo_spec = pl.BlockSpec((hb, A_WIDTH), lambda i: (jnp.maximum(i * (tr // hb) - 1, 0), OFF_AU // A_WIDTH))
        halo_arr = p_all
    return pl.pallas_call(
        functools.partial(_pool_kernel, tiles_per_seq=tps, decode=decode),
        grid=(rows // tr,),
        in_specs=[
            pl.BlockSpec((tr, A_WIDTH), lambda i: (i, OFF_AU // A_WIDTH)),
            pl.BlockSpec((tr, A_WIDTH), lambda i: (i, OFF_AG // A_WIDTH)),
            halo_spec,
            pl.BlockSpec((A_GROUPS, A_GROUP_DIM, A_GROUP_DIM), lambda i: (0, 0, 0)),
            pl.BlockSpec((1, A_WIDTH), lambda i: (0, 0)),
        ],
        out_specs=pl.BlockSpec((tr, A_WIDTH), lambda i: (i, 0)),
        out_shape=jax.ShapeDtypeStruct((rows, A_WIDTH), p_all.dtype),
        scratch_shapes=[pltpu.VMEM((A_GROUPS, tr, hb + tr), wdt)],
        compiler_params=_cparams("arbitrary"),
        name="pool_decode" if decode else "pool_prompt",
    )(p_all, p_all, halo_arr, pool_w.astype(wdt), pool_scale.reshape(1, A_WIDTH))


def _gla_kernel(q_ref, k_ref, v_ref, g_ref, r_ref, wa_ref, ba_ref, gn_ref, s0_ref,
                o_ref, so_ref, s_scr, *, chunk, span, decode):
    tb = pl.program_id(1)
    nb = pl.num_programs(1)
    tbk = q_ref.shape[0]
    mm_dt = F32 if decode else BF16

    @pl.when(tb == 0)
    def _():
        s_scr[...] = s0_ref[...]

    logit = jnp.dot(r_ref[...], wa_ref[...], preferred_element_type=F32, precision=HIGHEST) + ba_ref[...]
    log_a = _log_sigmoid(logit) / B_GATE_TEMP
    nsub = span // chunk
    tri = (_iota((span, span), 0) >= _iota((span, span), 1))
    scale = B_DK ** -0.5
    kw = log_a.shape[1]
    nt_dims = (((1,), (1,)), ((), ()))
    if not decode:
        la_hi = log_a.astype(BF16)
        la_r1 = log_a - la_hi.astype(F32)
        la_mid = la_r1.astype(BF16)
        la_lo = (la_r1 - la_mid.astype(F32)).astype(BF16)
        la_terms = jnp.concatenate([la_hi, la_mid, la_lo], axis=1)
        tri_b = tri.astype(BF16)
    zero_blk = jnp.zeros((chunk, B_DKP), mm_dt)
    for c in range(tbk // span):
        rs = slice(c * span, (c + 1) * span)
        if decode:
            b = jnp.dot(tri.astype(F32), log_a[rs], preferred_element_type=F32, precision=HIGHEST)
        else:
            b3 = jnp.dot(tri_b, la_terms[rs], preferred_element_type=F32)
            b = (b3[:, :kw] + b3[:, kw:2 * kw]) + b3[:, 2 * kw:]
        ends = [b[(i + 1) * chunk - 1:(i + 1) * chunk, :] for i in range(nsub)]
        starts = [jnp.zeros_like(ends[0])] + ends[:-1]
        b_last = ends[-1]
        rows_of = lambda vecs: jnp.concatenate([jnp.broadcast_to(v, (chunk, kw)) for v in vecs], axis=0)
        r_rows = rows_of(starts)
        q_in = (q_ref[rs, :].astype(F32) * scale) * jnp.exp(b - r_rows)
        k_in = k_ref[rs, :].astype(F32) * jnp.exp(r_rows - b)
        q_t = q_in * rows_of([jnp.exp(v) for v in starts])
        k_end = k_in * rows_of([jnp.exp(b_last - v) for v in starts])
        decay = jnp.exp(b_last)
        for h in range(B_HEADS):
            ks = slice(h * B_DKP, (h + 1) * B_DKP)
            vs = slice(h * B_DVP, (h + 1) * B_DVP)
            q_blocks, k_blocks = [], []
            for i in range(nsub):
                qi = q_in[i * chunk:(i + 1) * chunk, ks]
                kj = k_in[i * chunk:(i + 1) * chunk, ks].astype(mm_dt)
                q_blocks.append(jnp.concatenate(
                    [(qi * jnp.exp(starts[i][:, ks] - starts[j][:, ks])).astype(mm_dt) for j in range(i)]
                    + [qi.astype(mm_dt)] + [zero_blk] * (nsub - 1 - i), axis=1))
                k_blocks.append(jnp.concatenate([kj if j == i else zero_blk for j in range(nsub)], axis=1))
            att = lax.dot_general(jnp.concatenate(q_blocks, axis=0), jnp.concatenate(k_blocks, axis=0), nt_dims,
                                  preferred_element_type=F32)
            att = jnp.where(tri, att, 0.0)
            vh = v_ref[rs, vs].astype(mm_dt)
            st = s_scr[h]
            o = jnp.dot(att.astype(mm_dt), vh, preferred_element_type=F32)
            o = o + lax.dot_general(q_t[:, ks].astype(mm_dt), st.astype(mm_dt), nt_dims,
                                    preferred_element_type=F32)
            kv_t = lax.dot_general(vh, k_end[:, ks].astype(mm_dt), (((0,), (0,)), ((), ())),
                                   preferred_element_type=F32)
            s_scr[h] = st * decay[:, ks] + kv_t
            r = lax.rsqrt(jnp.sum(o * o, axis=-1, keepdims=True) * (1.0 / B_DV) + EPS)
            on = (o * r) * gn_ref[:, vs]
            o_ref[rs, vs] = (on * _silu(g_ref[rs, vs].astype(F32))).astype(o_ref.dtype)

    @pl.when(tb == nb - 1)
    def _():
        so_ref[...] = s_scr[...]


def _gla_branch(p_all, f_all, wa_p, ba_p, gn_p, s0_t, *, n_seq, seq_len, decode):
    rows = p_all.shape[0]
    chunk = math.gcd(seq_len, B_CHUNK)
    tbk = _pick_tile(seq_len, 512)
    nb = seq_len // tbk
    span = GLA_SPAN if (not decode and tbk % GLA_SPAN == 0 and GLA_SPAN % chunk == 0) else chunk
    kw = B_HEADS * B_DKP
    vw = B_HEADS * B_DVP
    return pl.pallas_call(
        functools.partial(_gla_kernel, chunk=chunk, span=span, decode=decode),
        grid=(n_seq, nb),
        in_specs=[
            pl.BlockSpec((tbk, kw), lambda s, t: (s * nb + t, OFF_BQ // kw)),
            pl.BlockSpec((tbk, kw), lambda s, t: (s * nb + t, OFF_BK // kw)),
            pl.BlockSpec((tbk, vw), lambda s, t: (s * nb + t, OFF_BV // vw)),
            pl.BlockSpec((tbk, vw), lambda s, t: (s * nb + t, OFF_BG // vw)),
            pl.BlockSpec((tbk, LANES), lambda s, t: (s * nb + t, 0)),
            pl.BlockSpec((LANES, kw), lambda s, t: (0, 0)),
            pl.BlockSpec((1, kw), lambda s, t: (0, 0)),
            pl.BlockSpec((1, vw), lambda s, t: (0, 0)),
            pl.BlockSpec((None, B_HEADS, B_DVP, B_DKP), lambda s, t: (s, 0, 0, 0)),
        ],
        out_specs=[
            pl.BlockSpec((tbk, vw), lambda s, t: (s * nb + t, 0)),
            pl.BlockSpec((None, B_HEADS, B_DVP, B_DKP), lambda s, t: (s, 0, 0, 0)),
        ],
        out_shape=[
            jax.ShapeDtypeStruct((rows, vw), p_all.dtype),
            jax.ShapeDtypeStruct((n_seq, B_HEADS, B_DVP, B_DKP), F32),
        ],
        scratch_shapes=[pltpu.VMEM((B_HEADS, B_DVP, B_DKP), F32)],
        compiler_params=_cparams("arbitrary", "arbitrary"),
        name="gla_decode" if decode else "gla_prompt",
    )(p_all, p_all, p_all, p_all, f_all, wa_p, ba_p, gn_p, s0_t)


def _foxprep_kernel(cq_ref, ck_ref, cv_ref, cf_ref, qn_ref, kn_ref, bf_ref, *rest, tiles_per_seq, first_layer):
    if first_layer:
        qa_ref, ka_ref, ko_ref, vo_ref, lf_ref, carry_scr = rest
    else:
        qa_ref, ka_ref, ko_ref, vo_ref, lf_ref, carry_scr = rest[2:]
    i = pl.program_id(0)
    tr = cq_ref.shape[0]
    hd = C_HEAD_DIM
    lane = _iota((tr, LANES), 1)

    lf = jnp.where(lane < C_HEADS, _log_sigmoid(cf_ref[...] + bf_ref[...]), 0.0)
    lf_ref[...] = lf

    @pl.when(i % tiles_per_seq == 0)
    def _():
        carry_scr[...] = jnp.zeros_like(carry_scr)

    tri = _iota((tr, tr), 0) >= _iota((tr, tr), 1)
    if tr % LANES == 0:
        hi = lf.astype(BF16)
        r1 = lf - hi.astype(F32)
        mid = r1.astype(BF16)
        lo = (r1 - mid.astype(F32)).astype(BF16)
        c3 = jnp.dot(tri.astype(BF16), jnp.concatenate([hi, mid, lo], axis=1), preferred_element_type=F32)
        cum = (c3[:, :LANES] + c3[:, LANES:2 * LANES]) + c3[:, 2 * LANES:]
    else:
        cum = jnp.dot(tri.astype(F32), lf, preferred_element_type=F32, precision=HIGHEST)
    cum = cum + carry_scr[...]
    carry_scr[...] = cum[tr - 1:tr, :]

    def qk_norm(x, g_ref):
        r = lax.rsqrt(jnp.mean(x * x, axis=-1, keepdims=True) + EPS)
        return (x * r) * g_ref[...]

    one = jnp.ones((tr, LANES), F32)
    zero = jnp.zeros((tr, LANES), F32)
    for h in range(C_HEADS):
        cs = slice(h * hd, (h + 1) * hd)
        ch = jnp.sum(jnp.where(lane == h, cum, 0.0), axis=-1, keepdims=True)
        hi = ch.astype(BF16).astype(F32)
        r1 = ch - hi
        mid = r1.astype(BF16).astype(F32)
        lo = r1 - mid
        qn = qk_norm(cq_ref[:, cs].astype(F32), qn_ref)
        kn = qk_norm(ck_ref[:, cs].astype(F32), kn_ref)
        aux_q = jnp.where(lane == 0, hi, jnp.where(lane == 1, mid, jnp.where(lane == 2, lo,
                          jnp.where(lane < 6, one, zero))))
        aux_k = jnp.where(lane < 3, one, jnp.where(lane == 3, -hi, jnp.where(lane == 4, -mid,
                          jnp.where(lane == 5, -lo, zero))))
        a0 = 2 * h * hd
        qa_ref[:, a0:a0 + hd] = (qn * (hd ** -0.5)).astype(qa_ref.dtype)
        qa_ref[:, a0 + hd:a0 + 2 * hd] = aux_q.astype(qa_ref.dtype)
        ka_ref[:, a0:a0 + hd] = kn.astype(ka_ref.dtype)
        ka_ref[:, a0 + hd:a0 + 2 * hd] = aux_k.astype(ka_ref.dtype)
        vn = cv_ref[:, cs].astype(F32)
        if first_layer:
            for l2 in range(ko_ref.shape[0]):
                ko_ref[l2, h] = kn
                vo_ref[l2, h] = vn
        else:
            ko_ref[h] = kn
            vo_ref[h] = vn


def _fox_prep(p_all, f_all, fox_q_norm, fox_k_norm, bf_p, k_stack, v_stack, *, seq_len, layer, depth):
    rows = p_all.shape[0]
    n_seq = rows // seq_len
    tr = _pick_tile(seq_len, 512)
    tps = seq_len // tr
    hd = C_HEAD_DIM
    cw = C_WIDTH
    aw = C_HEADS * 2 * hd
    first = k_stack is None
    stack_shape = (depth, n_seq, C_HEADS, seq_len, hd)
    if first:
        kv_spec = pl.BlockSpec((depth, None, C_HEADS, tr, hd), lambda i: (0, i // tps, 0, i % tps, 0))
        extra_specs, extra_args, aliases = [], [], {}
    else:
        kv_spec = pl.BlockSpec((None, None, C_HEADS, tr, hd), lambda i: (layer, i // tps, 0, i % tps, 0))
        extra_specs = [pl.BlockSpec(memory_space=pl.ANY)] * 2
        extra_args, aliases = [k_stack, v_stack], {7: 2, 8: 3}
    return pl.pallas_call(
        functools.partial(_foxprep_kernel, tiles_per_seq=tps, first_layer=first),
        grid=(rows // tr,),
        in_specs=[
            pl.BlockSpec((tr, cw), lambda i: (i, OFF_CQ // cw)),
            pl.BlockSpec((tr, cw), lambda i: (i, OFF_CK // cw)),
            pl.BlockSpec((tr, cw), lambda i: (i, OFF_CV // cw)),
            pl.BlockSpec((tr, LANES), lambda i: (i, 1)),
            pl.BlockSpec((1, hd), lambda i: (0, 0)),
            pl.BlockSpec((1, hd), lambda i: (0, 0)),
            pl.BlockSpec((1, LANES), lambda i: (0, 0)),
        ] + extra_specs,
        out_specs=[
            pl.BlockSpec((tr, aw), lambda i: (i, 0)),
            pl.BlockSpec((tr, aw), lambda i: (i, 0)),
            kv_spec,
            kv_spec,
            pl.BlockSpec((tr, LANES), lambda i: (i, 0)),
        ],
        out_shape=[
            jax.ShapeDtypeStruct((rows, aw), p_all.dtype),
            jax.ShapeDtypeStruct((rows, aw), p_all.dtype),
            jax.ShapeDtypeStruct(stack_shape, F32),
            jax.ShapeDtypeStruct(stack_shape, F32),
            jax.ShapeDtypeStruct((rows, LANES), F32),
        ],
        input_output_aliases=aliases,
        scratch_shapes=[pltpu.VMEM((1, LANES), F32)],
        compiler_params=_cparams("arbitrary"),
        name="fox_prep",
    )(p_all, p_all, p_all, f_all, fox_q_norm.reshape(1, hd), fox_k_norm.reshape(1, hd), bf_p, *extra_args)


def _flash_kernel(q_ref, k_ref, v_ref, g_ref, o_ref, m_scr, l_scr, acc_scr, *, rs):
    i = pl.program_id(2)
    t = q_ref.shape[0]
    hd = C_HEAD_DIM
    m_scr[...] = jnp.full_like(m_scr, -jnp.inf)
    l_scr[...] = jnp.zeros_like(l_scr)
    acc_scr[...] = jnp.zeros_like(acc_scr)
    ones = jnp.ones((t, LANES), BF16)

    def step(j, diagonal):
        k0 = pl.multiple_of(j * t, t)
        kb = k_ref[pl.ds(k0, t), :]
        vb = jnp.concatenate([v_ref[pl.ds(k0, t), :], ones], axis=1)
        for r in range(t // rs):
            rows = slice(r * rs, (r + 1) * rs)
            nk = (r + 1) * rs if diagonal else t
            s = lax.dot_general(q_ref[rows, :], kb[:nk], (((1,), (1,)), ((), ())), preferred_element_type=F32)
            if diagonal:
                s = jnp.where(_iota((rs, nk), 0) + r * rs >= _iota((rs, nk), 1), s, -jnp.inf)
            m_prev = m_scr[rows, :]
            m_new = jnp.maximum(m_prev, jnp.max(s, axis=-1, keepdims=True))
            alpha = jnp.exp(m_prev - m_new)
            p = jnp.exp(s - jnp.concatenate([m_new] * (nk // LANES), axis=1))
            pv = jnp.dot(p.astype(BF16), vb[:nk], preferred_element_type=F32)
            l_scr[rows, :] = alpha * l_scr[rows, :] + pv[:, hd:]
            acc_scr[rows, :] = alpha * acc_scr[rows, :] + pv[:, :hd]
            m_scr[rows, :] = m_new

    def pair(jj, carry):
        step(2 * jj, False)
        step(2 * jj + 1, False)
        return carry

    lax.fori_loop(0, i // 2, pair, 0)

    @pl.when(i % 2 == 1)
    def _():
        step(i - 1, False)

    step(i, True)
    out = acc_scr[...] / l_scr[...]
    o_ref[...] = (out * _silu(g_ref[...].astype(F32))).astype(BF16)


def _fox_flash(q_aug, k_aug, p_all, *, n_seq, seq_len):
    rows = q_aug.shape[0]
    t = _pick_tile(seq_len, 1024, LANES)
    rs = _pick_tile(t, 256, LANES)
    nt = seq_len // t
    hd = C_HEAD_DIM
    aw = 2 * hd
    return pl.pallas_call(
        functools.partial(_flash_kernel, rs=rs),
        grid=(n_seq, C_HEADS, nt),
        in_specs=[
            pl.BlockSpec((t, aw), lambda b, h, i: (b * nt + i, h)),
            pl.BlockSpec((seq_len, aw), lambda b, h, i: (b, h)),
            pl.BlockSpec((seq_len, hd), lambda b, h, i: (b, OFF_CV // hd + h)),
            pl.BlockSpec((t, hd), lambda b, h, i: (b * nt + i, OFF_CG // hd + h)),
        ],
        out_specs=pl.BlockSpec((t, hd), lambda b, h, i: (b * nt + i, h)),
        out_shape=jax.ShapeDtypeStruct((rows, C_WIDTH), BF16),
        scratch_shapes=[pltpu.VMEM((t, LANES), F32), pltpu.VMEM((t, LANES), F32), pltpu.VMEM((t, hd), F32)],
        compiler_params=_cparams("parallel", "parallel", "arbitrary"),
        name="fox_flash",
    )(q_aug, k_aug, p_all, p_all)


def _forget_sums(pt_ref, s, lf_ref, lfn_ref, ck_scr, ckn_scr):
    npg = ck_scr.shape[1]

    def gather(p, carry):
        page = pt_ref[s, p]
        for h in range(C_HEADS):
            ck_scr[h, pl.ds(p, 1), :] = lf_ref[h, pl.ds(page, 1), :]
        return carry

    lax.fori_loop(0, npg, gather, 0)
    upper = (_iota((LANES, LANES), 0) <= _iota((LANES, LANES), 1)).astype(F32)
    earlier = (_iota((npg, npg), 1) < _iota((npg, npg), 0)).astype(F32)
    lfn_cum = jnp.dot(lfn_ref[...], upper, preferred_element_type=F32, precision=HIGHEST)
    ckn_scr[...] = jnp.zeros_like(ckn_scr)
    for h in range(C_HEADS):
        cum = jnp.dot(ck_scr[h], upper, preferred_element_type=F32, precision=HIGHEST)
        tot = jnp.broadcast_to(cum[:, LANES - 1:LANES], cum.shape)
        before = jnp.dot(earlier, tot, preferred_element_type=F32, precision=HIGHEST)
        ck_scr[h] = cum + before
        ckn_scr[h:h + 1, :] = lfn_cum[h:h + 1, :] + (before[npg - 1:npg, :] + tot[npg - 1:npg, :])


DEC_QROWS = 16


def _decode_kernel(pt_ref, qa_ref, *refs, gpp, dec_seq):
    k_refs = refs[:gpp]
    v_refs = refs[gpp:2 * gpp]
    (lf_ref, lfn_ref, kn_ref, vn_ref, g_ref, o_ref,
     q_scr, m_scr, l_scr, acc_scr, ck_scr, ckn_scr) = refs[2 * gpp:]
    t = pl.program_id(1)
    nt = pl.num_programs(1)
    nq = dec_seq
    qp = DEC_QROWS
    nrow = C_HEADS * qp
    hd = C_HEAD_DIM
    row = _iota((nrow, LANES), 0)
    lane = _iota((nrow, LANES), 1)
    qi = jnp.bitwise_and(row, qp - 1)

    @pl.when(t == 0)
    def _():
        qa = qa_ref[...].astype(F32)
        zero = jnp.zeros((qp - nq, hd), F32)
        q_scr[...] = jnp.concatenate(
            [jnp.concatenate([qa[:, h * 2 * hd:h * 2 * hd + hd], zero], axis=0) for h in range(C_HEADS)],
            axis=0).astype(BF16)
        m_scr[...] = jnp.full_like(m_scr, -jnp.inf)
        l_scr[...] = jnp.zeros_like(l_scr)
        acc_scr[...] = jnp.zeros_like(acc_scr)
        _forget_sums(pt_ref, pl.program_id(0), lf_ref, lfn_ref, ck_scr, ckn_scr)

    def head_rows(get_row):
        return jnp.concatenate(
            [jnp.broadcast_to(get_row(h), (qp, get_row(h).shape[1])) for h in range(C_HEADS)], axis=0)

    ckn_rows = head_rows(lambda h: ckn_scr[h:h + 1, :])
    page0 = pl.multiple_of(t * gpp, gpp)

    def ck_row(h):
        blk = ck_scr[h, pl.ds(page0, gpp), :]
        return jnp.concatenate([blk[g:g + 1, :] for g in range(gpp)], axis=1)
    c_q = jnp.sum(jnp.where(lane == qi, ckn_rows, 0.0), axis=-1, keepdims=True)

    def attend(k_of, v_of, ck_rows, mask):
        s = jnp.concatenate(
            [lax.dot_general(q_scr[h * qp:(h + 1) * qp, :], k_of(h), (((1,), (1,)), ((), ())),
                             preferred_element_type=F32) for h in range(C_HEADS)], axis=0)
        s = s + (c_q - ck_rows)
        if mask is not None:
            s = jnp.where(mask, s, -jnp.inf)
        m_prev = m_scr[...]
        m_new = jnp.maximum(m_prev, jnp.max(s, axis=-1, keepdims=True))
        alpha = jnp.exp(m_prev - m_new)
        pr = jnp.exp(s - m_new)
        l_scr[...] = alpha * l_scr[...] + jnp.sum(pr, axis=-1, keepdims=True)
        prb = pr.astype(BF16)
        o = jnp.concatenate(
            [jnp.dot(prb[h * qp:(h + 1) * qp, :], v_of(h), preferred_element_type=F32) for h in range(C_HEADS)],
            axis=0)
        acc_scr[...] = alpha * acc_scr[...] + o
        m_scr[...] = m_new

    attend(lambda h: jnp.concatenate([k_refs[g][h] for g in range(gpp)], axis=0).astype(BF16),
           lambda h: jnp.concatenate([v_refs[g][h] for g in range(gpp)], axis=0).astype(BF16),
           head_rows(ck_row), None)

    @pl.when(t == nt - 1)
    def _():
        pad = jnp.zeros((PAGE_SIZE - nq, hd), F32)
        attend(lambda h: jnp.concatenate([kn_ref[h], pad], axis=0).astype(BF16),
               lambda h: jnp.concatenate([vn_ref[h], pad], axis=0).astype(BF16),
               ckn_rows, lane <= qi)
        out = acc_scr[...] / l_scr[...]
        gate = _silu(g_ref[...].astype(F32))
        for h in range(C_HEADS):
            cs = slice(h * hd, (h + 1) * hd)
            o_ref[:, cs] = (out[h * qp:h * qp + nq, :] * gate[:, cs]).astype(o_ref.dtype)


def _fox_decode(page_table, q_aug, cache_k_hm, cache_v_hm, logf_hp, lfn_t, k_new, v_new, p_all, *, layer, dec_seq):
    n_seq, npg = page_table.shape
    n_pool = logf_hp.shape[2]
    rows = q_aug.shape[0]
    nrow = C_HEADS * DEC_QROWS
    gpp = _pick_tile(npg, 16, SUBLANES)

    def page_spec(g):
        return pl.BlockSpec((None, None, C_HEADS, PAGE_SIZE, C_HEAD_DIM),
                            lambda s, t, pt: (layer, pt[s, t * gpp + g], 0, 0, 0))

    return pl.pallas_call(
        functools.partial(_decode_kernel, gpp=gpp, dec_seq=dec_seq),
        grid_spec=pltpu.PrefetchScalarGridSpec(
            num_scalar_prefetch=1,
            grid=(n_seq, npg // gpp),
            in_specs=[pl.BlockSpec((dec_seq, C_HEADS * 2 * C_HEAD_DIM), lambda s, t, pt: (s, 0))]
            + [page_spec(g) for g in range(gpp)]
            + [page_spec(g) for g in range(gpp)]
            + [
                pl.BlockSpec((None, C_HEADS, n_pool, PAGE_SIZE), lambda s, t, pt: (layer, 0, 0, 0)),
                pl.BlockSpec((None, SUBLANES, LANES), lambda s, t, pt: (s, 0, 0)),
                pl.BlockSpec((None, C_HEADS, dec_seq, C_HEAD_DIM), lambda s, t, pt: (s, 0, 0, 0)),
                pl.BlockSpec((None, C_HEADS, dec_seq, C_HEAD_DIM), lambda s, t, pt: (s, 0, 0, 0)),
                pl.BlockSpec((dec_seq, C_WIDTH), lambda s, t, pt: (s, OFF_CG // C_WIDTH)),
            ],
            out_specs=pl.BlockSpec((dec_seq, C_WIDTH), lambda s, t, pt: (s, 0)),
            scratch_shapes=[
                pltpu.VMEM((nrow, C_HEAD_DIM), BF16),
                pltpu.VMEM((nrow, 1), F32),
                pltpu.VMEM((nrow, 1), F32),
                pltpu.VMEM((nrow, C_HEAD_DIM), F32),
                pltpu.VMEM((C_HEADS, npg, PAGE_SIZE), F32),
                pltpu.VMEM((SUBLANES, LANES), F32),
            ],
        ),
        out_shape=jax.ShapeDtypeStruct((rows, C_WIDTH), p_all.dtype),
        compiler_params=_cparams("arbitrary", "arbitrary"),
        name="fox_decode",
    )(page_table, q_aug, *([cache_k_hm] * gpp), *([cache_v_hm] * gpp), logf_hp, lfn_t, k_new, v_new, p_all)


def _merge_kernel(x_ref, a_ref, b_ref, c_ref, mg0_ref, mg1_ref, mg2_ref,
                  xs_ref, as_ref, bs_ref, cs_ref, mgs0_ref, mgs1_ref, mgs2_ref,
                  bm_ref, wa_ref, wb_ref, wc_ref, wo_ref, y_ref, ys_ref):
    d = D_MODEL

    def gated(o_ref, w_ref, mg_ref, k):
        gate = jax.nn.sigmoid(mg_ref[...].astype(F32) + bm_ref[:, k * d:(k + 1) * d])
        return gate * jnp.dot(o_ref[...].astype(BF16), w_ref[...], preferred_element_type=F32)

    def layer_out(xr, ar, br, cr, m0, m1, m2):
        merged = gated(ar, wa_ref, m0, 0) + gated(br, wb_ref, m1, 1) + gated(cr, wc_ref, m2, 2)
        return xr[...] + jnp.dot(merged.astype(BF16), wo_ref[...], preferred_element_type=F32)

    y_ref[...] = layer_out(x_ref, a_ref, b_ref, c_ref, mg0_ref, mg1_ref, mg2_ref)

    @pl.when(pl.program_id(0) == pl.num_programs(0) - 1)
    def _():
        ys_ref[...] = layer_out(xs_ref, as_ref, bs_ref, cs_ref, mgs0_ref, mgs1_ref, mgs2_ref)


def _merge(x2, a_o, b_o, c_o, p_all, xs2, a_s, b_s, c_s, p_s, b_merge, wa, wb, wc, wo):
    rows = x2.shape[0]
    rows_s = xs2.shape[0]
    tm = _pick_tile(rows, 256)
    d = D_MODEL
    bw = B_HEADS * B_DVP

    def const(shape, col=0):
        return pl.BlockSpec(shape, lambda i: (0, col), pipeline_mode=pl.Buffered(1))

    return pl.pallas_call(
        _merge_kernel,
        grid=(rows // tm,),
        in_specs=[
            pl.BlockSpec((tm, d), lambda i: (i, 0)),
            pl.BlockSpec((tm, A_WIDTH), lambda i: (i, 0)),
            pl.BlockSpec((tm, bw), lambda i: (i, 0)),
            pl.BlockSpec((tm, C_WIDTH), lambda i: (i, 0)),
            pl.BlockSpec((tm, d), lambda i: (i, OFF_MG // d + 0)),
            pl.BlockSpec((tm, d), lambda i: (i, OFF_MG // d + 1)),
            pl.BlockSpec((tm, d), lambda i: (i, OFF_MG // d + 2)),
            const((rows_s, d)),
            const((rows_s, A_WIDTH)),
            const((rows_s, bw)),
            const((rows_s, C_WIDTH)),
            const((rows_s, d), OFF_MG // d + 0),
            const((rows_s, d), OFF_MG // d + 1),
            const((rows_s, d), OFF_MG // d + 2),
            const((1, N_BRANCH * d)),
            const((A_WIDTH, d)),
            const((bw, d)),
            const((C_WIDTH, d)),
            const((d, d)),
        ],
        out_specs=[pl.BlockSpec((tm, d), lambda i: (i, 0)), pl.BlockSpec((rows_s, d), lambda i: (0, 0))],
        out_shape=[jax.ShapeDtypeStruct((rows, d), F32), jax.ShapeDtypeStruct((rows_s, d), F32)],
        compiler_params=_cparams("arbitrary"),
        name="merge_out",
    )(x2, a_o, b_o, c_o, p_all, p_all, p_all, xs2, a_s, b_s, c_s, p_s, p_s, p_s,
      b_merge.reshape(1, N_BRANCH * d), wa, wb, wc, wo)


def _pad_heads(x, n_heads, d, dp, axis):
    shp = x.shape
    x = x.reshape(shp[:axis] + (n_heads, d) + shp[axis + 1:])
    pad = [(0, 0)] * x.ndim
    pad[axis + 1] = (0, dp - d)
    x = jnp.pad(x, pad)
    return x.reshape(shp[:axis] + (n_heads * dp,) + shp[axis + 1:])


def _pack_plan():
    src = {}
    acc = 0
    for name, size in zip(("a_u", "a_g", "b_q", "b_k", "b_v", "b_g", "b_r", "c_q", "c_k", "c_v", "c_g", "c_f", "m_g"),
                          SPLIT_SIZES):
        src[name] = acc
        acc += size
    d_in = acc
    segs = [(OFF_MG, "m_g", 1, 6144, 6144), (OFF_CG, "c_g", 1, C_WIDTH, C_WIDTH), (OFF_CQ, "c_q", 1, C_WIDTH, C_WIDTH),
            (OFF_CK, "c_k", 1, C_WIDTH, C_WIDTH), (OFF_CV, "c_v", 1, C_WIDTH, C_WIDTH),
            (OFF_AU, "a_u", 1, A_WIDTH, A_WIDTH), (OFF_AG, "a_g", 1, A_WIDTH, A_WIDTH),
            (OFF_BQ, "b_q", B_HEADS, B_DK, B_DKP), (OFF_BK, "b_k", B_HEADS, B_DK, B_DKP),
            (OFF_BV, "b_v", B_HEADS, B_DV, B_DVP), (OFF_BG, "b_g", B_HEADS, B_DV, B_DVP),
            (OFF_SM, "b_r", 1, B_GATE_RANK, LANES), (OFF_SM + LANES, "c_f", 1, C_HEADS, LANES)]
    starts = [0] * (NP // LANES)
    nvalid = [0] * (NP // LANES)
    for off, name, heads, dreal, dpad in segs:
        for h in range(heads):
            for b in range(dpad // LANES):
                blk = (off + h * dpad) // LANES + b
                n = max(0, min(LANES, dreal - b * LANES))
                s = src[name] + h * dreal + b * LANES
                assert s + LANES <= d_in
                starts[blk], nvalid[blk] = (s if n else 0), n
    return starts, nvalid


PACK_WINDOWS = 4


def _packw_kernel(st_ref, nv_ref, *refs):
    del st_ref
    w_refs, o_ref = refs[:PACK_WINDOWS], refs[PACK_WINDOWS]
    j = pl.program_id(0)
    row = _iota((LANES, D_MODEL), 0)
    for g in range(PACK_WINDOWS):
        live = row < nv_ref[j * PACK_WINDOWS + g]
        for l in range(o_ref.shape[0]):
            o_ref[l, g * LANES:(g + 1) * LANES, :] = jnp.where(live, w_refs[g][:, l, :], 0.0).astype(BF16)


def _pack_w_in(w_in):
    depth = w_in.shape[0]
    w_t = jnp.transpose(w_in, (2, 0, 1))
    starts, nvalid = _pack_plan()
    nblk = NP // LANES
    assert nblk % PACK_WINDOWS == 0

    def window(g):
        return pl.BlockSpec((pl.Element(LANES), pl.Element(depth), pl.Element(D_MODEL)),
                            lambda j, st, nv: (st[j * PACK_WINDOWS + g], 0, 0))

    return pl.pallas_call(
        _packw_kernel,
        grid_spec=pltpu.PrefetchScalarGridSpec(
            num_scalar_prefetch=2,
            grid=(nblk // PACK_WINDOWS,),
            in_specs=[window(g) for g in range(PACK_WINDOWS)],
            out_specs=pl.BlockSpec((depth, PACK_WINDOWS * LANES, D_MODEL), lambda j, st, nv: (0, j, 0)),
        ),
        out_shape=jax.ShapeDtypeStruct((depth, NP, D_MODEL), BF16),
        compiler_params=_cparams("parallel"),
        name="pack_w_in",
    )(jnp.asarray(starts, jnp.int32), jnp.asarray(nvalid, jnp.int32), *([w_t] * PACK_WINDOWS))


def _layer_params(l, norm_g, pool_w, pool_scale, gla_w_a2, gla_b_a, gla_norm, fox_b_f, fox_q_norm,
                  fox_k_norm, w_branch_a, w_branch_b, w_branch_c, b_merge, w_out):
    wa2 = _pad_heads(gla_w_a2[l], B_HEADS, B_DK, B_DKP, 1)
    return dict(
        norm_g=norm_g[l],
        pool_w=pool_w[l],
        pool_scale=pool_scale[l],
        wa2=jnp.pad(wa2, ((0, LANES - B_GATE_RANK), (0, 0))),
        ba=_pad_heads(gla_b_a[l].reshape(1, -1), B_HEADS, B_DK, B_DKP, 1),
        gn=_pad_heads(gla_norm[l].reshape(1, -1), B_HEADS, B_DV, B_DVP, 1),
        bf=jnp.pad(fox_b_f[l].reshape(1, -1), ((0, 0), (0, LANES - C_HEADS))),
        qn=fox_q_norm[l],
        kn=fox_k_norm[l],
        wa=w_branch_a[l].astype(BF16),
        wb=_pad_heads(w_branch_b[l], B_HEADS, B_DV, B_DVP, 0).astype(BF16),
        wc=w_branch_c[l].astype(BF16),
        b_merge=b_merge[l],
        wo=w_out[l].astype(BF16),
    )


def _state_to_kernel(s):
    st = jnp.swapaxes(s, -1, -2)
    return jnp.pad(st, ((0, 0), (0, 0), (0, B_DVP - B_DV), (0, B_DKP - B_DK)))


def _state_from_kernel(st):
    return jnp.swapaxes(st[:, :, :B_DV, :B_DK], -1, -2)


def kernel(x_prompt, x_sample, cache_k, cache_v, cache_logf, state_gla, state_pool, page_table, norm_g, w_in, pool_w, pool_scale, gla_w_a2, gla_b_a, gla_norm, fox_b_f, fox_q_norm, fox_k_norm, w_branch_a, w_branch_b, w_branch_c, b_merge, w_out):
    depth = w_in.shape[0]
    bp, seq, d = x_prompt.shape
    n_seq, dec_seq, _ = x_sample.shape
    assert d == D_MODEL and dec_seq == SUBLANES and seq % LANES == 0
    yp = x_prompt.reshape(bp * seq, d)
    ys = x_sample.reshape(n_seq * dec_seq, d)
    cache_k_hm = jnp.transpose(cache_k, (0, 1, 3, 2, 4))
    cache_v_hm = jnp.transpose(cache_v, (0, 1, 3, 2, 4))
    logf_hp = jnp.transpose(cache_logf, (0, 3, 1, 2))
    zero_state = jnp.zeros((bp, B_HEADS, B_DVP, B_DKP), F32)
    w_packed = _pack_w_in(w_in)
    outs = {k: [] for k in ("lp", "gp", "pp", "ls", "gs", "ps")}
    kp_stack = vp_stack = ks_stack = vs_stack = None
    for l in range(depth):
        prm = _layer_params(l, norm_g, pool_w, pool_scale, gla_w_a2, gla_b_a, gla_norm, fox_b_f,
                            fox_q_norm, fox_k_norm, w_branch_a, w_branch_b, w_branch_c, b_merge, w_out)

        p_all, f_all, p_s, f_s = _inproj(yp, ys, prm["norm_g"], w_packed, l)

        a_o = _pool_branch(p_all, prm["pool_w"], prm["pool_scale"], seq_len=seq)
        b_o, s_fin = _gla_branch(p_all, f_all, prm["wa2"], prm["ba"], prm["gn"], zero_state,
                                 n_seq=bp, seq_len=seq, decode=False)
        q_aug, k_aug, kp_stack, vp_stack, lf = _fox_prep(p_all, f_all, prm["qn"], prm["kn"], prm["bf"],
                                                         kp_stack, vp_stack, seq_len=seq, layer=l, depth=depth)
        c_o = _fox_flash(q_aug, k_aug, p_all, n_seq=bp, seq_len=seq)
        outs["lp"].append(lf[:, :C_HEADS].reshape(bp, seq, C_HEADS))
        outs["gp"].append(_state_from_kernel(s_fin))
        a_u = p_all[:, OFF_AU:OFF_AU + A_WIDTH].reshape(bp, seq, A_WIDTH)
        outs["pp"].append(a_u[:, seq - A_BUF:].astype(F32))

        pool_state = jnp.pad(state_pool[l], ((0, 0), (1, 0), (0, 0)))
        a_s = _pool_branch(p_s, prm["pool_w"], prm["pool_scale"], seq_len=dec_seq, state=pool_state)
        b_s, s_fin_s = _gla_branch(p_s, f_s, prm["wa2"], prm["ba"], prm["gn"], _state_to_kernel(state_gla[l]),
                                   n_seq=n_seq, seq_len=dec_seq, decode=True)
        q_aug_s, _, ks_stack, vs_stack, lf_s = _fox_prep(p_s, f_s, prm["qn"], prm["kn"], prm["bf"],
                                                         ks_stack, vs_stack, seq_len=dec_seq, layer=l, depth=depth)
        k_ns, v_ns = ks_stack[l], vs_stack[l]
        lfn_t = jnp.swapaxes(lf_s.reshape(n_seq, dec_seq, LANES)[:, :, :SUBLANES], 1, 2)
        lfn_t = jnp.pad(lfn_t, ((0, 0), (0, 0), (0, LANES - dec_seq)))
        c_s = _fox_decode(page_table, q_aug_s, cache_k_hm, cache_v_hm, logf_hp, lfn_t, k_ns, v_ns, p_s,
                          layer=l, dec_seq=dec_seq)
        yp, ys = _merge(yp, a_o, b_o, c_o, p_all, ys, a_s, b_s, c_s, p_s,
                        prm["b_merge"], prm["wa"], prm["wb"], prm["wc"], prm["wo"])
        outs["ls"].append(lf_s[:, :C_HEADS].reshape(n_seq, dec_seq, C_HEADS))
        outs["gs"].append(_state_from_kernel(s_fin_s))
        a_us = p_s[:, OFF_AU:OFF_AU + A_WIDTH].reshape(n_seq, dec_seq, A_WIDTH).astype(F32)
        outs["ps"].append(jnp.concatenate([state_pool[l], a_us], axis=1)[:, -A_BUF:])

    st = lambda k: jnp.stack(outs[k])
    hm = lambda a: jnp.swapaxes(a, 2, 3)
    return (yp.reshape(bp, seq, d), ys.reshape(n_seq, dec_seq, d),
            hm(kp_stack), hm(vp_stack), st("lp"), st("gp"), st("pp"),
            hm(ks_stack), hm(vs_stack), st("ls"), st("gs"), st("ps"))
```

```python
import functools
import math

import jax
import jax.numpy as jnp
from jax import lax
from jax.experimental import pallas as pl
from jax.experimental.pallas import tpu as pltpu

F32 = jnp.float32
BF16 = jnp.bfloat16
HIGHEST = lax.Precision.HIGHEST

D_MODEL = 2048
A_WIDTH = 512
A_WINDOWS = (2, 4, 8, 16)
A_GROUPS = 4
A_GROUP_DIM = 128
A_BUF = 15
B_HEADS = 4
B_DK = 96
B_DV = 192
B_GATE_RANK = 16
B_GATE_TEMP = 16.0
B_CHUNK = 32
C_HEADS = 6
C_HEAD_DIM = 128
C_WIDTH = 768
N_BRANCH = 3
EPS = 1e-6
PAGE_SIZE = 128
SPLIT_SIZES = (512, 512, 384, 384, 768, 768, 16, 768, 768, 768, 768, 6, 6144)

LANES = 128
SUBLANES = 8
VMEM_LIMIT_BYTES = 56 * 1024 * 1024

B_DKP = 128
B_DVP = 256
GLA_SPAN = 128

OFF_MG = 0
OFF_CG = 6144
OFF_CQ = OFF_CG + C_WIDTH
OFF_CK = OFF_CQ + C_WIDTH
OFF_CV = OFF_CK + C_WIDTH
OFF_AU = OFF_CV + C_WIDTH
OFF_AG = OFF_AU + A_WIDTH
OFF_BQ = OFF_AG + A_WIDTH
OFF_BK = OFF_BQ + B_HEADS * B_DKP
OFF_BV = OFF_BK + B_HEADS * B_DKP
OFF_BG = OFF_BV + B_HEADS * B_DVP
OFF_SM = OFF_BG + B_HEADS * B_DVP
SM_WIDTH = 2 * LANES
PROJ_TN = 1536
NP = 13824
assert OFF_SM + SM_WIDTH <= NP and NP % PROJ_TN == 0
SM_TILE = OFF_SM // PROJ_TN
SM_LOCAL = OFF_SM - SM_TILE * PROJ_TN
assert SM_LOCAL + SM_WIDTH <= PROJ_TN


def _pick_tile(n, pref, align=SUBLANES):
    if n <= pref:
        return n
    t = (pref // align) * align
    while t > align and n % t:
        t -= align
    assert n % t == 0, (n, pref)
    return t


def _cparams(*sem):
    return pltpu.CompilerParams(dimension_semantics=sem, vmem_limit_bytes=VMEM_LIMIT_BYTES)


def _log_sigmoid(x):
    return jnp.minimum(x, 0.0) - jnp.log1p(jnp.exp(-jnp.abs(x)))


def _silu(x):
    return x * jax.nn.sigmoid(x)


def _iota(shape, dim):
    return lax.broadcasted_iota(jnp.int32, shape, dim)


def _inproj_kernel(x_ref, xs_ref, g_ref, w_ref, p_ref, f_ref, ps_ref, fs_ref, h_scr, hs_scr, *, nj, rchunk):
    i = pl.program_id(0)
    j = pl.program_id(1)
    last_i = pl.num_programs(0) - 1

    def rms(x):
        r = lax.rsqrt(jnp.mean(x * x, axis=-1, keepdims=True) + EPS)
        return ((x * r) * g_ref[...]).astype(BF16)

    @pl.when(j == 0)
    def _():
        def body(c, carry):
            r0 = pl.multiple_of(c * rchunk, rchunk)
            h_scr[pl.ds(r0, rchunk), :] = rms(x_ref[pl.ds(r0, rchunk), :])
            return carry

        lax.fori_loop(0, x_ref.shape[0] // rchunk, body, 0)

        @pl.when(i == last_i)
        def _():
            hs_scr[...] = rms(xs_ref[...])

    dims = (((1,), (1,)), ((), ()))
    acc = lax.dot_general(h_scr[...], w_ref[...], dims, preferred_element_type=F32)
    p_ref[...] = acc.astype(p_ref.dtype)

    @pl.when(j == nj - 1)
    def _():
        f_ref[...] = acc[:, SM_LOCAL:SM_LOCAL + SM_WIDTH]

    @pl.when(i == last_i)
    def _():
        acc_s = lax.dot_general(hs_scr[...], w_ref[...], dims, preferred_element_type=F32)
        ps_ref[...] = acc_s

        @pl.when(j == nj - 1)
        def _():
            fs_ref[...] = acc_s[:, SM_LOCAL:SM_LOCAL + SM_WIDTH]


def _inproj(x2, xs2, norm_g, w_packed, layer):
    rows = x2.shape[0]
    rows_s = xs2.shape[0]
    tm = _pick_tile(rows, 1024)
    ni = rows // tm
    nj = NP // PROJ_TN
    assert SM_TILE == nj - 1
    rchunk = _pick_tile(tm, 128)
    return pl.pallas_call(
        functools.partial(_inproj_kernel, nj=nj, rchunk=rchunk),
        grid=(ni, nj),
        in_specs=[
            pl.BlockSpec((tm, D_MODEL), lambda i, j: (i, 0)),
            pl.BlockSpec((rows_s, D_MODEL), lambda i, j: (0, 0)),
            pl.BlockSpec((1, D_MODEL), lambda i, j: (0, 0)),
            pl.BlockSpec((None, PROJ_TN, D_MODEL), lambda i, j: (layer, j, 0)),
        ],
        out_specs=[
            pl.BlockSpec((tm, PROJ_TN), lambda i, j: (i, j)),
            pl.BlockSpec((tm, SM_WIDTH), lambda i, j: (i, 0)),
            pl.BlockSpec((rows_s, PROJ_TN), lambda i, j: (0, jnp.where(i == ni - 1, j, 0))),
            pl.BlockSpec((rows_s, SM_WIDTH), lambda i, j: (0, 0)),
        ],
        out_shape=[
            jax.ShapeDtypeStruct((rows, NP), BF16),
            jax.ShapeDtypeStruct((rows, SM_WIDTH), F32),
            jax.ShapeDtypeStruct((rows_s, NP), F32),
            jax.ShapeDtypeStruct((rows_s, SM_WIDTH), F32),
        ],
        scratch_shapes=[pltpu.VMEM((tm, D_MODEL), BF16), pltpu.VMEM((rows_s, D_MODEL), BF16)],
        compiler_params=_cparams("arbitrary", "arbitrary"),
        name="inproj",
    )(x2, xs2, norm_g.reshape(1, D_MODEL), w_packed)


def _pool_kernel(u_ref, g_ref, halo_ref, pw_ref, sc_ref, o_ref, band_scr, *, tiles_per_seq, decode):
    i = pl.program_id(0)
    tr = u_ref.shape[0]
    hr = halo_ref.shape[0]
    mm_dt = F32 if decode else BF16
    u = u_ref[...].astype(F32)
    halo = halo_ref[...].astype(F32)
    if not decode:
        halo = jnp.where(i % tiles_per_seq == 0, 0.0, halo)
    ext = jnp.concatenate([halo, u], axis=0).astype(mm_dt)

    @pl.when(i == 0)
    def _():
        r = _iota((tr, hr + tr), 0)
        c = _iota((tr, hr + tr), 1) - hr
        for g, w in enumerate(A_WINDOWS):
            band_scr[g] = ((c >= r - (w - 1)) & (c <= r)).astype(mm_dt)

    rowpos = _iota((tr, 1), 0)
    pos = ((i % tiles_per_seq) * tr + rowpos).astype(F32)
    outs = []
    for g, w in enumerate(A_WINDOWS):
        lo, hi = g * A_GROUP_DIM, (g + 1) * A_GROUP_DIM
        band = band_scr[g]
        if decode:
            win_sum = jnp.dot(band, ext[:, lo:hi], preferred_element_type=F32, precision=HIGHEST)
            count = float(w)
        else:
            win_sum = jnp.dot(band, ext[:, lo:hi], preferred_element_type=F32)
            count = jnp.minimum(pos + 1.0, float(w))
        pooled = win_sum / count - u[:, lo:hi]
        outs.append(jnp.dot(pooled.astype(mm_dt), pw_ref[g].astype(mm_dt), preferred_element_type=F32))
    mixed = jnp.concatenate(outs, axis=1) * sc_ref[...]
    o_ref[...] = (mixed * _silu(g_ref[...].astype(F32))).astype(o_ref.dtype)


def _pool_branch(p_all, pool_w, pool_scale, *, seq_len, state=None):
    rows = p_all.shape[0]
    decode = state is not None
    wdt = F32 if decode else BF16
    if decode:
        tr, tps = seq_len, 1
        hb = 16
        halo_spec = pl.BlockSpec((None, hb, A_WIDTH), lambda i: (i, 0, 0))
        halo_arr = state
    else:
        tr = _pick_tile(seq_len, 512, LANES)
        tps = seq_len // tr
        hb = LANES
        assert tr % hb == 0
        halo_spec = pl.BlockSpec((hb, A_WIDTH), lambda i: (jnp.maximum(i * (tr // hb) - 1, 0), OFF_AU // A_WIDTH))
        halo_arr = p_all
    return pl.pallas_call(
        functools.partial(_pool_kernel, tiles_per_seq=tps, decode=decode),
        grid=(rows // tr,),
        in_specs=[
            pl.BlockSpec((tr, A_WIDTH), lambda i: (i, OFF_AU // A_WIDTH)),
            pl.BlockSpec((tr, A_WIDTH), lambda i: (i, OFF_AG // A_WIDTH)),
            halo_spec,
            pl.BlockSpec((A_GROUPS, A_GROUP_DIM, A_GROUP_DIM), lambda i: (0, 0, 0)),
            pl.BlockSpec((1, A_WIDTH), lambda i: (0, 0)),
        ],
        out_specs=pl.BlockSpec((tr, A_WIDTH), lambda i: (i, 0)),
        out_shape=jax.ShapeDtypeStruct((rows, A_WIDTH), p_all.dtype),
        scratch_shapes=[pltpu.VMEM((A_GROUPS, tr, hb + tr), wdt)],
        compiler_params=_cparams("arbitrary"),
        name="pool_decode" if decode else "pool_prompt",
    )(p_all, p_all, halo_arr, pool_w.astype(wdt), pool_scale.reshape(1, A_WIDTH))


def _gla_kernel(q_ref, k_ref, v_ref, g_ref, r_ref, wa_ref, ba_ref, gn_ref, s0_ref,
                o_ref, so_ref, s_scr, *, chunk, span, decode):
    tb = pl.program_id(1)
    nb = pl.num_programs(1)
    tbk = q_ref.shape[0]
    mm_dt = F32 if decode else BF16

    @pl.when(tb == 0)
    def _():
        s_scr[...] = s0_ref[...]

    logit = jnp.dot(r_ref[...], wa_ref[...], preferred_element_type=F32, precision=HIGHEST) + ba_ref[...]
    log_a = _log_sigmoid(logit) / B_GATE_TEMP
    nsub = span // chunk
    tri = (_iota((span, span), 0) >= _iota((span, span), 1))
    scale = B_DK ** -0.5
    kw = log_a.shape[1]
    nt_dims = (((1,), (1,)), ((), ()))
    if not decode:
        la_hi = log_a.astype(BF16)
        la_r1 = log_a - la_hi.astype(F32)
        la_mid = la_r1.astype(BF16)
        la_lo = (la_r1 - la_mid.astype(F32)).astype(BF16)
        la_terms = jnp.concatenate([la_hi, la_mid, la_lo], axis=1)
        tri_b = tri.astype(BF16)
    zero_blk = jnp.zeros((chunk, B_DKP), mm_dt)
    for c in range(tbk // span):
        rs = slice(c * span, (c + 1) * span)
        if decode:
            b = jnp.dot(tri.astype(F32), log_a[rs], preferred_element_type=F32, precision=HIGHEST)
        else:
            b3 = jnp.dot(tri_b, la_terms[rs], preferred_element_type=F32)
            b = (b3[:, :kw] + b3[:, kw:2 * kw]) + b3[:, 2 * kw:]
        ends = [b[(i + 1) * chunk - 1:(i + 1) * chunk, :] for i in range(nsub)]
        starts = [jnp.zeros_like(ends[0])] + ends[:-1]
        b_last = ends[-1]
        rows_of = lambda vecs: jnp.concatenate([jnp.broadcast_to(v, (chunk, kw)) for v in vecs], axis=0)
        r_rows = rows_of(starts)
        q_in = (q_ref[rs, :].astype(F32) * scale) * jnp.exp(b - r_rows)
        k_in = k_ref[rs, :].astype(F32) * jnp.exp(r_rows - b)
        q_t = q_in * rows_of([jnp.exp(v) for v in starts])
        k_end = k_in * rows_of([jnp.exp(b_last - v) for v in starts])
        decay = jnp.exp(b_last)
        for h in range(B_HEADS):
            ks = slice(h * B_DKP, (h + 1) * B_DKP)
            vs = slice(h * B_DVP, (h + 1) * B_DVP)
            q_blocks, k_blocks = [], []
            for i in range(nsub):
                qi = q_in[i * chunk:(i + 1) * chunk, ks]
                kj = k_in[i * chunk:(i + 1) * chunk, ks].astype(mm_dt)
                q_blocks.append(jnp.concatenate(
                    [(qi * jnp.exp(starts[i][:, ks] - starts[j][:, ks])).astype(mm_dt) for j in range(i)]
                    + [qi.astype(mm_dt)] + [zero_blk] * (nsub - 1 - i), axis=1))
                k_blocks.append(jnp.concatenate([kj if j == i else zero_blk for j in range(nsub)], axis=1))
            att = lax.dot_general(jnp.concatenate(q_blocks, axis=0), jnp.concatenate(k_blocks, axis=0), nt_dims,
                                  preferred_element_type=F32)
            att = jnp.where(tri, att, 0.0)
            vh = v_ref[rs, vs].astype(mm_dt)
            st = s_scr[h]
            o = jnp.dot(att.astype(mm_dt), vh, preferred_element_type=F32)
            o = o + lax.dot_general(q_t[:, ks].astype(mm_dt), st.astype(mm_dt), nt_dims,
                                    preferred_element_type=F32)
            kv_t = lax.dot_general(vh, k_end[:, ks].astype(mm_dt), (((0,), (0,)), ((), ())),
                                   preferred_element_type=F32)
            s_scr[h] = st * decay[:, ks] + kv_t
            r = lax.rsqrt(jnp.sum(o * o, axis=-1, keepdims=True) * (1.0 / B_DV) + EPS)
            on = (o * r) * gn_ref[:, vs]
            o_ref[rs, vs] = (on * _silu(g_ref[rs, vs].astype(F32))).astype(o_ref.dtype)

    @pl.when(tb == nb - 1)
    def _():
        so_ref[...] = s_scr[...]


def _gla_branch(p_all, f_all, wa_p, ba_p, gn_p, s0_t, *, n_seq, seq_len, decode):
    rows = p_all.shape[0]
    chunk = math.gcd(seq_len, B_CHUNK)
    tbk = _pick_tile(seq_len, 512)
    nb = seq_len // tbk
    span = GLA_SPAN if (not decode and tbk % GLA_SPAN == 0 and GLA_SPAN % chunk == 0) else chunk
    kw = B_HEADS * B_DKP
    vw = B_HEADS * B_DVP
    return pl.pallas_call(
        functools.partial(_gla_kernel, chunk=chunk, span=span, decode=decode),
        grid=(n_seq, nb),
        in_specs=[
            pl.BlockSpec((tbk, kw), lambda s, t: (s * nb + t, OFF_BQ // kw)),
            pl.BlockSpec((tbk, kw), lambda s, t: (s * nb + t, OFF_BK // kw)),
            pl.BlockSpec((tbk, vw), lambda s, t: (s * nb + t, OFF_BV // vw)),
            pl.BlockSpec((tbk, vw), lambda s, t: (s * nb + t, OFF_BG // vw)),
            pl.BlockSpec((tbk, LANES), lambda s, t: (s * nb + t, 0)),
            pl.BlockSpec((LANES, kw), lambda s, t: (0, 0)),
            pl.BlockSpec((1, kw), lambda s, t: (0, 0)),
            pl.BlockSpec((1, vw), lambda s, t: (0, 0)),
            pl.BlockSpec((None, B_HEADS, B_DVP, B_DKP), lambda s, t: (s, 0, 0, 0)),
        ],
        out_specs=[
            pl.BlockSpec((tbk, vw), lambda s, t: (s * nb + t, 0)),
            pl.BlockSpec((None, B_HEADS, B_DVP, B_DKP), lambda s, t: (s, 0, 0, 0)),
        ],
        out_shape=[
            jax.ShapeDtypeStruct((rows, vw), p_all.dtype),
            jax.ShapeDtypeStruct((n_seq, B_HEADS, B_DVP, B_DKP), F32),
        ],
        scratch_shapes=[pltpu.VMEM((B_HEADS, B_DVP, B_DKP), F32)],
        compiler_params=_cparams("arbitrary", "arbitrary"),
        name="gla_decode" if decode else "gla_prompt",
    )(p_all, p_all, p_all, p_all, f_all, wa_p, ba_p, gn_p, s0_t)


def _foxprep_kernel(cq_ref, ck_ref, cv_ref, cf_ref, qn_ref, kn_ref, bf_ref, *rest, tiles_per_seq, first_layer):
    if first_layer:
        qa_ref, ka_ref, ko_ref, vo_ref, lf_ref, carry_scr = rest
    else:
        qa_ref, ka_ref, ko_ref, vo_ref, lf_ref, carry_scr = rest[2:]
    i = pl.program_id(0)
    tr = cq_ref.shape[0]
    hd = C_HEAD_DIM
    lane = _iota((tr, LANES), 1)

    lf = jnp.where(lane < C_HEADS, _log_sigmoid(cf_ref[...] + bf_ref[...]), 0.0)
    lf_ref[...] = lf

    @pl.when(i % tiles_per_seq == 0)
    def _():
        carry_scr[...] = jnp.zeros_like(carry_scr)

    tri = _iota((tr, tr), 0) >= _iota((tr, tr), 1)
    if tr % LANES == 0:
        hi = lf.astype(BF16)
        r1 = lf - hi.astype(F32)
        mid = r1.astype(BF16)
        lo = (r1 - mid.astype(F32)).astype(BF16)
        c3 = jnp.dot(tri.astype(BF16), jnp.concatenate([hi, mid, lo], axis=1), preferred_element_type=F32)
        cum = (c3[:, :LANES] + c3[:, LANES:2 * LANES]) + c3[:, 2 * LANES:]
    else:
        cum = jnp.dot(tri.astype(F32), lf, preferred_element_type=F32, precision=HIGHEST)
    cum = cum + carry_scr[...]
    carry_scr[...] = cum[tr - 1:tr, :]

    def qk_norm(x, g_ref):
        r = lax.rsqrt(jnp.mean(x * x, axis=-1, keepdims=True) + EPS)
        return (x * r) * g_ref[...]

    one = jnp.ones((tr, LANES), F32)
    zero = jnp.zeros((tr, LANES), F32)
    for h in range(C_HEADS):
        cs = slice(h * hd, (h + 1) * hd)
        ch = jnp.sum(jnp.where(lane == h, cum, 0.0), axis=-1, keepdims=True)
        hi = ch.astype(BF16).astype(F32)
        r1 = ch - hi
        mid = r1.astype(BF16).astype(F32)
        lo = r1 - mid
        qn = qk_norm(cq_ref[:, cs].astype(F32), qn_ref)
        kn = qk_norm(ck_ref[:, cs].astype(F32), kn_ref)
        aux_q = jnp.where(lane == 0, hi, jnp.where(lane == 1, mid, jnp.where(lane == 2, lo,
                          jnp.where(lane < 6, one, zero))))
        aux_k = jnp.where(lane < 3, one, jnp.where(lane == 3, -hi, jnp.where(lane == 4, -mid,
                          jnp.where(lane == 5, -lo, zero))))
        a0 = 2 * h * hd
        qa_ref[:, a0:a0 + hd] = (qn * (hd ** -0.5)).astype(qa_ref.dtype)
        qa_ref[:, a0 + hd:a0 + 2 * hd] = aux_q.astype(qa_ref.dtype)
        ka_ref[:, a0:a0 + hd] = kn.astype(ka_ref.dtype)
        ka_ref[:, a0 + hd:a0 + 2 * hd] = aux_k.astype(ka_ref.dtype)
        vn = cv_ref[:, cs].astype(F32)
        if first_layer:
            for l2 in range(ko_ref.shape[0]):
                ko_ref[l2, h] = kn
                vo_ref[l2, h] = vn
        else:
            ko_ref[h] = kn
            vo_ref[h] = vn


def _fox_prep(p_all, f_all, fox_q_norm, fox_k_norm, bf_p, k_stack, v_stack, *, seq_len, layer, depth):
    rows = p_all.shape[0]
    n_seq = rows // seq_len
    tr = _pick_tile(seq_len, 512)
    tps = seq_len // tr
    hd = C_HEAD_DIM
    cw = C_WIDTH
    aw = C_HEADS * 2 * hd
    first = k_stack is None
    stack_shape = (depth, n_seq, C_HEADS, seq_len, hd)
    if first:
        kv_spec = pl.BlockSpec((depth, None, C_HEADS, tr, hd), lambda i: (0, i // tps, 0, i % tps, 0))
        extra_specs, extra_args, aliases = [], [], {}
    else:
        kv_spec = pl.BlockSpec((None, None, C_HEADS, tr, hd), lambda i: (layer, i // tps, 0, i % tps, 0))
        extra_specs = [pl.BlockSpec(memory_space=pl.ANY)] * 2
        extra_args, aliases = [k_stack, v_stack], {7: 2, 8: 3}
    return pl.pallas_call(
        functools.partial(_foxprep_kernel, tiles_per_seq=tps, first_layer=first),
        grid=(rows // tr,),
        in_specs=[
            pl.BlockSpec((tr, cw), lambda i: (i, OFF_CQ // cw)),
            pl.BlockSpec((tr, cw), lambda i: (i, OFF_CK // cw)),
            pl.BlockSpec((tr, cw), lambda i: (i, OFF_CV // cw)),
            pl.BlockSpec((tr, LANES), lambda i: (i, 1)),
            pl.BlockSpec((1, hd), lambda i: (0, 0)),
            pl.BlockSpec((1, hd), lambda i: (0, 0)),
            pl.BlockSpec((1, LANES), lambda i: (0, 0)),
        ] + extra_specs,
        out_specs=[
            pl.BlockSpec((tr, aw), lambda i: (i, 0)),
            pl.BlockSpec((tr, aw), lambda i: (i, 0)),
            kv_spec,
            kv_spec,
            pl.BlockSpec((tr, LANES), lambda i: (i, 0)),
        ],
        out_shape=[
            jax.ShapeDtypeStruct((rows, aw), p_all.dtype),
            jax.ShapeDtypeStruct((rows, aw), p_all.dtype),
            jax.ShapeDtypeStruct(stack_shape, F32),
            jax.ShapeDtypeStruct(stack_shape, F32),
            jax.ShapeDtypeStruct((rows, LANES), F32),
        ],
        input_output_aliases=aliases,
        scratch_shapes=[pltpu.VMEM((1, LANES), F32)],
        compiler_params=_cparams("arbitrary"),
        name="fox_prep",
    )(p_all, p_all, p_all, f_all, fox_q_norm.reshape(1, hd), fox_k_norm.reshape(1, hd), bf_p, *extra_args)


def _flash_kernel(q_ref, k_ref, v_ref, g_ref, o_ref, m_scr, l_scr, acc_scr, *, rs):
    i = pl.program_id(2)
    t = q_ref.shape[0]
    hd = C_HEAD_DIM
    m_scr[...] = jnp.full_like(m_scr, -jnp.inf)
    l_scr[...] = jnp.zeros_like(l_scr)
    acc_scr[...] = jnp.zeros_like(acc_scr)
    ones = jnp.ones((t, LANES), BF16)

    def step(j, diagonal):
        k0 = pl.multiple_of(j * t, t)
        kb = k_ref[pl.ds(k0, t), :]
        vb = jnp.concatenate([v_ref[pl.ds(k0, t), :], ones], axis=1)
        for r in range(t // rs):
            rows = slice(r * rs, (r + 1) * rs)
            nk = (r + 1) * rs if diagonal else t
            s = lax.dot_general(q_ref[rows, :], kb[:nk], (((1,), (1,)), ((), ())), preferred_element_type=F32)
            if diagonal:
                s = jnp.where(_iota((rs, nk), 0) + r * rs >= _iota((rs, nk), 1), s, -jnp.inf)
            m_prev = m_scr[rows, :]
            m_new = jnp.maximum(m_prev, jnp.max(s, axis=-1, keepdims=True))
            alpha = jnp.exp(m_prev - m_new)
            p = jnp.exp((s - jnp.concatenate([m_new] * (nk // LANES), axis=1)).astype(BF16))
            pv = jnp.dot(p, vb[:nk], preferred_element_type=F32)
            l_scr[rows, :] = alpha * l_scr[rows, :] + pv[:, hd:]
            acc_scr[rows, :] = alpha * acc_scr[rows, :] + pv[:, :hd]
            m_scr[rows, :] = m_new

    def pair(jj, carry):
        step(2 * jj, False)
        step(2 * jj + 1, False)
        return carry

    lax.fori_loop(0, i // 2, pair, 0)

    @pl.when(i % 2 == 1)
    def _():
        step(i - 1, False)

    step(i, True)
    out = acc_scr[...] / l_scr[...]
    o_ref[...] = (out * _silu(g_ref[...].astype(F32))).astype(BF16)


def _fox_flash(q_aug, k_aug, p_all, *, n_seq, seq_len):
    rows = q_aug.shape[0]
    t = _pick_tile(seq_len, 1024, LANES)
    rs = _pick_tile(t, 256, LANES)
    nt = seq_len // t
    hd = C_HEAD_DIM
    aw = 2 * hd
    return pl.pallas_call(
        functools.partial(_flash_kernel, rs=rs),
        grid=(n_seq, C_HEADS, nt),
        in_specs=[
            pl.BlockSpec((t, aw), lambda b, h, i: (b * nt + i, h)),
            pl.BlockSpec((seq_len, aw), lambda b, h, i: (b, h)),
            pl.BlockSpec((seq_len, hd), lambda b, h, i: (b, OFF_CV // hd + h)),
            pl.BlockSpec((t, hd), lambda b, h, i: (b * nt + i, OFF_CG // hd + h)),
        ],
        out_specs=pl.BlockSpec((t, hd), lambda b, h, i: (b * nt + i, h)),
        out_shape=jax.ShapeDtypeStruct((rows, C_WIDTH), BF16),
        scratch_shapes=[pltpu.VMEM((t, LANES), F32), pltpu.VMEM((t, LANES), F32), pltpu.VMEM((t, hd), F32)],
        compiler_params=_cparams("parallel", "parallel", "arbitrary"),
        name="fox_flash",
    )(q_aug, k_aug, p_all, p_all)


def _forget_sums(pt_ref, s, lf_ref, lfn_ref, ck_scr, ckn_scr):
    npg = ck_scr.shape[1]

    def gather(p, carry):
        page = pt_ref[s, p]
        for h in range(C_HEADS):
            ck_scr[h, pl.ds(p, 1), :] = lf_ref[h, pl.ds(page, 1), :]
        return carry

    lax.fori_loop(0, npg, gather, 0)
    upper = (_iota((LANES, LANES), 0) <= _iota((LANES, LANES), 1)).astype(F32)
    earlier = (_iota((npg, npg), 1) < _iota((npg, npg), 0)).astype(F32)
    lfn_cum = jnp.dot(lfn_ref[...], upper, preferred_element_type=F32, precision=HIGHEST)
    ckn_scr[...] = jnp.zeros_like(ckn_scr)
    for h in range(C_HEADS):
        cum = jnp.dot(ck_scr[h], upper, preferred_element_type=F32, precision=HIGHEST)
        tot = jnp.broadcast_to(cum[:, LANES - 1:LANES], cum.shape)
        before = jnp.dot(earlier, tot, preferred_element_type=F32, precision=HIGHEST)
        ck_scr[h] = cum + before
        ckn_scr[h:h + 1, :] = lfn_cum[h:h + 1, :] + (before[npg - 1:npg, :] + tot[npg - 1:npg, :])


DEC_QROWS = 16


def _decode_kernel(pt_ref, qa_ref, *refs, gpp, dec_seq):
    k_refs = refs[:gpp]
    v_refs = refs[gpp:2 * gpp]
    (lf_ref, lfn_ref, kn_ref, vn_ref, g_ref, o_ref,
     q_scr, m_scr, l_scr, acc_scr, ck_scr, ckn_scr) = refs[2 * gpp:]
    t = pl.program_id(1)
    nt = pl.num_programs(1)
    nq = dec_seq
    qp = DEC_QROWS
    nrow = C_HEADS * qp
    hd = C_HEAD_DIM
    row = _iota((nrow, LANES), 0)
    lane = _iota((nrow, LANES), 1)
    qi = jnp.bitwise_and(row, qp - 1)

    @pl.when(t == 0)
    def _():
        qa = qa_ref[...].astype(F32)
        zero = jnp.zeros((qp - nq, hd), F32)
        q_scr[...] = jnp.concatenate(
            [jnp.concatenate([qa[:, h * 2 * hd:h * 2 * hd + hd], zero], axis=0) for h in range(C_HEADS)],
            axis=0).astype(BF16)
        m_scr[...] = jnp.full_like(m_scr, -jnp.inf)
        l_scr[...] = jnp.zeros_like(l_scr)
        acc_scr[...] = jnp.zeros_like(acc_scr)
        _forget_sums(pt_ref, pl.program_id(0), lf_ref, lfn_ref, ck_scr, ckn_scr)

    def head_rows(get_row):
        return jnp.concatenate(
            [jnp.broadcast_to(get_row(h), (qp, get_row(h).shape[1])) for h in range(C_HEADS)], axis=0)

    ckn_rows = head_rows(lambda h: ckn_scr[h:h + 1, :])
    page0 = pl.multiple_of(t * gpp, gpp)

    def ck_row(h):
        blk = ck_scr[h, pl.ds(page0, gpp), :]
        return jnp.concatenate([blk[g:g + 1, :] for g in range(gpp)], axis=1)
    c_q = jnp.sum(jnp.where(lane == qi, ckn_rows, 0.0), axis=-1, keepdims=True)

    def attend(k_of, v_of, ck_rows, mask):
        s = jnp.concatenate(
            [lax.dot_general(q_scr[h * qp:(h + 1) * qp, :], k_of(h), (((1,), (1,)), ((), ())),
                             preferred_element_type=F32) for h in range(C_HEADS)], axis=0)
        s = s + (c_q - ck_rows)
        if mask is not None:
            s = jnp.where(mask, s, -jnp.inf)
        m_prev = m_scr[...]
        m_new = jnp.maximum(m_prev, jnp.max(s, axis=-1, keepdims=True))
        alpha = jnp.exp(m_prev - m_new)
        pr = jnp.exp(s - m_new)
        l_scr[...] = alpha * l_scr[...] + jnp.sum(pr, axis=-1, keepdims=True)
        prb = pr.astype(BF16)
        o = jnp.concatenate(
            [jnp.dot(prb[h * qp:(h + 1) * qp, :], v_of(h), preferred_element_type=F32) for h in range(C_HEADS)],
            axis=0)
        acc_scr[...] = alpha * acc_scr[...] + o
        m_scr[...] = m_new

    attend(lambda h: jnp.concatenate([k_refs[g][h] for g in range(gpp)], axis=0).astype(BF16),
           lambda h: jnp.concatenate([v_refs[g][h] for g in range(gpp)], axis=0).astype(BF16),
           head_rows(ck_row), None)

    @pl.when(t == nt - 1)
    def _():
        pad = jnp.zeros((PAGE_SIZE - nq, hd), F32)
        attend(lambda h: jnp.concatenate([kn_ref[h], pad], axis=0).astype(BF16),
               lambda h: jnp.concatenate([vn_ref[h], pad], axis=0).astype(BF16),
               ckn_rows, lane <= qi)
        out = acc_scr[...] / l_scr[...]
        gate = _silu(g_ref[...].astype(F32))
        for h in range(C_HEADS):
            cs = slice(h * hd, (h + 1) * hd)
            o_ref[:, cs] = (out[h * qp:h * qp + nq, :] * gate[:, cs]).astype(o_ref.dtype)


def _fox_decode(page_table, q_aug, cache_k_hm, cache_v_hm, logf_hp, lfn_t, k_new, v_new, p_all, *, layer, dec_seq):
    n_seq, npg = page_table.shape
    n_pool = logf_hp.shape[2]
    rows = q_aug.shape[0]
    nrow = C_HEADS * DEC_QROWS
    gpp = _pick_tile(npg, 16, SUBLANES)

    def page_spec(g):
        return pl.BlockSpec((None, None, C_HEADS, PAGE_SIZE, C_HEAD_DIM),
                            lambda s, t, pt: (layer, pt[s, t * gpp + g], 0, 0, 0))

    return pl.pallas_call(
        functools.partial(_decode_kernel, gpp=gpp, dec_seq=dec_seq),
        grid_spec=pltpu.PrefetchScalarGridSpec(
            num_scalar_prefetch=1,
            grid=(n_seq, npg // gpp),
            in_specs=[pl.BlockSpec((dec_seq, C_HEADS * 2 * C_HEAD_DIM), lambda s, t, pt: (s, 0))]
            + [page_spec(g) for g in range(gpp)]
            + [page_spec(g) for g in range(gpp)]
            + [
                pl.BlockSpec((None, C_HEADS, n_pool, PAGE_SIZE), lambda s, t, pt: (layer, 0, 0, 0)),
                pl.BlockSpec((None, SUBLANES, LANES), lambda s, t, pt: (s, 0, 0)),
                pl.BlockSpec((None, C_HEADS, dec_seq, C_HEAD_DIM), lambda s, t, pt: (s, 0, 0, 0)),
                pl.BlockSpec((None, C_HEADS, dec_seq, C_HEAD_DIM), lambda s, t, pt: (s, 0, 0, 0)),
                pl.BlockSpec((dec_seq, C_WIDTH), lambda s, t, pt: (s, OFF_CG // C_WIDTH)),
            ],
            out_specs=pl.BlockSpec((dec_seq, C_WIDTH), lambda s, t, pt: (s, 0)),
            scratch_shapes=[
                pltpu.VMEM((nrow, C_HEAD_DIM), BF16),
                pltpu.VMEM((nrow, 1), F32),
                pltpu.VMEM((nrow, 1), F32),
                pltpu.VMEM((nrow, C_HEAD_DIM), F32),
                pltpu.VMEM((C_HEADS, npg, PAGE_SIZE), F32),
                pltpu.VMEM((SUBLANES, LANES), F32),
            ],
        ),
        out_shape=jax.ShapeDtypeStruct((rows, C_WIDTH), p_all.dtype),
        compiler_params=_cparams("arbitrary", "arbitrary"),
        name="fox_decode",
    )(page_table, q_aug, *([cache_k_hm] * gpp), *([cache_v_hm] * gpp), logf_hp, lfn_t, k_new, v_new, p_all)


def _merge_kernel(x_ref, a_ref, b_ref, c_ref, mg0_ref, mg1_ref, mg2_ref,
                  xs_ref, as_ref, bs_ref, cs_ref, mgs0_ref, mgs1_ref, mgs2_ref,
                  bm_ref, wa_ref, wb_ref, wc_ref, wo_ref, y_ref, ys_ref):
    d = D_MODEL

    def gated(o_ref, w_ref, mg_ref, k):
        gate = jax.nn.sigmoid(mg_ref[...].astype(F32) + bm_ref[:, k * d:(k + 1) * d])
        return gate * jnp.dot(o_ref[...].astype(BF16), w_ref[...], preferred_element_type=F32)

    def layer_out(xr, ar, br, cr, m0, m1, m2):
        merged = gated(ar, wa_ref, m0, 0) + gated(br, wb_ref, m1, 1) + gated(cr, wc_ref, m2, 2)
        return xr[...] + jnp.dot(merged.astype(BF16), wo_ref[...], preferred_element_type=F32)

    y_ref[...] = layer_out(x_ref, a_ref, b_ref, c_ref, mg0_ref, mg1_ref, mg2_ref)

    @pl.when(pl.program_id(0) == pl.num_programs(0) - 1)
    def _():
        ys_ref[...] = layer_out(xs_ref, as_ref, bs_ref, cs_ref, mgs0_ref, mgs1_ref, mgs2_ref)


def _merge(x2, a_o, b_o, c_o, p_all, xs2, a_s, b_s, c_s, p_s, b_merge, wa, wb, wc, wo):
    rows = x2.shape[0]
    rows_s = xs2.shape[0]
    tm = _pick_tile(rows, 256)
    d = D_MODEL
    bw = B_HEADS * B_DVP

    def const(shape, col=0):
        return pl.BlockSpec(shape, lambda i: (0, col), pipeline_mode=pl.Buffered(1))

    return pl.pallas_call(
        _merge_kernel,
        grid=(rows // tm,),
        in_specs=[
            pl.BlockSpec((tm, d), lambda i: (i, 0)),
            pl.BlockSpec((tm, A_WIDTH), lambda i: (i, 0)),
            pl.BlockSpec((tm, bw), lambda i: (i, 0)),
            pl.BlockSpec((tm, C_WIDTH), lambda i: (i, 0)),
            pl.BlockSpec((tm, d), lambda i: (i, OFF_MG // d + 0)),
            pl.BlockSpec((tm, d), lambda i: (i, OFF_MG // d + 1)),
            pl.BlockSpec((tm, d), lambda i: (i, OFF_MG // d + 2)),
            const((rows_s, d)),
            const((rows_s, A_WIDTH)),
            const((rows_s, bw)),
            const((rows_s, C_WIDTH)),
            const((rows_s, d), OFF_MG // d + 0),
            const((rows_s, d), OFF_MG // d + 1),
            const((rows_s, d), OFF_MG // d + 2),
            const((1, N_BRANCH * d)),
            const((A_WIDTH, d)),
            const((bw, d)),
            const((C_WIDTH, d)),
            const((d, d)),
        ],
        out_specs=[pl.BlockSpec((tm, d), lambda i: (i, 0)), pl.BlockSpec((rows_s, d), lambda i: (0, 0))],
        out_shape=[jax.ShapeDtypeStruct((rows, d), F32), jax.ShapeDtypeStruct((rows_s, d), F32)],
        compiler_params=_cparams("arbitrary"),
        name="merge_out",
    )(x2, a_o, b_o, c_o, p_all, p_all, p_all, xs2, a_s, b_s, c_s, p_s, p_s, p_s,
      b_merge.reshape(1, N_BRANCH * d), wa, wb, wc, wo)


def _pad_heads(x, n_heads, d, dp, axis):
    shp = x.shape
    x = x.reshape(shp[:axis] + (n_heads, d) + shp[axis + 1:])
    pad = [(0, 0)] * x.ndim
    pad[axis + 1] = (0, dp - d)
    x = jnp.pad(x, pad)
    return x.reshape(shp[:axis] + (n_heads * dp,) + shp[axis + 1:])


def _pack_plan():
    src = {}
    acc = 0
    for name, size in zip(("a_u", "a_g", "b_q", "b_k", "b_v", "b_g", "b_r", "c_q", "c_k", "c_v", "c_g", "c_f", "m_g"),
                          SPLIT_SIZES):
        src[name] = acc
        acc += size
    d_in = acc
    segs = [(OFF_MG, "m_g", 1, 6144, 6144), (OFF_CG, "c_g", 1, C_WIDTH, C_WIDTH), (OFF_CQ, "c_q", 1, C_WIDTH, C_WIDTH),
            (OFF_CK, "c_k", 1, C_WIDTH, C_WIDTH), (OFF_CV, "c_v", 1, C_WIDTH, C_WIDTH),
            (OFF_AU, "a_u", 1, A_WIDTH, A_WIDTH), (OFF_AG, "a_g", 1, A_WIDTH, A_WIDTH),
            (OFF_BQ, "b_q", B_HEADS, B_DK, B_DKP), (OFF_BK, "b_k", B_HEADS, B_DK, B_DKP),
            (OFF_BV, "b_v", B_HEADS, B_DV, B_DVP), (OFF_BG, "b_g", B_HEADS, B_DV, B_DVP),
            (OFF_SM, "b_r", 1, B_GATE_RANK, LANES), (OFF_SM + LANES, "c_f", 1, C_HEADS, LANES)]
    starts = [0] * (NP // LANES)
    nvalid = [0] * (NP // LANES)
    for off, name, heads, dreal, dpad in segs:
        for h in range(heads):
            for b in range(dpad // LANES):
                blk = (off + h * dpad) // LANES + b
                n = max(0, min(LANES, dreal - b * LANES))
                s = src[name] + h * dreal + b * LANES
                assert s + LANES <= d_in
                starts[blk], nvalid[blk] = (s if n else 0), n
    return starts, nvalid


PACK_WINDOWS = 4


def _packw_kernel(st_ref, nv_ref, *refs):
    del st_ref
    w_refs, o_ref = refs[:PACK_WINDOWS], refs[PACK_WINDOWS]
    j = pl.program_id(0)
    row = _iota((LANES, D_MODEL), 0)
    for g in range(PACK_WINDOWS):
        live = row < nv_ref[j * PACK_WINDOWS + g]
        for l in range(o_ref.shape[0]):
            o_ref[l, g * LANES:(g + 1) * LANES, :] = jnp.where(live, w_refs[g][:, l, :], 0.0).astype(BF16)


def _pack_w_in(w_in):
    depth = w_in.shape[0]
    w_t = jnp.transpose(w_in, (2, 0, 1))
    starts, nvalid = _pack_plan()
    nblk = NP // LANES
    assert nblk % PACK_WINDOWS == 0

    def window(g):
        return pl.BlockSpec((pl.Element(LANES), pl.Element(depth), pl.Element(D_MODEL)),
                            lambda j, st, nv: (st[j * PACK_WINDOWS + g], 0, 0))

    return pl.pallas_call(
        _packw_kernel,
        grid_spec=pltpu.PrefetchScalarGridSpec(
            num_scalar_prefetch=2,
            grid=(nblk // PACK_WINDOWS,),
            in_specs=[window(g) for g in range(PACK_WINDOWS)],
            out_specs=pl.BlockSpec((depth, PACK_WINDOWS * LANES, D_MODEL), lambda j, st, nv: (0, j, 0)),
        ),
        out_shape=jax.ShapeDtypeStruct((depth, NP, D_MODEL), BF16),
        compiler_params=_cparams("parallel"),
        name="pack_w_in",
    )(jnp.asarray(starts, jnp.int32), jnp.asarray(nvalid, jnp.int32), *([w_t] * PACK_WINDOWS))


def _layer_params(l, norm_g, pool_w, pool_scale, gla_w_a2, gla_b_a, gla_norm, fox_b_f, fox_q_norm,
                  fox_k_norm, w_branch_a, w_branch_b, w_branch_c, b_merge, w_out):
    wa2 = _pad_heads(gla_w_a2[l], B_HEADS, B_DK, B_DKP, 1)
    return dict(
        norm_g=norm_g[l],
        pool_w=pool_w[l],
        pool_scale=pool_scale[l],
        wa2=jnp.pad(wa2, ((0, LANES - B_GATE_RANK), (0, 0))),
        ba=_pad_heads(gla_b_a[l].reshape(1, -1), B_HEADS, B_DK, B_DKP, 1),
        gn=_pad_heads(gla_norm[l].reshape(1, -1), B_HEADS, B_DV, B_DVP, 1),
        bf=jnp.pad(fox_b_f[l].reshape(1, -1), ((0, 0), (0, LANES - C_HEADS))),
        qn=fox_q_norm[l],
        kn=fox_k_norm[l],
        wa=w_branch_a[l].astype(BF16),
        wb=_pad_heads(w_branch_b[l], B_HEADS, B_DV, B_DVP, 0).astype(BF16),
        wc=w_branch_c[l].astype(BF16),
        b_merge=b_merge[l],
        wo=w_out[l].astype(BF16),
    )


def _state_to_kernel(s):
    st = jnp.swapaxes(s, -1, -2)
    return jnp.pad(st, ((0, 0), (0, 0), (0, B_DVP - B_DV), (0, B_DKP - B_DK)))


def _state_from_kernel(st):
    return jnp.swapaxes(st[:, :, :B_DV, :B_DK], -1, -2)


def kernel(x_prompt, x_sample, cache_k, cache_v, cache_logf, state_gla, state_pool, page_table, norm_g, w_in, pool_w, pool_scale, gla_w_a2, gla_b_a, gla_norm, fox_b_f, fox_q_norm, fox_k_norm, w_branch_a, w_branch_b, w_branch_c, b_merge, w_out):
    depth = w_in.shape[0]
    bp, seq, d = x_prompt.shape
    n_seq, dec_seq, _ = x_sample.shape
    assert d == D_MODEL and dec_seq == SUBLANES and seq % LANES == 0
    yp = x_prompt.reshape(bp * seq, d)
    ys = x_sample.reshape(n_seq * dec_seq, d)
    cache_k_hm = jnp.transpose(cache_k, (0, 1, 3, 2, 4))
    cache_v_hm = jnp.transpose(cache_v, (0, 1, 3, 2, 4))
    logf_hp = jnp.transpose(cache_logf, (0, 3, 1, 2))
    zero_state = jnp.zeros((bp, B_HEADS, B_DVP, B_DKP), F32)
    w_packed = _pack_w_in(w_in)
    outs = {k: [] for k in ("lp", "gp", "pp", "ls", "gs", "ps")}
    kp_stack = vp_stack = ks_stack = vs_stack = None
    for l in range(depth):
        prm = _layer_params(l, norm_g, pool_w, pool_scale, gla_w_a2, gla_b_a, gla_norm, fox_b_f,
                            fox_q_norm, fox_k_norm, w_branch_a, w_branch_b, w_branch_c, b_merge, w_out)

        p_all, f_all, p_s, f_s = _inproj(yp, ys, prm["norm_g"], w_packed, l)

        a_o = _pool_branch(p_all, prm["pool_w"], prm["pool_scale"], seq_len=seq)
        b_o, s_fin = _gla_branch(p_all, f_all, prm["wa2"], prm["ba"], prm["gn"], zero_state,
                                 n_seq=bp, seq_len=seq, decode=False)
        q_aug, k_aug, kp_stack, vp_stack, lf = _fox_prep(p_all, f_all, prm["qn"], prm["kn"], prm["bf"],
                                                         kp_stack, vp_stack, seq_len=seq, layer=l, depth=depth)
        c_o = _fox_flash(q_aug, k_aug, p_all, n_seq=bp, seq_len=seq)
        outs["lp"].append(lf[:, :C_HEADS].reshape(bp, seq, C_HEADS))
        outs["gp"].append(_state_from_kernel(s_fin))
        a_u = p_all[:, OFF_AU:OFF_AU + A_WIDTH].reshape(bp, seq, A_WIDTH)
        outs["pp"].append(a_u[:, seq - A_BUF:].astype(F32))

        pool_state = jnp.pad(state_pool[l], ((0, 0), (1, 0), (0, 0)))
        a_s = _pool_branch(p_s, prm["pool_w"], prm["pool_scale"], seq_len=dec_seq, state=pool_state)
        b_s, s_fin_s = _gla_branch(p_s, f_s, prm["wa2"], prm["ba"], prm["gn"], _state_to_kernel(state_gla[l]),
                                   n_seq=n_seq, seq_len=dec_seq, decode=True)
        q_aug_s, _, ks_stack, vs_stack, lf_s = _fox_prep(p_s, f_s, prm["qn"], prm["kn"], prm["bf"],
                                                         ks_stack, vs_stack, seq_len=dec_seq, layer=l, depth=depth)
        k_ns, v_ns = ks_stack[l], vs_stack[l]
        lfn_t = jnp.swapaxes(lf_s.reshape(n_seq, dec_seq, LANES)[:, :, :SUBLANES], 1, 2)
        lfn_t = jnp.pad(lfn_t, ((0, 0), (0, 0), (0, LANES - dec_seq)))
        c_s = _fox_decode(page_table, q_aug_s, cache_k_hm, cache_v_hm, logf_hp, lfn_t, k_ns, v_ns, p_s,
                          layer=l, dec_seq=dec_seq)
        yp, ys = _merge(yp, a_o, b_o, c_o, p_all, ys, a_s, b_s, c_s, p_s,
                        prm["b_merge"], prm["wa"], prm["wb"], prm["wc"], prm["wo"])
        outs["ls"].append(lf_s[:, :C_HEADS].reshape(n_seq, dec_seq, C_HEADS))
        outs["gs"].append(_state_from_kernel(s_fin_s))
        a_us = p_s[:, OFF_AU:OFF_AU + A_WIDTH].reshape(n_seq, dec_seq, A_WIDTH).astype(F32)
        outs["ps"].append(jnp.concatenate([state_pool[l], a_us], axis=1)[:, -A_BUF:])

    st = lambda k: jnp.stack(outs[k])
    hm = lambda a: jnp.swapaxes(a, 2, 3)
    return (yp.reshape(bp, seq, d), ys.reshape(n_seq, dec_seq, d),
            hm(kp_stack), hm(vp_stack), st("lp"), st("gp"), st("pp"),
            hm(ks_stack), hm(vs_stack), st("ls"), st("gs"), st("ps"))
```
